```python
import math
import jax, jax.numpy as jnp
from jax import lax
import numpy as np

D_MODEL = 1024
BATCH = 8
SEQ = 2048
DEPTH = 1
DEC_BATCH = 8
DEC_SEQ = 64
PAST_LEN = 2048

CHUNK = 64
QBLOCK = 128
RMS_EPS = 1e-6
SSM_GROUPS = 32
SSM_GROUP_CH = 16
SSM_WIDTH = SSM_GROUPS * SSM_GROUP_CH
SSM_STATE = 64
DT_MIN = 0.001
DT_MAX = 0.1
MLA_HEADS = 8
QK_NOPE = 64
QK_ROPE = 32
QK_HEAD = QK_NOPE + QK_ROPE
V_HEAD = 64
Q_LORA = 384
KV_LORA = 256
ROPE_BASE = 10000.0
MLA_WIDTH = MLA_HEADS * V_HEAD
MEM_TOKENS = 256
MEM_HEADS = 4
MEM_HEAD = 128
MEM_WIDTH = MEM_HEADS * MEM_HEAD
N_BRANCHES = 3
IN_WIDTH = SSM_WIDTH + Q_LORA + KV_LORA + QK_ROPE + MEM_WIDTH + N_BRANCHES * D_MODEL
N_EXPERT_GROUPS = 4
EXPERTS_PER_GROUP = 8
N_EXPERTS = N_EXPERT_GROUPS * EXPERTS_PER_GROUP
TOP_K_IN_GROUP = 2
EXPERT_FF = 256

kernel_name = "hybrid_s5_mla_memxattn_hiermoe_stream_step"


def rmsnorm(x, g):
    xf = x.astype(jnp.float32)
    xf = xf * lax.rsqrt(jnp.mean(xf * xf, axis=-1, keepdims=True) + RMS_EPS)
    return (xf * g.astype(jnp.float32)).astype(x.dtype)


def apply_rope(x, pos):
    half = x.shape[-1] // 2
    inv = ROPE_BASE ** (-jnp.arange(half, dtype=jnp.float32) / half)
    ang = pos.astype(jnp.float32)[:, None] * inv[None, :]
    cos = jnp.cos(ang)[None, :, None, :]
    sin = jnp.sin(ang)[None, :, None, :]
    xf = x.astype(jnp.float32)
    x1, x2 = xf[..., :half], xf[..., half:]
    return jnp.concatenate([x1 * cos - x2 * sin, x2 * cos + x1 * sin], axis=-1).astype(x.dtype)


def _complex_affine_combine(earlier, later):
    a1r, a1i, b1r, b1i = earlier
    a2r, a2i, b2r, b2i = later
    ar = a1r * a2r - a1i * a2i
    ai = a1r * a2i + a1i * a2r
    br = a2r * b1r - a2i * b1i + b2r
    bi = a2r * b1i + a2i * b1r + b2i
    return (ar, ai, br, bi)


def s5_branch(u, h0_re, h0_im, p):
    b_, l_, _ = u.shape
    uf = u.astype(jnp.float32).reshape(b_, l_, SSM_GROUPS, SSM_GROUP_CH)
    dt = jnp.exp(p["ssm_log_dt"].astype(jnp.float32))[:, None]
    lr = p["ssm_a_re"].astype(jnp.float32)
    li = p["ssm_a_im"].astype(jnp.float32)
    mag = jnp.exp(lr * dt)
    ab_re, ab_im = mag * jnp.cos(li * dt), mag * jnp.sin(li * dt)
    den = lr * lr + li * li
    nr, ni = ab_re - 1.0, ab_im
    f_re = (nr * lr + ni * li) / den
    f_im = (ni * lr - nr * li) / den
    br = p["ssm_b_re"].astype(jnp.float32)
    bi = p["ssm_b_im"].astype(jnp.float32)
    bb_re = f_re[..., None] * br - f_im[..., None] * bi
    bb_im = f_re[..., None] * bi + f_im[..., None] * br
    bu_re = jnp.einsum('blgc,gpc->blgp', uf, bb_re)
    bu_im = jnp.einsum('blgc,gpc->blgp', uf, bb_im)
    a_re_b = jnp.broadcast_to(ab_re, bu_re.shape)
    a_im_b = jnp.broadcast_to(ab_im, bu_re.shape)
    acc_re, acc_im, s_re, s_im = lax.associative_scan(
        _complex_affine_combine, (a_re_b, a_im_b, bu_re, bu_im), axis=1)
    h0r = h0_re.astype(jnp.float32)[:, None]
    h0i = h0_im.astype(jnp.float32)[:, None]
    h_re = s_re + acc_re * h0r - acc_im * h0i
    h_im = s_im + acc_re * h0i + acc_im * h0r
    y = (jnp.einsum('blgp,gcp->blgc', h_re, p["ssm_c_re"].astype(jnp.float32))
         - jnp.einsum('blgp,gcp->blgc', h_im, p["ssm_c_im"].astype(jnp.float32)))
    y = y.reshape(b_, l_, SSM_WIDTH) + p["ssm_d"].astype(jnp.float32) * uf.reshape(b_, l_, SSM_WIDTH)
    z = jax.nn.gelu(y).astype(u.dtype)
    zz = z @ p["w_glu"]
    out = zz[..., :D_MODEL] * jax.nn.sigmoid(zz[..., D_MODEL:])
    return out, h_re[:, -1], h_im[:, -1]


def mla_queries(q_lat, pos, p):
    q = rmsnorm(q_lat, p["g_qlat"]) @ p["w_uq"]
    q = q.reshape(q.shape[0], q.shape[1], MLA_HEADS, QK_HEAD)
    q = rmsnorm(q, p["g_qn"])
    return jnp.concatenate([q[..., :QK_NOPE], apply_rope(q[..., QK_NOPE:], pos)], axis=-1)


def mla_keys_values(c_kv, k_pe, pos, p):
    b_, s_, _ = c_kv.shape
    kv = (c_kv @ p["w_ukv"]).reshape(b_, s_, MLA_HEADS, QK_NOPE + V_HEAD)
    k_nope, v = kv[..., :QK_NOPE], kv[..., QK_NOPE:]
    kpe = jnp.broadcast_to(k_pe[:, :, None, :], (b_, s_, MLA_HEADS, QK_ROPE))
    k = rmsnorm(jnp.concatenate([k_nope, kpe], axis=-1), p["g_kn"])
    k = jnp.concatenate([k[..., :QK_NOPE], apply_rope(k[..., QK_NOPE:], pos)], axis=-1)
    return k, v


def attention_blocked(q, k, v):
    b_, l_, h_, _ = q.shape
    scale = 1.0 / math.sqrt(QK_HEAD)
    k_chunk = jnp.arange(l_) // CHUNK

    def one_block(i):
        qb = lax.dynamic_slice_in_dim(q, i * QBLOCK, QBLOCK, axis=1)
        s = jnp.einsum('bqhd,bkhd->bhqk', qb, k).astype(jnp.float32) * scale
        q_chunk = (i * QBLOCK + jnp.arange(QBLOCK)) // CHUNK
        mask = k_chunk[None, :] <= q_chunk[:, None]
        s = jnp.where(mask[None, None], s, -1e30)
        pr = jax.nn.softmax(s, axis=-1).astype(v.dtype)
        return jnp.einsum('bhqk,bkhd->bqhd', pr, v)

    out = lax.map(one_block, jnp.arange(l_ // QBLOCK))
    return out.transpose(1, 0, 2, 3, 4).reshape(b_, l_, h_, v.shape[-1])


def attention_with_past(q, k, v, q_pos, k_pos):
    scale = 1.0 / math.sqrt(QK_HEAD)
    s = jnp.einsum('bqhd,bkhd->bhqk', q, k).astype(jnp.float32) * scale
    mask = (k_pos // CHUNK)[None, :] <= (q_pos // CHUNK)[:, None]
    s = jnp.where(mask[None, None], s, -1e30)
    pr = jax.nn.softmax(s, axis=-1).astype(v.dtype)
    return jnp.einsum('bhqk,bkhd->bqhd', pr, v)


def memory_kv(mem, p):
    b_, m_, _ = mem.shape
    kv = (rmsnorm(mem, p["g_mem"]) @ p["w_mem_kv"]).reshape(b_, m_, 2, MEM_HEADS, MEM_HEAD)
    k = rmsnorm(kv[:, :, 0], p["g_mkn"])
    return k, kv[:, :, 1]


def memory_attention(q_mem, mem_k, mem_v, p):
    b_, l_, _ = q_mem.shape
    q = rmsnorm(q_mem.reshape(b_, l_, MEM_HEADS, MEM_HEAD), p["g_mqn"])
    s = jnp.einsum('bqhd,bkhd->bhqk', q, mem_k).astype(jnp.float32) * (1.0 / math.sqrt(MEM_HEAD))
    pr = jax.nn.softmax(s, axis=-1).astype(mem_v.dtype)
    o = jnp.einsum('bhqk,bkhd->bqhd', pr, mem_v).reshape(b_, l_, MEM_WIDTH)
    return o @ p["w_o_mem"]


def hierarchical_moe(h, p):
    lg = (h @ p["w_rg"]).astype(jnp.float32) + p["b_rg"].astype(jnp.float32)
    pg = jax.nn.softmax(lg, axis=-1)
    g_onehot = jax.nn.one_hot(jnp.argmax(lg, axis=-1), N_EXPERT_GROUPS, dtype=jnp.float32)
    p_top = jnp.sum(pg * g_onehot, axis=-1, keepdims=True)
    le = (h @ p["w_re"]).astype(jnp.float32).reshape(*h.shape[:-1], N_EXPERT_GROUPS, EXPERTS_PER_GROUP)
    le = le + p["b_re"].astype(jnp.float32)
    le_sel = jnp.sum(le * g_onehot[..., None], axis=-2)
    tv, ti = lax.top_k(le_sel, TOP_K_IN_GROUP)
    qv = jax.nn.softmax(tv, axis=-1)
    w_grp = jnp.sum(jax.nn.one_hot(ti, EXPERTS_PER_GROUP, dtype=jnp.float32) * qv[..., None], axis=-2)
    gate = (g_onehot[..., :, None] * (p_top * w_grp)[..., None, :]).reshape(*h.shape[:-1], N_EXPERTS)
    gate = gate.astype(h.dtype)
    y = jnp.zeros_like(h)
    for e in range(N_EXPERTS):
        hid = jax.nn.silu(h @ p["w_e1"][e]) * (h @ p["w_e3"][e])
        y = y + gate[..., e:e + 1] * (hid @ p["w_e2"][e])
    return y


def layer_forward(x, pos, h0_re, h0_im, mem_k, mem_v, past_ckv, past_kpe, p):
    b_, l_, _ = x.shape
    h = rmsnorm(x, p["g_attn"])
    proj = h @ p["w_in"]
    o1 = SSM_WIDTH
    o2 = o1 + Q_LORA
    o3 = o2 + KV_LORA
    o4 = o3 + QK_ROPE
    o5 = o4 + MEM_WIDTH
    u, q_lat, kv_lat = proj[..., :o1], proj[..., o1:o2], proj[..., o2:o3]
    k_pe, q_mem, g_logit = proj[..., o3:o4], proj[..., o4:o5], proj[..., o5:]
    br_a, h_re, h_im = s5_branch(u, h0_re, h0_im, p)
    c_kv = rmsnorm(kv_lat, p["g_kvlat"])
    q = mla_queries(q_lat, pos, p)
    if past_ckv is None:
        k, v = mla_keys_values(c_kv, k_pe, pos, p)
        o = attention_blocked(q, k, v)
    else:
        k_pos = jnp.concatenate([jnp.arange(past_ckv.shape[1]), pos])
        k, v = mla_keys_values(jnp.concatenate([past_ckv, c_kv], axis=1),
                               jnp.concatenate([past_kpe, k_pe], axis=1), k_pos, p)
        o = attention_with_past(q, k, v, pos, k_pos)
    br_b = o.reshape(b_, l_, MLA_WIDTH) @ p["w_o_mla"]
    br_c = memory_attention(q_mem, mem_k, mem_v, p)
    gates = jax.nn.sigmoid(g_logit).reshape(b_, l_, N_BRANCHES, D_MODEL)
    merged = gates[..., 0, :] * br_a + gates[..., 1, :] * br_b + gates[..., 2, :] * br_c
    x = x + merged @ p["w_out"]
    x = x + hierarchical_moe(rmsnorm(x, p["g_ffn"]), p)
    return x, c_kv, k_pe, h_re.astype(x.dtype), h_im.astype(x.dtype)


def setup_inputs(seed: int = 0) -> dict:
    key = jax.random.key(seed)
    ks = iter(jax.random.split(key, 48))
    f32 = jnp.float32

    def nrm(shape, scale):
        return jax.random.normal(next(ks), shape, f32) * scale

    def gain(shape):
        return 1.0 + 0.05 * jax.random.normal(next(ks), shape, f32)

    n_idx = jnp.arange(SSM_STATE, dtype=f32)
    return {
        "x_prompt": nrm((BATCH, SEQ, D_MODEL), 1.0),
        "x_sample": nrm((DEC_BATCH, DEC_SEQ, D_MODEL), 1.0),
        "cache_mla_ckv": nrm((DEPTH, DEC_BATCH, PAST_LEN, KV_LORA), 1.0),
        "cache_mla_kpe": nrm((DEPTH, DEC_BATCH, PAST_LEN, QK_ROPE), 1.0),
        "cache_ssm_re": nrm((DEPTH, DEC_BATCH, SSM_GROUPS, SSM_STATE), 0.3),
        "cache_ssm_im": nrm((DEPTH, DEC_BATCH, SSM_GROUPS, SSM_STATE), 0.3),
        "cache_mem_k": nrm((DEPTH, DEC_BATCH, MEM_TOKENS, MEM_HEADS, MEM_HEAD), 1.0),
        "cache_mem_v": nrm((DEPTH, DEC_BATCH, MEM_TOKENS, MEM_HEADS, MEM_HEAD), 1.0),
        "mem_prompt": nrm((BATCH, MEM_TOKENS, D_MODEL), 1.0),
        "g_attn": gain((DEPTH, D_MODEL)),
        "w_in": nrm((DEPTH, D_MODEL, IN_WIDTH), D_MODEL ** -0.5),
        "ssm_a_re": -0.5 * jnp.exp(nrm((DEPTH, SSM_GROUPS, SSM_STATE), 0.01)),
        "ssm_a_im": math.pi * n_idx + nrm((DEPTH, SSM_GROUPS, SSM_STATE), 0.01),
        "ssm_log_dt": jax.random.uniform(next(ks), (DEPTH, SSM_GROUPS), f32,
                                         minval=math.log(DT_MIN), maxval=math.log(DT_MAX)),
        "ssm_b_re": nrm((DEPTH, SSM_GROUPS, SSM_STATE, SSM_GROUP_CH), (2.0 * SSM_GROUP_CH) ** -0.5),
        "ssm_b_im": nrm((DEPTH, SSM_GROUPS, SSM_STATE, SSM_GROUP_CH), (2.0 * SSM_GROUP_CH) ** -0.5),
        "ssm_c_re": nrm((DEPTH, SSM_GROUPS, SSM_GROUP_CH, SSM_STATE), (2.0 * SSM_STATE) ** -0.5),
        "ssm_c_im": nrm((DEPTH, SSM_GROUPS, SSM_GROUP_CH, SSM_STATE), (2.0 * SSM_STATE) ** -0.5),
        "ssm_d": nrm((DEPTH, SSM_WIDTH), 1.0),
        "w_glu": nrm((DEPTH, SSM_WIDTH, 2 * D_MODEL), SSM_WIDTH ** -0.5),
        "g_qlat": gain((DEPTH, Q_LORA)),
        "w_uq": nrm((DEPTH, Q_LORA, MLA_HEADS * QK_HEAD), Q_LORA ** -0.5),
        "g_kvlat": gain((DEPTH, KV_LORA)),
        "w_ukv": nrm((DEPTH, KV_LORA, MLA_HEADS * (QK_NOPE + V_HEAD)), KV_LORA ** -0.5),
        "g_qn": gain((DEPTH, QK_HEAD)),
        "g_kn": gain((DEPTH, QK_HEAD)),
        "w_o_mla": nrm((DEPTH, MLA_WIDTH, D_MODEL), MLA_WIDTH ** -0.5),
        "g_mem": gain((DEPTH, D_MODEL)),
        "w_mem_kv": nrm((DEPTH, D_MODEL, 2 * MEM_WIDTH), D_MODEL ** -0.5),
        "g_mqn": gain((DEPTH, MEM_HEAD)),
        "g_mkn": gain((DEPTH, MEM_HEAD)),
        "w_o_mem": nrm((DEPTH, MEM_WIDTH, D_MODEL), MEM_WIDTH ** -0.5),
        "w_out": nrm((DEPTH, D_MODEL, D_MODEL), D_MODEL ** -0.5),
        "g_ffn": gain((DEPTH, D_MODEL)),
        "w_rg": nrm((DEPTH, D_MODEL, N_EXPERT_GROUPS), D_MODEL ** -0.5),
        "b_rg": nrm((DEPTH, N_EXPERT_GROUPS), 0.01),
        "w_re": nrm((DEPTH, D_MODEL, N_EXPERTS), D_MODEL ** -0.5),
        "b_re": nrm((DEPTH, N_EXPERT_GROUPS, EXPERTS_PER_GROUP), 0.01),
        "w_e1": nrm((DEPTH, N_EXPERTS, D_MODEL, EXPERT_FF), D_MODEL ** -0.5),
        "w_e3": nrm((DEPTH, N_EXPERTS, D_MODEL, EXPERT_FF), D_MODEL ** -0.5),
        "w_e2": nrm((DEPTH, N_EXPERTS, EXPERT_FF, D_MODEL), EXPERT_FF ** -0.5),
    }


def reference(x_prompt, x_sample, cache_mla_ckv, cache_mla_kpe, cache_ssm_re, cache_ssm_im,
              cache_mem_k, cache_mem_v, mem_prompt, g_attn, w_in, ssm_a_re, ssm_a_im, ssm_log_dt,
              ssm_b_re, ssm_b_im, ssm_c_re, ssm_c_im, ssm_d, w_glu, g_qlat, w_uq, g_kvlat, w_ukv,
              g_qn, g_kn, w_o_mla, g_mem, w_mem_kv, g_mqn, g_mkn, w_o_mem, w_out, g_ffn,
              w_rg, b_rg, w_re, b_re, w_e1, w_e3, w_e2):
    xp, xs = x_prompt, x_sample
    pos_p = jnp.arange(xp.shape[1])
    pos_s = cache_mla_ckv.shape[2] + jnp.arange(xs.shape[1])
    p_ckv, p_kpe, p_sre, p_sim, p_mk, p_mv = [], [], [], [], [], []
    s_ckv, s_kpe, s_sre, s_sim = [], [], [], []
    for l in range(DEPTH):
        p = {
            "g_attn": g_attn[l], "w_in": w_in[l], "ssm_a_re": ssm_a_re[l], "ssm_a_im": ssm_a_im[l],
            "ssm_log_dt": ssm_log_dt[l], "ssm_b_re": ssm_b_re[l], "ssm_b_im": ssm_b_im[l],
            "ssm_c_re": ssm_c_re[l], "ssm_c_im": ssm_c_im[l], "ssm_d": ssm_d[l], "w_glu": w_glu[l],
            "g_qlat": g_qlat[l], "w_uq": w_uq[l], "g_kvlat": g_kvlat[l], "w_ukv": w_ukv[l],
            "g_qn": g_qn[l], "g_kn": g_kn[l], "w_o_mla": w_o_mla[l], "g_mem": g_mem[l],
            "w_mem_kv": w_mem_kv[l], "g_mqn": g_mqn[l], "g_mkn": g_mkn[l], "w_o_mem": w_o_mem[l],
            "w_out": w_out[l], "g_ffn": g_ffn[l], "w_rg": w_rg[l], "b_rg": b_rg[l],
            "w_re": w_re[l], "b_re": b_re[l], "w_e1": w_e1[l], "w_e3": w_e3[l], "w_e2": w_e2[l],
        }
        mk, mv = memory_kv(mem_prompt, p)
        h0 = jnp.zeros((xp.shape[0], SSM_GROUPS, SSM_STATE), jnp.float32)
        xp, ckv_p, kpe_p, sre_p, sim_p = layer_forward(xp, pos_p, h0, h0, mk, mv, None, None, p)
        xs, ckv_s, kpe_s, sre_s, sim_s = layer_forward(
            xs, pos_s, cache_ssm_re[l], cache_ssm_im[l], cache_mem_k[l], cache_mem_v[l],
            cache_mla_ckv[l], cache_mla_kpe[l], p)
        p_ckv.append(ckv_p); p_kpe.append(kpe_p); p_sre.append(sre_p); p_sim.append(sim_p)
        p_mk.append(mk); p_mv.append(mv)
        s_ckv.append(ckv_s); s_kpe.append(kpe_s); s_sre.append(sre_s); s_sim.append(sim_s)
    return (xp, xs, jnp.stack(p_ckv), jnp.stack(p_kpe), jnp.stack(p_sre), jnp.stack(p_sim),
            jnp.stack(p_mk), jnp.stack(p_mv), jnp.stack(s_ckv), jnp.stack(s_kpe),
            jnp.stack(s_sre), jnp.stack(s_sim))
```

```python
import functools
import math

import jax
import jax.numpy as jnp
import numpy as np
from jax import lax
from jax.experimental import pallas as pl
from jax.experimental.pallas import tpu as pltpu

D_MODEL = 1024
CHUNK = 64
RMS_EPS = 1e-6
SSM_GROUPS = 32
SSM_GROUP_CH = 16
SSM_WIDTH = SSM_GROUPS * SSM_GROUP_CH
SSM_STATE = 64
SSM_STATES = SSM_GROUPS * SSM_STATE
SSM_BLOCKS = 2
MLA_HEADS = 8
QK_NOPE = 64
QK_ROPE = 32
QK_HEAD = QK_NOPE + QK_ROPE
V_HEAD = 64
Q_LORA = 384
KV_LORA = 256
ROPE_BASE = 10000.0
MLA_WIDTH = MLA_HEADS * V_HEAD
MEM_HEADS = 4
MEM_HEAD = 128
MEM_WIDTH = MEM_HEADS * MEM_HEAD
N_BRANCHES = 3
N_EXPERT_GROUPS = 4
EXPERTS_PER_GROUP = 8
N_EXPERTS = N_EXPERT_GROUPS * EXPERTS_PER_GROUP
EXPERT_FF = 256

LANES = 128
SUBLANES = 8
HEAD_PAD = LANES
QK_PAD = MLA_HEADS * HEAD_PAD
VMEM_LIMIT = 56 * 1024 * 1024

BF16 = jnp.bfloat16
F32 = jnp.float32
NEG_INF = -1e30


def _cparams(sem):
    return pltpu.CompilerParams(dimension_semantics=sem, vmem_limit_bytes=VMEM_LIMIT)


def _const_spec(shape):
    nd = len(shape)
    return pl.BlockSpec(shape, lambda *_: (0,) * nd, pipeline_mode=pl.Buffered(1))


def _rms_scale(xf, width):
    return lax.rsqrt(jnp.sum(xf * xf, axis=-1, keepdims=True) * (1.0 / width) + RMS_EPS)


def _dot(a, b):
    return jnp.dot(a, b, preferred_element_type=F32)


def _dot_nt(a, b):
    return lax.dot_general(a, b, (((1,), (1,)), ((), ())), preferred_element_type=F32)


def _zoh_kernel(lr_ref, li_ref, ldt_ref, abr_ref, abi_ref, fr_ref, fi_ref):
    lr = lr_ref[...]
    li = li_ref[...]
    dt = jnp.exp(ldt_ref[...])
    mag = jnp.exp(lr * dt)
    ab_re = mag * jnp.cos(li * dt)
    ab_im = mag * jnp.sin(li * dt)
    den = lr * lr + li * li
    nr = ab_re - 1.0
    ni = ab_im
    abr_ref[...] = ab_re
    abi_ref[...] = ab_im
    fr_ref[...] = (nr * lr + ni * li) / den
    fi_ref[...] = (ni * lr - nr * li) / den


def _zoh(a_re, a_im, log_dt):
    shp = jax.ShapeDtypeStruct((SSM_GROUPS, SSM_STATE), F32)
    return pl.pallas_call(_zoh_kernel, out_shape=(shp, shp, shp, shp), name="zoh")(
        a_re, a_im, log_dt.reshape(SSM_GROUPS, 1))


def _expand_kv(ckv_bf, kpe_p, kpe_s, wk_ref, wv_ref, ak, bk):
    k_nope = _dot(ckv_bf, wk_ref[...])
    v = _dot(ckv_bf, wv_ref[...])
    ss_pe = jnp.sum(kpe_p * kpe_p, axis=-1, keepdims=True)
    rot = kpe_p * ak + kpe_s * bk
    heads = []
    for h in range(MLA_HEADS):
        kh = k_nope[:, h * HEAD_PAD:(h + 1) * HEAD_PAD]
        ss = jnp.sum(kh * kh, axis=-1, keepdims=True) + ss_pe
        rs = lax.rsqrt(ss * (1.0 / QK_HEAD) + RMS_EPS)
        heads.append((rs * (kh * ak + rot)).astype(BF16))
    return jnp.concatenate(heads, axis=-1), v.astype(BF16)


def _memkv_kernel(mem_ref, g_ref, w_ref, gk_ref, k_ref, v_ref):
    x = mem_ref[...]
    h = (x * _rms_scale(x, D_MODEL) * g_ref[...]).astype(BF16)
    kv = _dot(h, w_ref[...])
    gk = gk_ref[...]
    for hd in range(MEM_HEADS):
        kh = kv[:, hd * MEM_HEAD:(hd + 1) * MEM_HEAD]
        k_ref[:, hd * MEM_HEAD:(hd + 1) * MEM_HEAD] = kh * _rms_scale(kh, MEM_HEAD) * gk
    v_ref[...] = kv[:, MEM_WIDTH:]


def _memkv(mem2d, g_mem, w_mem_kv_bf, g_mkn):
    n = mem2d.shape[0]
    tm = 256
    out = jax.ShapeDtypeStruct((n, MEM_WIDTH), F32)
    return pl.pallas_call(
        _memkv_kernel,
        grid=(n // tm,),
        in_specs=[pl.BlockSpec((tm, D_MODEL), lambda i: (i, 0)),
                  _const_spec((1, D_MODEL)),
                  _const_spec((D_MODEL, 2 * MEM_WIDTH)),
                  _const_spec((1, MEM_HEAD))],
        out_specs=(pl.BlockSpec((tm, MEM_WIDTH), lambda i: (i, 0)),
                   pl.BlockSpec((tm, MEM_WIDTH), lambda i: (i, 0))),
        out_shape=(out, out),
        compiler_params=_cparams(("parallel",)),
        name="memkv",
    )(mem2d, g_mem.reshape(1, D_MODEL), w_mem_kv_bf, g_mkn.reshape(1, MEM_HEAD))


def _inproj_kernel(x_ref, mk_ref, mv_ref, aq_ref, bq_ref, ak_ref, bk_ref,
                   g_attn_ref, w_u_ref, w_q_ref, w_kv_ref, w_pe_ref, w_qm_ref, w_g_ref,
                   g_qlat_ref, wq_ref, wqs_ref, g_kvlat_ref, wk_ref, wv_ref, g_mqn_ref,
                   u_ref, q_ref, k_ref, v_ref, ckv_ref, kpe_ref, om_ref, gate_ref):
    x = x_ref[0]
    h = (x * _rms_scale(x, D_MODEL) * g_attn_ref[...]).astype(BF16)

    u_ref[...] = _dot(h, w_u_ref[...])
    gate_ref[0] = jax.nn.sigmoid(_dot(h, w_g_ref[...])).astype(BF16)

    q_lat = _dot(h, w_q_ref[...])
    qn = (q_lat * _rms_scale(q_lat, Q_LORA) * g_qlat_ref[...]).astype(BF16)
    q_up = _dot(qn, wq_ref[...])
    q_sw = _dot(qn, wqs_ref[...])
    aq = aq_ref[...]
    bq = bq_ref[...]
    for hd in range(MLA_HEADS):
        sl = slice(hd * HEAD_PAD, (hd + 1) * HEAD_PAD)
        qh = q_up[:, sl]
        rs = _rms_scale(qh, QK_HEAD)
        q_ref[0, :, sl] = (rs * (qh * aq + q_sw[:, sl] * bq)).astype(BF16)

    kv_lat = _dot(h, w_kv_ref[...])
    c_kv = kv_lat * _rms_scale(kv_lat, KV_LORA) * g_kvlat_ref[...]
    ckv_ref[0] = c_kv
    kpe_p = _dot(h, w_pe_ref[:, :HEAD_PAD])
    kpe_s = _dot(h, w_pe_ref[:, HEAD_PAD:])
    kpe_ref[0] = kpe_p[:, QK_NOPE:QK_HEAD]
    k_all, v_all = _expand_kv(c_kv.astype(BF16), kpe_p, kpe_s, wk_ref, wv_ref, ak_ref[...], bk_ref[...])
    k_ref[0] = k_all
    v_ref[0] = v_all

    q_mem = _dot(h, w_qm_ref[...])
    gq = g_mqn_ref[...] * (1.0 / math.sqrt(MEM_HEAD))
    for hd in range(MEM_HEADS):
        sl = slice(hd * MEM_HEAD, (hd + 1) * MEM_HEAD)
        qh = q_mem[:, sl]
        qh = (qh * _rms_scale(qh, MEM_HEAD) * gq).astype(BF16)
        s = _dot_nt(qh, mk_ref[0, :, sl])
        p = jnp.exp(s - jnp.max(s, axis=-1, keepdims=True))
        o = _dot(p.astype(BF16), mv_ref[0, :, sl])
        om_ref[0, :, sl] = (o / jnp.sum(p, axis=-1, keepdims=True)).astype(BF16)


def _inproj(x, mk_bf, mv_bf, tabs, wts, tl):
    b, l, _ = x.shape
    m = mk_bf.shape[1]
    aq, bq, ak, bk = tabs
    tok = lambda w: pl.BlockSpec((1, tl, w), lambda bi, li: (bi, li, 0))
    tab = pl.BlockSpec((tl, HEAD_PAD), lambda bi, li: (li, 0))
    memspec = pl.BlockSpec((1, m, MEM_WIDTH), lambda bi, li: (bi, 0, 0))
    in_specs = [tok(D_MODEL), memspec, memspec, tab, tab, tab, tab] + [_const_spec(w.shape) for w in wts]
    out_shape = (
        jax.ShapeDtypeStruct((l, b * SSM_WIDTH), F32),
        jax.ShapeDtypeStruct((b, l, QK_PAD), BF16),
        jax.ShapeDtypeStruct((b, l, QK_PAD), BF16),
        jax.ShapeDtypeStruct((b, l, MLA_WIDTH), BF16),
        jax.ShapeDtypeStruct((b, l, KV_LORA), F32),
        jax.ShapeDtypeStruct((b, l, QK_ROPE), F32),
        jax.ShapeDtypeStruct((b, l, MEM_WIDTH), BF16),
        jax.ShapeDtypeStruct((b, l, N_BRANCHES * D_MODEL), BF16),
    )
    out_specs = (
        pl.BlockSpec((tl, SSM_WIDTH), lambda bi, li: (li, bi)),
        tok(QK_PAD), tok(QK_PAD), tok(MLA_WIDTH), tok(KV_LORA), tok(QK_ROPE), tok(MEM_WIDTH),
        tok(N_BRANCHES * D_MODEL),
    )
    return pl.pallas_call(
        _inproj_kernel,
        grid=(b, l // tl),
        in_specs=in_specs,
        out_specs=out_specs,
        out_shape=out_shape,
        compiler_params=_cparams(("parallel", "parallel")),
        name="inproj",
    )(x, mk_bf, mv_bf, aq, bq, ak, bk, *wts)


def _kvexp_kernel(ckv_ref, kpe_p_ref, kpe_s_ref, ak_ref, bk_ref, wk_ref, wv_ref, k_ref, v_ref):
    k_all, v_all = _expand_kv(ckv_ref[0].astype(BF16), kpe_p_ref[0], kpe_s_ref[0], wk_ref, wv_ref,
                              ak_ref[...], bk_ref[...])
    k_ref[0] = k_all
    v_ref[0] = v_all


def _kvexp(ckv, kpe_p, kpe_s, ak, bk, wk, wv, tl):
    b, l, _ = ckv.shape
    tok = lambda w: pl.BlockSpec((1, tl, w), lambda bi, li: (bi, li, 0))
    tab = pl.BlockSpec((tl, HEAD_PAD), lambda bi, li: (li, 0))
    return pl.pallas_call(
        _kvexp_kernel,
        grid=(b, l // tl),
        in_specs=[tok(KV_LORA), tok(HEAD_PAD), tok(HEAD_PAD), tab, tab] + [_const_spec(w.shape) for w in (wk, wv)],
        out_specs=(tok(QK_PAD), tok(MLA_WIDTH)),
        out_shape=(jax.ShapeDtypeStruct((b, l, QK_PAD), BF16), jax.ShapeDtypeStruct((b, l, MLA_WIDTH), BF16)),
        compiler_params=_cparams(("parallel", "parallel")),
        name="kvexp",
    )(ckv, kpe_p, kpe_s, ak, bk, wk, wv)


SCAN_LANES = 1024


def _s5_kernel(u_ref, h0r_ref, h0i_ref, ar_ref, ai_ref, bmat_ref, cmat_ref, d_ref, wglu_ref,
               out_ref, hr_out_ref, hi_out_ref, sre_ref, sim_ref, hr_ref, hi_ref, *, batch):
    i = pl.program_id(0)
    rows = u_ref.shape[0]
    steps = rows // batch

    @pl.when(i == 0)
    def _():
        hr_ref[...] = h0r_ref[...]
        hi_ref[...] = h0i_ref[...]

    u = u_ref[...]
    ub = u.astype(BF16)
    blk_ch = SSM_WIDTH // SSM_BLOCKS
    blk_st = SSM_STATES // SSM_BLOCKS
    for blk in range(SSM_BLOCKS):
        bu = _dot(ub[:, blk * blk_ch:(blk + 1) * blk_ch], bmat_ref[blk])
        sre_ref[:, blk * blk_st:(blk + 1) * blk_st] = bu[:, :blk_st]
        sim_ref[:, blk * blk_st:(blk + 1) * blk_st] = bu[:, blk_st:]

    for c in range(SSM_STATES // SCAN_LANES):
        sl = slice(c * SCAN_LANES, (c + 1) * SCAN_LANES)
        a_re = ar_ref[:, sl]
        a_im = ai_ref[:, sl]

        def body(t, carry):
            h_re, h_im = carry
            r0 = pl.multiple_of(t * batch, batch)
            n_re = a_re * h_re - a_im * h_im + sre_ref[pl.ds(r0, batch), sl]
            n_im = a_re * h_im + a_im * h_re + sim_ref[pl.ds(r0, batch), sl]
            sre_ref[pl.ds(r0, batch), sl] = n_re
            sim_ref[pl.ds(r0, batch), sl] = n_im
            return n_re, n_im

        h_re, h_im = lax.fori_loop(0, steps, body, (hr_ref[:, sl], hi_ref[:, sl]), unroll=2)
        hr_ref[:, sl] = h_re
        hi_ref[:, sl] = h_im

    @pl.when(i == pl.num_programs(0) - 1)
    def _():
        hr_out_ref[...] = hr_ref[...]
        hi_out_ref[...] = hi_ref[...]

    ys = []
    for blk in range(SSM_BLOCKS):
        st = slice(blk * blk_st, (blk + 1) * blk_st)
        ys.append(_dot(sre_ref[:, st].astype(BF16), cmat_ref[0, blk])
                  - _dot(sim_ref[:, st].astype(BF16), cmat_ref[1, blk]))
    y = jnp.concatenate(ys, axis=-1) + d_ref[...] * u
    z = jax.nn.gelu(y).astype(BF16)
    zz = _dot(z, wglu_ref[...])
    out_ref[...] = (zz[:, :D_MODEL] * jax.nn.sigmoid(zz[:, D_MODEL:])).astype(BF16)


def _s5(u_tm, h0_re, h0_im, a_re8, a_im8, bmat, cmat, ssm_d, w_glu_bf, batch, rows):
    n = u_tm.shape[0]
    st = jax.ShapeDtypeStruct((batch, SSM_STATES), F32)
    return pl.pallas_call(
        functools.partial(_s5_kernel, batch=batch),
        grid=(n // rows,),
        in_specs=[pl.BlockSpec((rows, SSM_WIDTH), lambda i: (i, 0)),
                  _const_spec((batch, SSM_STATES)), _const_spec((batch, SSM_STATES)),
                  _const_spec((batch, SSM_STATES)), _const_spec((batch, SSM_STATES)),
                  _const_spec(bmat.shape), _const_spec(cmat.shape),
                  _const_spec((1, SSM_WIDTH)), _const_spec(w_glu_bf.shape)],
        out_specs=(pl.BlockSpec((rows, D_MODEL), lambda i: (i, 0)),
                   pl.BlockSpec((batch, SSM_STATES), lambda i: (0, 0)),
                   pl.BlockSpec((batch, SSM_STATES), lambda i: (0, 0))),
        out_shape=(jax.ShapeDtypeStruct((n, D_MODEL), BF16), st, st),
        scratch_shapes=[pltpu.VMEM((rows, SSM_STATES), F32), pltpu.VMEM((rows, SSM_STATES), F32),
                        pltpu.VMEM((batch, SSM_STATES), F32), pltpu.VMEM((batch, SSM_STATES), F32)],
        compiler_params=_cparams(("arbitrary",)),
        name="s5",
    )(u_tm, h0_re, h0_im, a_re8, a_im8, bmat, cmat, ssm_d.reshape(1, SSM_WIDTH), w_glu_bf)


def _attn_update(qh, kt, vt, m, l, acc, mask):
    s = _dot_nt(qh, kt)
    if mask is not None:
        s = jnp.where(mask, s, NEG_INF)
    m_new = jnp.maximum(m, jnp.max(s, axis=-1, keepdims=True))
    alpha = jnp.exp(m - m_new)
    p = jnp.exp(s - m_new)
    l = alpha * l + jnp.sum(p, axis=-1, keepdims=True)
    acc = alpha * acc + _dot(p.astype(BF16), vt)
    return m_new, l, acc


def _attn_causal_kernel(q_ref, k_ref, v_ref, o_ref, *, tq):
    i = pl.program_id(1)
    qc = lax.broadcasted_iota(jnp.int32, (tq, tq), 0) // CHUNK
    kc = lax.broadcasted_iota(jnp.int32, (tq, tq), 1) // CHUNK
    diag_mask = kc <= qc
    for hd in range(MLA_HEADS):
        ks = slice(hd * HEAD_PAD, (hd + 1) * HEAD_PAD)
        vs = slice(hd * V_HEAD, (hd + 1) * V_HEAD)
        qh = q_ref[0, :, ks]

        def body(j, carry):
            r0 = pl.multiple_of(j * tq, tq)
            return _attn_update(qh, k_ref[0, pl.ds(r0, tq), ks], v_ref[0, pl.ds(r0, tq), vs], *carry, None)

        init = (jnp.full((tq, 1), NEG_INF, F32), jnp.zeros((tq, 1), F32), jnp.zeros((tq, V_HEAD), F32))
        carry = lax.fori_loop(0, i, body, init)
        r0 = pl.multiple_of(i * tq, tq)
        _, l, acc = _attn_update(qh, k_ref[0, pl.ds(r0, tq), ks], v_ref[0, pl.ds(r0, tq), vs], *carry, diag_mask)
        o_ref[0, :, vs] = (acc / l).astype(BF16)


def _attn_causal(q, k, v, tq):
    b, l, _ = q.shape
    full = lambda w: pl.BlockSpec((1, l, w), lambda bi, qi: (bi, 0, 0))
    return pl.pallas_call(
        functools.partial(_attn_causal_kernel, tq=tq),
        grid=(b, l // tq),
        in_specs=[pl.BlockSpec((1, tq, QK_PAD), lambda bi, qi: (bi, qi, 0)), full(QK_PAD), full(MLA_WIDTH)],
        out_specs=pl.BlockSpec((1, tq, MLA_WIDTH), lambda bi, qi: (bi, qi, 0)),
        out_shape=jax.ShapeDtypeStruct((b, l, MLA_WIDTH), BF16),
        compiler_params=_cparams(("parallel", "arbitrary")),
        name="attn_prompt",
    )(q, k, v)


def _attn_past_kernel(q_ref, kp_ref, vp_ref, kn_ref, vn_ref, o_ref, *, tk):
    tq = q_ref.shape[1]
    n_past = kp_ref.shape[1] // tk
    for hd in range(MLA_HEADS):
        ks = slice(hd * HEAD_PAD, (hd + 1) * HEAD_PAD)
        vs = slice(hd * V_HEAD, (hd + 1) * V_HEAD)
        qh = q_ref[0, :, ks]

        def body(j, carry):
            r0 = pl.multiple_of(j * tk, tk)
            return _attn_update(qh, kp_ref[0, pl.ds(r0, tk), ks], vp_ref[0, pl.ds(r0, tk), vs], *carry, None)

        init = (jnp.full((tq, 1), NEG_INF, F32), jnp.zeros((tq, 1), F32), jnp.zeros((tq, V_HEAD), F32))
        carry = lax.fori_loop(0, n_past, body, init)
        _, l, acc = _attn_update(qh, kn_ref[0, :, ks], vn_ref[0, :, vs], *carry, None)
        o_ref[0, :, vs] = (acc / l).astype(BF16)


def _attn_past(q, k_past, v_past, k_new, v_new, tk):
    b, lq, _ = q.shape
    lp = k_past.shape[1]
    spec = lambda n, w: pl.BlockSpec((1, n, w), lambda bi: (bi, 0, 0))
    return pl.pallas_call(
        functools.partial(_attn_past_kernel, tk=tk),
        grid=(b,),
        in_specs=[spec(lq, QK_PAD), spec(lp, QK_PAD), spec(lp, MLA_WIDTH), spec(lq, QK_PAD), spec(lq, MLA_WIDTH)],
        out_specs=spec(lq, MLA_WIDTH),
        out_shape=jax.ShapeDtypeStruct((b, lq, MLA_WIDTH), BF16),
        compiler_params=_cparams(("parallel",)),
        name="attn_sample",
    )(q, k_past, v_past, k_new, v_new)


ROUTE_LANES = LANES


def _first_argmax(v, lane, width):
    vmax = jnp.max(v, axis=-1, keepdims=True)
    idx = jnp.min(jnp.where(v == vmax, lane, width), axis=-1, keepdims=True)
    return vmax, idx


def _merge_kernel(x_ref, bra_ref, o_ref, om_ref, gate_ref, w_omla_ref, w_omem_ref, w_out_ref, g_ffn_ref,
                  w_rt_ref, b_rt_ref, x1_ref, h2_ref, gmat_ref):
    br_b = _dot(o_ref[0], w_omla_ref[...])
    br_c = _dot(om_ref[0], w_omem_ref[...])
    g = gate_ref[0].astype(F32)
    merged = (g[:, :D_MODEL] * bra_ref[...].astype(F32) + g[:, D_MODEL:2 * D_MODEL] * br_b
              + g[:, 2 * D_MODEL:] * br_c)
    x1 = x_ref[0] + _dot(merged.astype(BF16), w_out_ref[...])
    x1_ref[0] = x1
    h2 = x1 * _rms_scale(x1, D_MODEL) * g_ffn_ref[...]
    h2_ref[0] = h2.astype(BF16)

    logits = jnp.dot(h2, w_rt_ref[...], precision=lax.Precision.HIGHEST, preferred_element_type=F32) + b_rt_ref[...]
    lane = lax.broadcasted_iota(jnp.int32, logits.shape, 1).astype(F32)
    ninf = jnp.float32(-jnp.inf)
    lg = jnp.where(lane < N_EXPERT_GROUPS, logits, ninf)
    lg_max, grp = _first_argmax(lg, lane, float(ROUTE_LANES))
    p_top = 1.0 / jnp.sum(jnp.exp(lg - lg_max), axis=-1, keepdims=True)
    lo = N_EXPERT_GROUPS + grp * EXPERTS_PER_GROUP
    in_grp = (lane >= lo) & (lane < lo + EXPERTS_PER_GROUP)
    le = jnp.where(in_grp, logits, ninf)
    v1, i1 = _first_argmax(le, lane, float(ROUTE_LANES))
    v2, i2 = _first_argmax(jnp.where(lane == i1, ninf, le), lane, float(ROUTE_LANES))
    e2 = jnp.exp(v2 - v1)
    w1 = p_top / (1.0 + e2)
    w2 = p_top * e2 / (1.0 + e2)
    elane = lane + N_EXPERT_GROUPS
    gmat_ref[0] = jnp.where(elane == i1, w1, 0.0) + jnp.where(elane == i2, w2, 0.0)


def _merge(x, bra_tm, o, om, gates, wts, tl):
    b, l, _ = x.shape
    tok = lambda w: pl.BlockSpec((1, tl, w), lambda bi, li: (bi, li, 0))
    in_specs = [tok(D_MODEL), pl.BlockSpec((tl, D_MODEL), lambda bi, li: (li, bi)), tok(MLA_WIDTH), tok(MEM_WIDTH),
                tok(N_BRANCHES * D_MODEL)] + [_const_spec(w.shape) for w in wts]
    return pl.pallas_call(
        _merge_kernel,
        grid=(b, l // tl),
        in_specs=in_specs,
        out_specs=(tok(D_MODEL), tok(D_MODEL), tok(ROUTE_LANES)),
        out_shape=(jax.ShapeDtypeStruct((b, l, D_MODEL), F32), jax.ShapeDtypeStruct((b, l, D_MODEL), BF16),
                   jax.ShapeDtypeStruct((b, l, ROUTE_LANES), F32)),
        compiler_params=_cparams(("parallel", "parallel")),
        name="merge",
    )(x, bra_tm, o, om, gates, *wts)


def _moe_kernel(x1_ref, h2_ref, gmat_ref, w13_ref, w2_ref, y_ref):
    e = pl.program_id(1)

    @pl.when(e == 0)
    def _():
        y_ref[...] = x1_ref[...]

    gm = gmat_ref[...]
    lane = lax.broadcasted_iota(jnp.int32, gm.shape, 1)
    ge = jnp.sum(jnp.where(lane == e, gm, 0.0), axis=-1, keepdims=True)

    @pl.when(jnp.max(ge) > 0.0)
    def _():
        h = h2_ref[...]
        a = _dot(h, w13_ref[0])
        hid = jax.nn.silu(a[:, :EXPERT_FF]) * a[:, EXPERT_FF:]
        y_ref[...] += ge * _dot(hid.astype(BF16), w2_ref[0])


def _moe(x1, h2, gmat, w13, w2, tm):
    n = x1.shape[0]
    return pl.pallas_call(
        _moe_kernel,
        grid=(n // tm, N_EXPERTS),
        in_specs=[pl.BlockSpec((tm, D_MODEL), lambda i, e: (i, 0)),
                  pl.BlockSpec((tm, D_MODEL), lambda i, e: (i, 0)),
                  pl.BlockSpec((tm, ROUTE_LANES), lambda i, e: (i, 0)),
                  pl.BlockSpec((1, D_MODEL, 2 * EXPERT_FF), lambda i, e: (e, 0, 0)),
                  pl.BlockSpec((1, EXPERT_FF, D_MODEL), lambda i, e: (e, 0, 0))],
        out_specs=pl.BlockSpec((tm, D_MODEL), lambda i, e: (i, 0)),
        out_shape=jax.ShapeDtypeStruct((n, D_MODEL), F32),
        compiler_params=_cparams(("parallel", "arbitrary")),
        name="moe",
    )(x1, h2, gmat, w13, w2)


def _rope_tables(pos, g, scale):
    half = QK_ROPE // 2
    inv = ROPE_BASE ** (-jnp.arange(half, dtype=F32) / half)
    ang = pos.astype(F32)[:, None] * inv[None, :]
    cos, sin = jnp.cos(ang), jnp.sin(ang)
    n = pos.shape[0]
    g1, g2 = g[QK_NOPE:QK_NOPE + half], g[QK_NOPE + half:QK_HEAD]
    pad = jnp.zeros((n, HEAD_PAD - QK_HEAD), F32)
    a = jnp.concatenate([jnp.broadcast_to(g[:QK_NOPE], (n, QK_NOPE)), g1 * cos, g2 * cos, pad], axis=-1)
    b = jnp.concatenate([jnp.zeros((n, QK_NOPE), F32), -g2 * sin, g1 * sin, pad], axis=-1)
    return a * scale, b * scale


def _pad_heads(w, per_head, keep):
    k = w.shape[0]
    w = w.reshape(k, MLA_HEADS, per_head)[:, :, :keep]
    return jnp.pad(w, ((0, 0), (0, 0), (0, HEAD_PAD - keep))).reshape(k, QK_PAD)


def _swap_rope_cols(w96):
    k = w96.shape[0]
    w = w96.reshape(k, MLA_HEADS, QK_HEAD)
    half = QK_ROPE // 2
    sw = jnp.concatenate([jnp.zeros((k, MLA_HEADS, QK_NOPE), w.dtype), w[:, :, QK_NOPE + half:],
                          w[:, :, QK_NOPE:QK_NOPE + half]], axis=-1)
    return sw.reshape(k, MLA_HEADS * QK_HEAD)


def _block_diag(w, rows_per_group, cols_per_group):
    gb = SSM_GROUPS // SSM_BLOCKS
    w = w.reshape(SSM_BLOCKS, gb, rows_per_group, cols_per_group)
    eye = jnp.eye(gb, dtype=w.dtype)
    out = jnp.einsum('bgrc,gh->bgrhc', w, eye)
    return out.reshape(SSM_BLOCKS, gb * rows_per_group, gb * cols_per_group)


def _layer(x, pos, h0_re, h0_im, mk, mv, past, p, tl, s5_rows, moe_tm):
    b, l, _ = x.shape
    scale = 1.0 / math.sqrt(QK_HEAD)
    aq, bq = _rope_tables(pos, p["g_qn"], scale)
    ak, bk = _rope_tables(pos, p["g_kn"], 1.0)
    inproj_wts = (p["g_attn"], p["w_u"], p["w_q"], p["w_kv"], p["w_pe"], p["w_qm"], p["w_g"], p["g_qlat"],
                  p["wq_pad"], p["wq_swap"], p["g_kvlat"], p["wk_pad"], p["wv"], p["g_mqn"])
    u_tm, q, k, v, c_kv, k_pe, om, gates = _inproj(x, mk.astype(BF16), mv.astype(BF16), (aq, bq, ak, bk),
                                                   inproj_wts, tl)
    a_re = jnp.broadcast_to(p["ab_re"], (b, SSM_STATES))
    a_im = jnp.broadcast_to(p["ab_im"], (b, SSM_STATES))
    bra, h_re, h_im = _s5(u_tm.reshape(l * b, SSM_WIDTH), h0_re, h0_im, a_re, a_im, p["bmat"],
                          p["cmat"], p["ssm_d"], p["w_glu"], b, s5_rows)
    if past is None:
        o = _attn_causal(q, k, v, 256)
    else:
        past_ckv, past_kpe = past
        lp = past_ckv.shape[1]
        akp, bkp = _rope_tables(jnp.arange(lp), p["g_kn"], 1.0)
        half = QK_ROPE // 2
        lane_pad = lambda a: jnp.pad(a, ((0, 0), (0, 0), (QK_NOPE, HEAD_PAD - QK_HEAD)))
        kpe_p = lane_pad(past_kpe)
        kpe_s = lane_pad(jnp.concatenate([past_kpe[..., half:], past_kpe[..., :half]], axis=-1))
        k_past, v_past = _kvexp(past_ckv, kpe_p, kpe_s, akp, bkp, p["wk_pad"], p["wv"], 512)
        o = _attn_past(q, k_past, v_past, k, v, 256)
    merge_wts = (p["w_o_mla"], p["w_o_mem"], p["w_out"], p["g_ffn"], p["w_rt"], p["b_rt"])
    x1, h2, gmat = _merge(x, bra.reshape(l, b * D_MODEL), o, om, gates, merge_wts, tl)
    n = b * l
    y = _moe(x1.reshape(n, D_MODEL), h2.reshape(n, D_MODEL), gmat.reshape(n, ROUTE_LANES), p["w13"], p["w2"], moe_tm)
    return y.reshape(b, l, D_MODEL), c_kv, k_pe, h_re, h_im


def kernel(x_prompt, x_sample, cache_mla_ckv, cache_mla_kpe, cache_ssm_re, cache_ssm_im, cache_mem_k, cache_mem_v, mem_prompt, g_attn, w_in, ssm_a_re, ssm_a_im, ssm_log_dt, ssm_b_re, ssm_b_im, ssm_c_re, ssm_c_im, ssm_d, w_glu, g_qlat, w_uq, g_kvlat, w_ukv, g_qn, g_kn, w_o_mla, g_mem, w_mem_kv, g_mqn, g_mkn, w_o_mem, w_out, g_ffn, w_rg, b_rg, w_re, b_re, w_e1, w_e3, w_e2):
    assert g_attn.shape[0] == 1, "single-layer step"
    bp, lp, _ = x_prompt.shape
    bs, ls, _ = x_sample.shape
    past_len = cache_mla_ckv.shape[2]
    lyr = 0

    o1 = SSM_WIDTH
    o2 = o1 + Q_LORA
    o3 = o2 + KV_LORA
    o4 = o3 + QK_ROPE
    o5 = o4 + MEM_WIDTH
    w_in_bf = w_in[lyr].astype(BF16)
    row = lambda a: a.reshape(1, -1)
    p = {
        "g_attn": row(g_attn[lyr]), "w_u": w_in_bf[:, :o1], "w_q": w_in_bf[:, o1:o2], "w_kv": w_in_bf[:, o2:o3],
        "w_qm": w_in_bf[:, o4:o5], "w_g": w_in_bf[:, o5:],
        "g_qlat": row(g_qlat[lyr]), "g_kvlat": row(g_kvlat[lyr]), "g_mqn": row(g_mqn[lyr]),
        "g_qn": g_qn[lyr], "g_kn": g_kn[lyr], "g_ffn": row(g_ffn[lyr]),
        "ssm_d": ssm_d[lyr], "w_glu": w_glu[lyr].astype(BF16),
        "w_o_mla": w_o_mla[lyr].astype(BF16), "w_o_mem": w_o_mem[lyr].astype(BF16), "w_out": w_out[lyr].astype(BF16),
    }
    wuq = w_uq[lyr]
    p["wq_pad"] = _pad_heads(wuq, QK_HEAD, QK_HEAD).astype(BF16)
    p["wq_swap"] = _pad_heads(_swap_rope_cols(wuq), QK_HEAD, QK_HEAD).astype(BF16)
    wukv = w_ukv[lyr]
    p["wk_pad"] = _pad_heads(wukv, QK_NOPE + V_HEAD, QK_NOPE).astype(BF16)
    p["wv"] = wukv.reshape(KV_LORA, MLA_HEADS, QK_NOPE + V_HEAD)[:, :, QK_NOPE:].reshape(KV_LORA, MLA_WIDTH).astype(BF16)
    half = QK_ROPE // 2
    w_pe = w_in_bf[:, o3:o4]
    col_pad = lambda w: jnp.pad(w, ((0, 0), (QK_NOPE, HEAD_PAD - QK_HEAD)))
    p["w_pe"] = jnp.concatenate([col_pad(w_pe), col_pad(jnp.concatenate([w_pe[:, half:], w_pe[:, :half]], axis=-1))],
                                axis=-1)

    ab_re, ab_im, f_re, f_im = _zoh(ssm_a_re[lyr], ssm_a_im[lyr], ssm_log_dt[lyr])
    b_re_, b_im_ = ssm_b_re[lyr], ssm_b_im[lyr]
    bb_re = f_re[..., None] * b_re_ - f_im[..., None] * b_im_
    bb_im = f_re[..., None] * b_im_ + f_im[..., None] * b_re_
    to_cp = lambda w: jnp.swapaxes(w, 1, 2)
    p["bmat"] = jnp.concatenate([_block_diag(to_cp(bb_re), SSM_GROUP_CH, SSM_STATE),
                                 _block_diag(to_cp(bb_im), SSM_GROUP_CH, SSM_STATE)], axis=-1).astype(BF16)
    to_pc = lambda w: jnp.swapaxes(w, 1, 2)
    p["cmat"] = jnp.stack([_block_diag(to_pc(ssm_c_re[lyr]), SSM_STATE, SSM_GROUP_CH),
                           _block_diag(to_pc(ssm_c_im[lyr]), SSM_STATE, SSM_GROUP_CH)]).astype(BF16)
    p["ab_re"] = ab_re.reshape(1, SSM_STATES)
    p["ab_im"] = ab_im.reshape(1, SSM_STATES)

    w_rt = jnp.concatenate([w_rg[lyr], w_re[lyr]], axis=-1)
    p["w_rt"] = jnp.pad(w_rt, ((0, 0), (0, ROUTE_LANES - w_rt.shape[1])))
    b_rt = jnp.concatenate([b_rg[lyr], b_re[lyr].reshape(-1)])
    p["b_rt"] = jnp.pad(b_rt, (0, ROUTE_LANES - b_rt.shape[0])).reshape(1, ROUTE_LANES)
    p["w13"] = jnp.concatenate([w_e1[lyr], w_e3[lyr]], axis=-1).astype(BF16)
    p["w2"] = w_e2[lyr].astype(BF16)

    mk, mv = _memkv(mem_prompt.reshape(-1, D_MODEL), g_mem[lyr], w_mem_kv[lyr].astype(BF16), g_mkn[lyr])
    m_tok = mem_prompt.shape[1]
    mk3, mv3 = mk.reshape(bp, m_tok, MEM_WIDTH), mv.reshape(bp, m_tok, MEM_WIDTH)
    zeros = jnp.zeros((bp, SSM_STATES), F32)
    yp, ckv_p, kpe_p, sre_p, sim_p = _layer(x_prompt, jnp.arange(lp), zeros, zeros, mk3, mv3, None, p,
                                            tl=256, s5_rows=512, moe_tm=1024)

    ys, ckv_s, kpe_s, sre_s, sim_s = _layer(
        x_sample, past_len + jnp.arange(ls), cache_ssm_re[lyr].reshape(bs, SSM_STATES),
        cache_ssm_im[lyr].reshape(bs, SSM_STATES), cache_mem_k[lyr].reshape(bs, -1, MEM_WIDTH),
        cache_mem_v[lyr].reshape(bs, -1, MEM_WIDTH), (cache_mla_ckv[lyr], cache_mla_kpe[lyr]), p,
        tl=ls, s5_rows=bs * ls, moe_tm=bs * ls)

    st = lambda a, bsz: a.reshape(1, bsz, SSM_GROUPS, SSM_STATE)
    mem_shape = (1, bp, m_tok, MEM_HEADS, MEM_HEAD)
    return (yp, ys, ckv_p[None], kpe_p[None], st(sre_p, bp), st(sim_p, bp), mk.reshape(mem_shape),
            mv.reshape(mem_shape), ckv_s[None], kpe_s[None], st(sre_s, bs), st(sim_s, bs))
```

```python
import functools
import math

import jax
import jax.numpy as jnp
import numpy as np
from jax import lax
from jax.experimental import pallas as pl
from jax.experimental.pallas import tpu as pltpu

D_MODEL = 1024
CHUNK = 64
RMS_EPS = 1e-6
SSM_GROUPS = 32
SSM_GROUP_CH = 16
SSM_WIDTH = SSM_GROUPS * SSM_GROUP_CH
SSM_STATE = 64
SSM_STATES = SSM_GROUPS * SSM_STATE
SSM_BLOCKS = 2
MLA_HEADS = 8
QK_NOPE = 64
QK_ROPE = 32
QK_HEAD = QK_NOPE + QK_ROPE
V_HEAD = 64
Q_LORA = 384
KV_LORA = 256
ROPE_BASE = 10000.0
MLA_WIDTH = MLA_HEADS * V_HEAD
MEM_HEADS = 4
MEM_HEAD = 128
MEM_WIDTH = MEM_HEADS * MEM_HEAD
N_BRANCHES = 3
N_EXPERT_GROUPS = 4
EXPERTS_PER_GROUP = 8
N_EXPERTS = N_EXPERT_GROUPS * EXPERTS_PER_GROUP
EXPERT_FF = 256

LANES = 128
SUBLANES = 8
HEAD_PAD = LANES
QK_PAD = MLA_HEADS * HEAD_PAD
VMEM_LIMIT = 56 * 1024 * 1024

BF16 = jnp.bfloat16
F32 = jnp.float32
NEG_INF = -1e30


def _cparams(sem):
    return pltpu.CompilerParams(dimension_semantics=sem, vmem_limit_bytes=VMEM_LIMIT)


def _const_spec(shape):
    nd = len(shape)
    return pl.BlockSpec(shape, lambda *_: (0,) * nd, pipeline_mode=pl.Buffered(1))


def _rms_scale(xf, width):
    return lax.rsqrt(jnp.sum(xf * xf, axis=-1, keepdims=True) * (1.0 / width) + RMS_EPS)


def _dot(a, b):
    return jnp.dot(a, b, preferred_element_type=F32)


def _dot_nt(a, b):
    return lax.dot_general(a, b, (((1,), (1,)), ((), ())), preferred_element_type=F32)


def _zoh_kernel(lr_ref, li_ref, ldt_ref, abr_ref, abi_ref, fr_ref, fi_ref):
    lr = lr_ref[...]
    li = li_ref[...]
    dt = jnp.exp(ldt_ref[...])
    mag = jnp.exp(lr * dt)
    ab_re = mag * jnp.cos(li * dt)
    ab_im = mag * jnp.sin(li * dt)
    den = lr * lr + li * li
    nr = ab_re - 1.0
    ni = ab_im
    abr_ref[...] = ab_re
    abi_ref[...] = ab_im
    fr_ref[...] = (nr * lr + ni * li) / den
    fi_ref[...] = (ni * lr - nr * li) / den


def _zoh(a_re, a_im, log_dt):
    shp = jax.ShapeDtypeStruct((SSM_GROUPS, SSM_STATE), F32)
    return pl.pallas_call(_zoh_kernel, out_shape=(shp, shp, shp, shp), name="zoh")(
        a_re, a_im, log_dt.reshape(SSM_GROUPS, 1))


def _expand_kv(ckv_bf, kpe_p, kpe_s, wk_ref, wv_ref, ak, bk):
    k_nope = _dot(ckv_bf, wk_ref[...])
    v = _dot(ckv_bf, wv_ref[...])
    ss_pe = jnp.sum(kpe_p * kpe_p, axis=-1, keepdims=True)
    rot = kpe_p * ak + kpe_s * bk
    heads = []
    for h in range(MLA_HEADS):
        kh = k_nope[:, h * HEAD_PAD:(h + 1) * HEAD_PAD]
        ss = jnp.sum(kh * kh, axis=-1, keepdims=True) + ss_pe
        rs = lax.rsqrt(ss * (1.0 / QK_HEAD) + RMS_EPS)
        heads.append((rs * (kh * ak + rot)).astype(BF16))
    return jnp.concatenate(heads, axis=-1), v.astype(BF16)


def _memkv_kernel(mem_ref, g_ref, w_ref, gk_ref, k_ref, v_ref):
    x = mem_ref[...]
    h = (x * _rms_scale(x, D_MODEL) * g_ref[...]).astype(BF16)
    kv = _dot(h, w_ref[...])
    gk = gk_ref[...]
    for hd in range(MEM_HEADS):
        kh = kv[:, hd * MEM_HEAD:(hd + 1) * MEM_HEAD]
        k_ref[:, hd * MEM_HEAD:(hd + 1) * MEM_HEAD] = kh * _rms_scale(kh, MEM_HEAD) * gk
    v_ref[...] = kv[:, MEM_WIDTH:]


def _memkv(mem2d, g_mem, w_mem_kv_bf, g_mkn):
    n = mem2d.shape[0]
    tm = 256
    out = jax.ShapeDtypeStruct((n, MEM_WIDTH), F32)
    return pl.pallas_call(
        _memkv_kernel,
        grid=(n // tm,),
        in_specs=[pl.BlockSpec((tm, D_MODEL), lambda i: (i, 0)),
                  _const_spec((1, D_MODEL)),
                  _const_spec((D_MODEL, 2 * MEM_WIDTH)),
                  _const_spec((1, MEM_HEAD))],
        out_specs=(pl.BlockSpec((tm, MEM_WIDTH), lambda i: (i, 0)),
                   pl.BlockSpec((tm, MEM_WIDTH), lambda i: (i, 0))),
        out_shape=(out, out),
        compiler_params=_cparams(("parallel",)),
        name="memkv",
    )(mem2d, g_mem.reshape(1, D_MODEL), w_mem_kv_bf, g_mkn.reshape(1, MEM_HEAD))


def _inproj_kernel(x_ref, mk_ref, mv_ref, aq_ref, bq_ref, ak_ref, bk_ref,
                   g_attn_ref, w_u_ref, w_q_ref, w_kv_ref, w_pe_ref, w_qm_ref, w_g_ref,
                   g_qlat_ref, wq_ref, wqs_ref, g_kvlat_ref, wk_ref, wv_ref, g_mqn_ref,
                   u_ref, q_ref, k_ref, v_ref, ckv_ref, kpe_ref, om_ref, gate_ref):
    x = x_ref[0]
    h = (x * _rms_scale(x, D_MODEL) * g_attn_ref[...]).astype(BF16)

    u_ref[0] = _dot(h, w_u_ref[...])
    gate_ref[0] = jax.nn.sigmoid(_dot(h, w_g_ref[...])).astype(BF16)

    q_lat = _dot(h, w_q_ref[...])
    qn = (q_lat * _rms_scale(q_lat, Q_LORA) * g_qlat_ref[...]).astype(BF16)
    q_up = _dot(qn, wq_ref[...])
    q_sw = _dot(qn, wqs_ref[...])
    aq = aq_ref[...]
    bq = bq_ref[...]
    for hd in range(MLA_HEADS):
        sl = slice(hd * HEAD_PAD, (hd + 1) * HEAD_PAD)
        qh = q_up[:, sl]
        rs = _rms_scale(qh, QK_HEAD)
        q_ref[0, :, sl] = (rs * (qh * aq + q_sw[:, sl] * bq)).astype(BF16)

    kv_lat = _dot(h, w_kv_ref[...])
    c_kv = kv_lat * _rms_scale(kv_lat, KV_LORA) * g_kvlat_ref[...]
    ckv_ref[0] = c_kv
    kpe_p = _dot(h, w_pe_ref[:, :HEAD_PAD])
    kpe_s = _dot(h, w_pe_ref[:, HEAD_PAD:])
    kpe_ref[0] = kpe_p[:, QK_NOPE:QK_HEAD]
    k_all, v_all = _expand_kv(c_kv.astype(BF16), kpe_p, kpe_s, wk_ref, wv_ref, ak_ref[...], bk_ref[...])
    k_ref[0] = k_all
    v_ref[0] = v_all

    q_mem = _dot(h, w_qm_ref[...])
    gq = g_mqn_ref[...] * (1.0 / math.sqrt(MEM_HEAD))
    for hd in range(MEM_HEADS):
        sl = slice(hd * MEM_HEAD, (hd + 1) * MEM_HEAD)
        qh = q_mem[:, sl]
        qh = (qh * _rms_scale(qh, MEM_HEAD) * gq).astype(BF16)
        s = _dot_nt(qh, mk_ref[0, :, sl])
        p = jnp.exp(s - jnp.max(s, axis=-1, keepdims=True))
        o = _dot(p.astype(BF16), mv_ref[0, :, sl])
        om_ref[0, :, sl] = (o / jnp.sum(p, axis=-1, keepdims=True)).astype(BF16)


def _inproj(x, mk_bf, mv_bf, tabs, wts, tl):
    b, l, _ = x.shape
    m = mk_bf.shape[1]
    aq, bq, ak, bk = tabs
    tok = lambda w: pl.BlockSpec((1, tl, w), lambda bi, li: (bi, li, 0))
    tab = pl.BlockSpec((tl, HEAD_PAD), lambda bi, li: (li, 0))
    memspec = pl.BlockSpec((1, m, MEM_WIDTH), lambda bi, li: (bi, 0, 0))
    in_specs = [tok(D_MODEL), memspec, memspec, tab, tab, tab, tab] + [_const_spec(w.shape) for w in wts]
    out_shape = (
        jax.ShapeDtypeStruct((b, l, SSM_WIDTH), F32),
        jax.ShapeDtypeStruct((b, l, QK_PAD), BF16),
        jax.ShapeDtypeStruct((b, l, QK_PAD), BF16),
        jax.ShapeDtypeStruct((b, l, MLA_WIDTH), BF16),
        jax.ShapeDtypeStruct((b, l, KV_LORA), F32),
        jax.ShapeDtypeStruct((b, l, QK_ROPE), F32),
        jax.ShapeDtypeStruct((b, l, MEM_WIDTH), BF16),
        jax.ShapeDtypeStruct((b, l, N_BRANCHES * D_MODEL), BF16),
    )
    out_specs = (
        tok(SSM_WIDTH), tok(QK_PAD), tok(QK_PAD), tok(MLA_WIDTH), tok(KV_LORA), tok(QK_ROPE), tok(MEM_WIDTH),
        tok(N_BRANCHES * D_MODEL),
    )
    return pl.pallas_call(
        _inproj_kernel,
        grid=(b, l // tl),
        in_specs=in_specs,
        out_specs=out_specs,
        out_shape=out_shape,
        compiler_params=_cparams(("parallel", "parallel")),
        name="inproj",
    )(x, mk_bf, mv_bf, aq, bq, ak, bk, *wts)


def _kvexp_kernel(ckv_ref, kpe_p_ref, kpe_s_ref, ak_ref, bk_ref, wk_ref, wv_ref, k_ref, v_ref):
    k_all, v_all = _expand_kv(ckv_ref[0].astype(BF16), kpe_p_ref[0], kpe_s_ref[0], wk_ref, wv_ref,
                              ak_ref[...], bk_ref[...])
    k_ref[0] = k_all
    v_ref[0] = v_all


def _kvexp(ckv, kpe_p, kpe_s, ak, bk, wk, wv, tl):
    b, l, _ = ckv.shape
    tok = lambda w: pl.BlockSpec((1, tl, w), lambda bi, li: (bi, li, 0))
    tab = pl.BlockSpec((tl, HEAD_PAD), lambda bi, li: (li, 0))
    return pl.pallas_call(
        _kvexp_kernel,
        grid=(b, l // tl),
        in_specs=[tok(KV_LORA), tok(HEAD_PAD), tok(HEAD_PAD), tab, tab] + [_const_spec(w.shape) for w in (wk, wv)],
        out_specs=(tok(QK_PAD), tok(MLA_WIDTH)),
        out_shape=(jax.ShapeDtypeStruct((b, l, QK_PAD), BF16), jax.ShapeDtypeStruct((b, l, MLA_WIDTH), BF16)),
        compiler_params=_cparams(("parallel", "parallel")),
        name="kvexp",
    )(ckv, kpe_p, kpe_s, ak, bk, wk, wv)


SCAN_LANES = 1024
S5_STEPS = 64


def _s5_kernel(u_ref, h0r_ref, h0i_ref, ar_ref, ai_ref, bmat_ref, cmat_ref, d_ref, wglu_ref,
               out_ref, hr_ref, hi_ref, sre_ref, sim_ref):
    i = pl.program_id(0)
    batch, steps, _ = u_ref.shape
    rows = batch * steps

    @pl.when(i == 0)
    def _():
        hr_ref[...] = h0r_ref[...]
        hi_ref[...] = h0i_ref[...]

    u = u_ref[...].reshape(rows, SSM_WIDTH)
    ub = u.astype(BF16)
    blk_ch = SSM_WIDTH // SSM_BLOCKS
    blk_st = SSM_STATES // SSM_BLOCKS
    tiles_per_blk = blk_st // LANES
    for blk in range(SSM_BLOCKS):
        bu = _dot(ub[:, blk * blk_ch:(blk + 1) * blk_ch], bmat_ref[blk])
        for j in range(tiles_per_blk):
            sre_ref[blk * tiles_per_blk + j] = bu[:, j * LANES:(j + 1) * LANES]
            sim_ref[blk * tiles_per_blk + j] = bu[:, blk_st + j * LANES:blk_st + (j + 1) * LANES]

    scan_tiles = SCAN_LANES // LANES
    for c in range(SSM_STATES // SCAN_LANES):
        tiles = range(c * scan_tiles, (c + 1) * scan_tiles)
        a_re = [ar_ref[:, j * LANES:(j + 1) * LANES] for j in tiles]
        a_im = [ai_ref[:, j * LANES:(j + 1) * LANES] for j in tiles]
        h_re = [hr_ref[:, j * LANES:(j + 1) * LANES] for j in tiles]
        h_im = [hi_ref[:, j * LANES:(j + 1) * LANES] for j in tiles]
        for t in range(steps):
            step_rows = pl.ds(t, batch, stride=steps)
            for n, j in enumerate(tiles):
                n_re = a_re[n] * h_re[n] - a_im[n] * h_im[n] + sre_ref[j, step_rows, :]
                n_im = a_re[n] * h_im[n] + a_im[n] * h_re[n] + sim_ref[j, step_rows, :]
                sre_ref[j, step_rows, :] = n_re
                sim_ref[j, step_rows, :] = n_im
                h_re[n], h_im[n] = n_re, n_im
        for n, j in enumerate(tiles):
            hr_ref[:, j * LANES:(j + 1) * LANES] = h_re[n]
            hi_ref[:, j * LANES:(j + 1) * LANES] = h_im[n]

    ys = []
    for blk in range(SSM_BLOCKS):
        blk_tiles = range(blk * tiles_per_blk, (blk + 1) * tiles_per_blk)
        s_re = jnp.concatenate([sre_ref[j].astype(BF16) for j in blk_tiles], axis=-1)
        s_im = jnp.concatenate([sim_ref[j].astype(BF16) for j in blk_tiles], axis=-1)
        ys.append(_dot(s_re, cmat_ref[0, blk]) - _dot(s_im, cmat_ref[1, blk]))
    y = jnp.concatenate(ys, axis=-1) + d_ref[...] * u
    z = jax.nn.gelu(y).astype(BF16)
    zz = _dot(z, wglu_ref[...])
    out = (zz[:, :D_MODEL] * jax.nn.sigmoid(zz[:, D_MODEL:])).astype(BF16)
    out_ref[...] = out.reshape(batch, steps, D_MODEL)


def _s5(u, h0_re, h0_im, a_re8, a_im8, bmat, cmat, ssm_d, w_glu_bf, steps):
    batch, l, _ = u.shape
    rows = batch * steps
    st = jax.ShapeDtypeStruct((batch, SSM_STATES), F32)
    return pl.pallas_call(
        _s5_kernel,
        grid=(l // steps,),
        in_specs=[pl.BlockSpec((batch, steps, SSM_WIDTH), lambda i: (0, i, 0)),
                  _const_spec((batch, SSM_STATES)), _const_spec((batch, SSM_STATES)),
                  _const_spec((batch, SSM_STATES)), _const_spec((batch, SSM_STATES)),
                  _const_spec(bmat.shape), _const_spec(cmat.shape),
                  _const_spec((1, SSM_WIDTH)), _const_spec(w_glu_bf.shape)],
        out_specs=(pl.BlockSpec((batch, steps, D_MODEL), lambda i: (0, i, 0)),
                   pl.BlockSpec((batch, SSM_STATES), lambda i: (0, 0)),
                   pl.BlockSpec((batch, SSM_STATES), lambda i: (0, 0))),
        out_shape=(jax.ShapeDtypeStruct((batch, l, D_MODEL), BF16), st, st),
        scratch_shapes=[pltpu.VMEM((SSM_STATES // LANES, rows, LANES), F32),
                        pltpu.VMEM((SSM_STATES // LANES, rows, LANES), F32)],
        compiler_params=_cparams(("arbitrary",)),
        name="s5",
    )(u, h0_re, h0_im, a_re8, a_im8, bmat, cmat, ssm_d.reshape(1, SSM_WIDTH), w_glu_bf)


def _attn_causal_kernel(q_ref, k_ref, v_ref, o_ref, s_ref, m_ref, l_ref, acc_ref, *, tq):
    i = pl.program_id(1)
    qc = lax.broadcasted_iota(jnp.int32, (tq, tq), 0) // CHUNK
    kc = lax.broadcasted_iota(jnp.int32, (tq, tq), 1) // CHUNK
    diag_mask = kc <= qc
    m_ref[...] = jnp.full(m_ref.shape, NEG_INF, F32)
    l_ref[...] = jnp.zeros(l_ref.shape, F32)
    acc_ref[...] = jnp.zeros(acc_ref.shape, F32)

    def scores(j, mask):
        r0 = pl.multiple_of(j * tq, tq)
        for hd in range(MLA_HEADS):
            ks = slice(hd * HEAD_PAD, (hd + 1) * HEAD_PAD)
            s = _dot_nt(q_ref[0, :, ks], k_ref[0, pl.ds(r0, tq), ks])
            if mask is not None:
                s = jnp.where(mask, s, NEG_INF)
            s_ref[hd, j] = s
            m_ref[hd] = jnp.maximum(m_ref[hd], jnp.maximum(s[:, :LANES], s[:, LANES:]))

    def pass1(j, c):
        scores(j, None)
        return c

    lax.fori_loop(0, i, pass1, 0)
    scores(i, diag_mask)
    for hd in range(MLA_HEADS):
        m_ref[hd] = jnp.broadcast_to(jnp.max(m_ref[hd], axis=-1, keepdims=True), (tq, LANES))

    def pass2(j, c):
        r0 = pl.multiple_of(j * tq, tq)
        for hd in range(MLA_HEADS):
            vs = slice(hd * V_HEAD, (hd + 1) * V_HEAD)
            s = s_ref[hd, j]
            mb = m_ref[hd]
            p0 = jnp.exp2(s[:, :LANES] - mb)
            p1 = jnp.exp2(s[:, LANES:] - mb)
            l_ref[hd] += p0 + p1
            p = jnp.concatenate([p0, p1], axis=-1).astype(BF16)
            acc_ref[hd] += _dot(p, v_ref[0, pl.ds(r0, tq), vs])
        return c

    lax.fori_loop(0, i + 1, pass2, 0)
    for hd in range(MLA_HEADS):
        vs = slice(hd * V_HEAD, (hd + 1) * V_HEAD)
        o_ref[0, :, vs] = (acc_ref[hd] / jnp.sum(l_ref[hd], axis=-1, keepdims=True)).astype(BF16)


def _attn_causal(q, k, v, tq):
    b, l, _ = q.shape
    assert tq == 2 * LANES and l % tq == 0 and tq % CHUNK == 0
    full = lambda w: pl.BlockSpec((1, l, w), lambda bi, qi: (bi, 0, 0))
    return pl.pallas_call(
        functools.partial(_attn_causal_kernel, tq=tq),
        grid=(b, l // tq),
        in_specs=[pl.BlockSpec((1, tq, QK_PAD), lambda bi, qi: (bi, qi, 0)), full(QK_PAD), full(MLA_WIDTH)],
        out_specs=pl.BlockSpec((1, tq, MLA_WIDTH), lambda bi, qi: (bi, qi, 0)),
        out_shape=jax.ShapeDtypeStruct((b, l, MLA_WIDTH), BF16),
        scratch_shapes=[pltpu.VMEM((MLA_HEADS, l // tq, tq, tq), F32),
                        pltpu.VMEM((MLA_HEADS, tq, LANES), F32),
                        pltpu.VMEM((MLA_HEADS, tq, LANES), F32),
                        pltpu.VMEM((MLA_HEADS, tq, V_HEAD), F32)],
        compiler_params=_cparams(("parallel", "arbitrary")),
        name="attn_prompt",
    )(q, k, v)


def _attn_past_kernel(q_ref, kp_ref, vp_ref, kn_ref, vn_ref, o_ref):
    for hd in range(MLA_HEADS):
        ks = slice(hd * HEAD_PAD, (hd + 1) * HEAD_PAD)
        vs = slice(hd * V_HEAD, (hd + 1) * V_HEAD)
        qh = q_ref[0, :, ks]
        s_past = _dot_nt(qh, kp_ref[0, :, ks])
        s_new = _dot_nt(qh, kn_ref[0, :, ks])
        m = jnp.maximum(jnp.max(s_past, axis=-1, keepdims=True), jnp.max(s_new, axis=-1, keepdims=True))
        p_past = jnp.exp2(s_past - m)
        p_new = jnp.exp2(s_new - m)
        l = jnp.sum(p_past, axis=-1, keepdims=True) + jnp.sum(p_new, axis=-1, keepdims=True)
        o = _dot(p_past.astype(BF16), vp_ref[0, :, vs]) + _dot(p_new.astype(BF16), vn_ref[0, :, vs])
        o_ref[0, :, vs] = (o / l).astype(BF16)


def _attn_past(q, k_past, v_past, k_new, v_new):
    b, lq, _ = q.shape
    lp = k_past.shape[1]
    assert lp % CHUNK == 0 and lq <= CHUNK
    spec = lambda n, w: pl.BlockSpec((1, n, w), lambda bi: (bi, 0, 0))
    return pl.pallas_call(
        _attn_past_kernel,
        grid=(b,),
        in_specs=[spec(lq, QK_PAD), spec(lp, QK_PAD), spec(lp, MLA_WIDTH), spec(lq, QK_PAD), spec(lq, MLA_WIDTH)],
        out_specs=spec(lq, MLA_WIDTH),
        out_shape=jax.ShapeDtypeStruct((b, lq, MLA_WIDTH), BF16),
        compiler_params=_cparams(("parallel",)),
        name="attn_sample",
    )(q, k_past, v_past, k_new, v_new)


ROUTE_LANES = LANES


def _first_argmax(v, lane, width):
    vmax = jnp.max(v, axis=-1, keepdims=True)
    idx = jnp.min(jnp.where(v == vmax, lane, width), axis=-1, keepdims=True)
    return vmax, idx


def _merge_kernel(x_ref, bra_ref, o_ref, om_ref, gate_ref, cnt0_ref, w_omla_ref, w_omem_ref, w_out_ref, g_ffn_ref,
                  w_rt_ref, b_rt_ref, x1_ref, h2_ref, route_ref, cnt_ref):
    first = (pl.program_id(0) == 0) & (pl.program_id(1) == 0)

    @pl.when(first)
    def _():
        cnt_ref[...] = cnt0_ref[...]

    br_b = _dot(o_ref[0], w_omla_ref[...])
    br_c = _dot(om_ref[0], w_omem_ref[...])
    g = gate_ref[0].astype(F32)
    merged = (g[:, :D_MODEL] * bra_ref[0].astype(F32) + g[:, D_MODEL:2 * D_MODEL] * br_b
              + g[:, 2 * D_MODEL:] * br_c)
    x1 = x_ref[0] + _dot(merged.astype(BF16), w_out_ref[...])
    x1_ref[0] = x1
    h2 = x1 * _rms_scale(x1, D_MODEL) * g_ffn_ref[...]
    h2_ref[0] = h2

    logits = jnp.dot(h2, w_rt_ref[...], precision=lax.Precision.HIGHEST, preferred_element_type=F32) + b_rt_ref[...]
    lane = lax.broadcasted_iota(jnp.int32, logits.shape, 1).astype(F32)
    ninf = jnp.float32(-jnp.inf)
    lg = jnp.where(lane < N_EXPERT_GROUPS, logits, ninf)
    lg_max, grp = _first_argmax(lg, lane, float(ROUTE_LANES))
    p_top = 1.0 / jnp.sum(jnp.exp(lg - lg_max), axis=-1, keepdims=True)
    lo = N_EXPERT_GROUPS + grp * EXPERTS_PER_GROUP
    in_grp = (lane >= lo) & (lane < lo + EXPERTS_PER_GROUP)
    le = jnp.where(in_grp, logits, ninf)
    v1, i1 = _first_argmax(le, lane, float(ROUTE_LANES))
    v2, i2 = _first_argmax(jnp.where(lane == i1, ninf, le), lane, float(ROUTE_LANES))
    e2 = jnp.exp(v2 - v1)
    w1 = p_top / (1.0 + e2)
    w2 = p_top * e2 / (1.0 + e2)
    elane = lane + N_EXPERT_GROUPS
    oh1 = elane == i1
    oh2 = elane == i2
    onehot = jnp.where(oh1 | oh2, 1.0, 0.0)
    tl = onehot.shape[0]
    tri = jnp.where(lax.broadcasted_iota(jnp.int32, (tl, tl), 0) > lax.broadcasted_iota(jnp.int32, (tl, tl), 1),
                    1.0, 0.0).astype(BF16)
    before = _dot(tri, onehot.astype(BF16)) + cnt_ref[...]
    r1 = jnp.sum(jnp.where(oh1, before, 0.0), axis=-1, keepdims=True)
    r2 = jnp.sum(jnp.where(oh2, before, 0.0), axis=-1, keepdims=True)
    cnt_ref[...] += jnp.sum(onehot, axis=0, keepdims=True)
    cols = (i1 - N_EXPERT_GROUPS, i2 - N_EXPERT_GROUPS, w1, w2, r1, r2)
    route = jnp.zeros_like(logits)
    for k, col in enumerate(cols):
        route = jnp.where(lane == k, col, route)
    route_ref[0] = route


def _merge(x, bra, o, om, gates, cnt0, wts, tl):
    b, l, _ = x.shape
    tok = lambda w: pl.BlockSpec((1, tl, w), lambda bi, li: (bi, li, 0))
    in_specs = [tok(D_MODEL), tok(D_MODEL), tok(MLA_WIDTH), tok(MEM_WIDTH), tok(N_BRANCHES * D_MODEL),
                _const_spec((1, ROUTE_LANES))] + [_const_spec(w.shape) for w in wts]
    return pl.pallas_call(
        _merge_kernel,
        grid=(b, l // tl),
        in_specs=in_specs,
        out_specs=(tok(D_MODEL), tok(D_MODEL), tok(ROUTE_LANES),
                   pl.BlockSpec((1, ROUTE_LANES), lambda bi, li: (0, 0))),
        out_shape=(jax.ShapeDtypeStruct((b, l, D_MODEL), F32), jax.ShapeDtypeStruct((b, l, D_MODEL), F32),
                   jax.ShapeDtypeStruct((b, l, ROUTE_LANES), F32), jax.ShapeDtypeStruct((1, ROUTE_LANES), F32)),
        compiler_params=_cparams(("arbitrary", "arbitrary")),
        name="merge",
    )(x, bra, o, om, gates, cnt0, *wts)


MOE_TILE = 256
MOE_DMA_TILE = 512


def _row_copy(src_hbm, src_row, dst_ref, dst_row, sem):
    return pltpu.make_async_copy(src_hbm.at[pl.ds(src_row, 1)], dst_ref.at[pl.ds(dst_row, 1)], sem)


def _dispatch_kernel(slot_ref, h2_hbm, hs_in_hbm, hs_hbm, sem):
    del hs_in_hbm
    td = slot_ref.shape[2]
    base = pl.program_id(0) * td

    def start(r, c):
        for k in range(2):
            _row_copy(h2_hbm, base + r, hs_hbm, slot_ref[0, k, r], sem).start()
        return c

    def wait(r, c):
        for k in range(2):
            _row_copy(h2_hbm, base + r, hs_hbm, slot_ref[0, k, r], sem).wait()
        return c

    lax.fori_loop(0, td, start, 0, unroll=8)
    lax.fori_loop(0, td, wait, 0, unroll=8)


def _dma_cparams():
    return pltpu.CompilerParams(dimension_semantics=("arbitrary",), vmem_limit_bytes=VMEM_LIMIT,
                                disable_bounds_checks=True)


def _dispatch(slots, h2, hs, td):
    n = h2.shape[0]
    return pl.pallas_call(
        _dispatch_kernel,
        grid=(n // td,),
        in_specs=[pl.BlockSpec((1, 2, td), lambda i: (i, 0, 0), memory_space=pltpu.SMEM),
                  pl.BlockSpec(memory_space=pl.ANY), pl.BlockSpec(memory_space=pl.ANY)],
        out_specs=pl.BlockSpec(memory_space=pl.ANY),
        out_shape=jax.ShapeDtypeStruct(hs.shape, hs.dtype),
        scratch_shapes=[pltpu.SemaphoreType.DMA],
        input_output_aliases={2: 0},
        compiler_params=_dma_cparams(),
        name="dispatch",
    )(slots, h2, hs)


def _experts_kernel(tile_expert_ref, n_tiles_ref, hs_ref, w13_ref, w2_ref, y_ref):
    t = pl.program_id(0)

    @pl.when(t < n_tiles_ref[0])
    def _():
        a = _dot(hs_ref[...].astype(BF16), w13_ref[0])
        hid = jax.nn.silu(a[:, :EXPERT_FF]) * a[:, EXPERT_FF:]
        y_ref[...] = _dot(hid.astype(BF16), w2_ref[0])

    @pl.when(t >= n_tiles_ref[0])
    def _():
        y_ref[...] = jnp.zeros(y_ref.shape, y_ref.dtype)


def _experts(tile_expert, n_tiles, hs, w13, w2):
    s = hs.shape[0]
    grid_spec = pltpu.PrefetchScalarGridSpec(
        num_scalar_prefetch=2,
        grid=(s // MOE_TILE,),
        in_specs=[pl.BlockSpec((MOE_TILE, D_MODEL), lambda t, te, nt: (t, 0)),
                  pl.BlockSpec((1, D_MODEL, 2 * EXPERT_FF), lambda t, te, nt: (te[t], 0, 0)),
                  pl.BlockSpec((1, EXPERT_FF, D_MODEL), lambda t, te, nt: (te[t], 0, 0))],
        out_specs=pl.BlockSpec((MOE_TILE, D_MODEL), lambda t, te, nt: (t, 0)),
    )
    return pl.pallas_call(
        _experts_kernel,
        grid_spec=grid_spec,
        out_shape=jax.ShapeDtypeStruct((s, D_MODEL), F32),
        compiler_params=_cparams(("arbitrary",)),
        name="experts",
    )(tile_expert, n_tiles, hs, w13, w2)


def _combine_kernel(slot_ref, x1_ref, route_ref, ye_hbm, y_ref, rows_ref, sem):
    tc = slot_ref.shape[2]

    def start(r, c):
        for k in range(2):
            _row_copy(ye_hbm, slot_ref[0, k, r], rows_ref.at[k], r, sem).start()
        return c

    def wait(r, c):
        for k in range(2):
            _row_copy(ye_hbm, slot_ref[0, k, r], rows_ref.at[k], r, sem).wait()
        return c

    lax.fori_loop(0, tc, start, 0, unroll=8)
    lax.fori_loop(0, tc, wait, 0, unroll=8)
    route = route_ref[...]
    y = x1_ref[...]
    for k in range(2):
        gate = route[:, 2 + k:3 + k]
        y = y + gate * rows_ref[k]
    y_ref[...] = y


def _combine(slots, x1, route, ye, tc):
    n = x1.shape[0]
    return pl.pallas_call(
        _combine_kernel,
        grid=(n // tc,),
        in_specs=[pl.BlockSpec((1, 2, tc), lambda i: (i, 0, 0), memory_space=pltpu.SMEM),
                  pl.BlockSpec((tc, D_MODEL), lambda i: (i, 0)),
                  pl.BlockSpec((tc, ROUTE_LANES), lambda i: (i, 0)),
                  pl.BlockSpec(memory_space=pl.ANY)],
        out_specs=pl.BlockSpec((tc, D_MODEL), lambda i: (i, 0)),
        out_shape=jax.ShapeDtypeStruct((n, D_MODEL), F32),
        scratch_shapes=[pltpu.VMEM((2, tc, D_MODEL), F32), pltpu.SemaphoreType.DMA],
        compiler_params=_dma_cparams(),
        name="combine",
    )(slots, x1, route, ye)


def _rope_tables(pos, g, scale):
    half = QK_ROPE // 2
    inv = ROPE_BASE ** (-jnp.arange(half, dtype=F32) / half)
    ang = pos.astype(F32)[:, None] * inv[None, :]
    cos, sin = jnp.cos(ang), jnp.sin(ang)
    n = pos.shape[0]
    g1, g2 = g[QK_NOPE:QK_NOPE + half], g[QK_NOPE + half:QK_HEAD]
    pad = jnp.zeros((n, HEAD_PAD - QK_HEAD), F32)
    a = jnp.concatenate([jnp.broadcast_to(g[:QK_NOPE], (n, QK_NOPE)), g1 * cos, g2 * cos, pad], axis=-1)
    b = jnp.concatenate([jnp.zeros((n, QK_NOPE), F32), -g2 * sin, g1 * sin, pad], axis=-1)
    return a * scale, b * scale


def _pad_heads(w, per_head, keep):
    k = w.shape[0]
    w = w.reshape(k, MLA_HEADS, per_head)[:, :, :keep]
    return jnp.pad(w, ((0, 0), (0, 0), (0, HEAD_PAD - keep))).reshape(k, QK_PAD)


def _swap_rope_cols(w96):
    k = w96.shape[0]
    w = w96.reshape(k, MLA_HEADS, QK_HEAD)
    half = QK_ROPE // 2
    sw = jnp.concatenate([jnp.zeros((k, MLA_HEADS, QK_NOPE), w.dtype), w[:, :, QK_NOPE + half:],
                          w[:, :, QK_NOPE:QK_NOPE + half]], axis=-1)
    return sw.reshape(k, MLA_HEADS * QK_HEAD)


def _block_diag(w, rows_per_group, cols_per_group):
    gb = SSM_GROUPS // SSM_BLOCKS
    w = w.reshape(SSM_BLOCKS, gb, rows_per_group, cols_per_group)
    eye = jnp.eye(gb, dtype=w.dtype)
    out = jnp.einsum('bgrc,gh->bgrhc', w, eye)
    return out.reshape(SSM_BLOCKS, gb * rows_per_group, gb * cols_per_group)


def _layer(x, pos, h0_re, h0_im, mk, mv, past, cnt0, p, tl):
    b, l, _ = x.shape
    scale = math.log2(math.e) / math.sqrt(QK_HEAD)
    aq, bq = _rope_tables(pos, p["g_qn"], scale)
    ak, bk = _rope_tables(pos, p["g_kn"], 1.0)
    inproj_wts = (p["g_attn"], p["w_u"], p["w_q"], p["w_kv"], p["w_pe"], p["w_qm"], p["w_g"], p["g_qlat"],
                  p["wq_pad"], p["wq_swap"], p["g_kvlat"], p["wk_pad"], p["wv"], p["g_mqn"])
    u, q, k, v, c_kv, k_pe, om, gates = _inproj(x, mk.astype(BF16), mv.astype(BF16), (aq, bq, ak, bk),
                                                inproj_wts, tl)
    a_re = jnp.broadcast_to(p["ab_re"], (b, SSM_STATES))
    a_im = jnp.broadcast_to(p["ab_im"], (b, SSM_STATES))
    bra, h_re, h_im = _s5(u, h0_re, h0_im, a_re, a_im, p["bmat"], p["cmat"], p["ssm_d"], p["w_glu"], S5_STEPS)
    if past is None:
        o = _attn_causal(q, k, v, 256)
    else:
        past_ckv, past_kpe = past
        lp = past_ckv.shape[1]
        akp, bkp = _rope_tables(jnp.arange(lp), p["g_kn"], 1.0)
        half = QK_ROPE // 2
        lane_pad = lambda a: jnp.pad(a, ((0, 0), (0, 0), (QK_NOPE, HEAD_PAD - QK_HEAD)))
        kpe_p = lane_pad(past_kpe)
        kpe_s = lane_pad(jnp.concatenate([past_kpe[..., half:], past_kpe[..., :half]], axis=-1))
        k_past, v_past = _kvexp(past_ckv, kpe_p, kpe_s, akp, bkp, p["wk_pad"], p["wv"], 512)
        o = _attn_past(q, k_past, v_past, k, v)
    merge_wts = (p["w_o_mla"], p["w_o_mem"], p["w_out"], p["g_ffn"], p["w_rt"], p["b_rt"])
    x1, h2, route, cnt = _merge(x, bra, o, om, gates, cnt0, merge_wts, tl)
    n = b * l
    return (x1.reshape(n, D_MODEL), h2.reshape(n, D_MODEL), route.reshape(n, ROUTE_LANES), cnt), c_kv, k_pe, h_re, h_im


def _slots(route, offsets, tile):
    expert = route[:, 0:2].astype(jnp.int32)
    rank = route[:, 4:6].astype(jnp.int32)
    slot = offsets[expert] + rank
    return slot.reshape(-1, tile, 2).transpose(0, 2, 1)


def kernel(x_prompt, x_sample, cache_mla_ckv, cache_mla_kpe, cache_ssm_re, cache_ssm_im, cache_mem_k, cache_mem_v, mem_prompt, g_attn, w_in, ssm_a_re, ssm_a_im, ssm_log_dt, ssm_b_re, ssm_b_im, ssm_c_re, ssm_c_im, ssm_d, w_glu, g_qlat, w_uq, g_kvlat, w_ukv, g_qn, g_kn, w_o_mla, g_mem, w_mem_kv, g_mqn, g_mkn, w_o_mem, w_out, g_ffn, w_rg, b_rg, w_re, b_re, w_e1, w_e3, w_e2):
    assert g_attn.shape[0] == 1, "single-layer step"
    bp, lp, _ = x_prompt.shape
    bs, ls, _ = x_sample.shape
    past_len = cache_mla_ckv.shape[2]
    lyr = 0

    o1 = SSM_WIDTH
    o2 = o1 + Q_LORA
    o3 = o2 + KV_LORA
    o4 = o3 + QK_ROPE
    o5 = o4 + MEM_WIDTH
    w_in_bf = w_in[lyr].astype(BF16)
    row = lambda a: a.reshape(1, -1)
    p = {
        "g_attn": row(g_attn[lyr]), "w_u": w_in_bf[:, :o1], "w_q": w_in_bf[:, o1:o2], "w_kv": w_in_bf[:, o2:o3],
        "w_qm": w_in_bf[:, o4:o5], "w_g": w_in_bf[:, o5:],
        "g_qlat": row(g_qlat[lyr]), "g_kvlat": row(g_kvlat[lyr]), "g_mqn": row(g_mqn[lyr]),
        "g_qn": g_qn[lyr], "g_kn": g_kn[lyr], "g_ffn": row(g_ffn[lyr]),
        "ssm_d": ssm_d[lyr], "w_glu": w_glu[lyr].astype(BF16),
        "w_o_mla": w_o_mla[lyr].astype(BF16), "w_o_mem": w_o_mem[lyr].astype(BF16), "w_out": w_out[lyr].astype(BF16),
    }
    wuq = w_uq[lyr]
    p["wq_pad"] = _pad_heads(wuq, QK_HEAD, QK_HEAD).astype(BF16)
    p["wq_swap"] = _pad_heads(_swap_rope_cols(wuq), QK_HEAD, QK_HEAD).astype(BF16)
    wukv = w_ukv[lyr]
    p["wk_pad"] = _pad_heads(wukv, QK_NOPE + V_HEAD, QK_NOPE).astype(BF16)
    p["wv"] = wukv.reshape(KV_LORA, MLA_HEADS, QK_NOPE + V_HEAD)[:, :, QK_NOPE:].reshape(KV_LORA, MLA_WIDTH).astype(BF16)
    half = QK_ROPE // 2
    w_pe = w_in_bf[:, o3:o4]
    col_pad = lambda w: jnp.pad(w, ((0, 0), (QK_NOPE, HEAD_PAD - QK_HEAD)))
    p["w_pe"] = jnp.concatenate([col_pad(w_pe), col_pad(jnp.concatenate([w_pe[:, half:], w_pe[:, :half]], axis=-1))],
                                axis=-1)

    ab_re, ab_im, f_re, f_im = _zoh(ssm_a_re[lyr], ssm_a_im[lyr], ssm_log_dt[lyr])
    b_re_, b_im_ = ssm_b_re[lyr], ssm_b_im[lyr]
    bb_re = f_re[..., None] * b_re_ - f_im[..., None] * b_im_
    bb_im = f_re[..., None] * b_im_ + f_im[..., None] * b_re_
    to_cp = lambda w: jnp.swapaxes(w, 1, 2)
    p["bmat"] = jnp.concatenate([_block_diag(to_cp(bb_re), SSM_GROUP_CH, SSM_STATE),
                                 _block_diag(to_cp(bb_im), SSM_GROUP_CH, SSM_STATE)], axis=-1).astype(BF16)
    to_pc = lambda w: jnp.swapaxes(w, 1, 2)
    p["cmat"] = jnp.stack([_block_diag(to_pc(ssm_c_re[lyr]), SSM_STATE, SSM_GROUP_CH),
                           _block_diag(to_pc(ssm_c_im[lyr]), SSM_STATE, SSM_GROUP_CH)]).astype(BF16)
    p["ab_re"] = ab_re.reshape(1, SSM_STATES)
    p["ab_im"] = ab_im.reshape(1, SSM_STATES)

    w_rt = jnp.concatenate([w_rg[lyr], w_re[lyr]], axis=-1)
    p["w_rt"] = jnp.pad(w_rt, ((0, 0), (0, ROUTE_LANES - w_rt.shape[1])))
    b_rt = jnp.concatenate([b_rg[lyr], b_re[lyr].reshape(-1)])
    p["b_rt"] = jnp.pad(b_rt, (0, ROUTE_LANES - b_rt.shape[0])).reshape(1, ROUTE_LANES)
    p["w13"] = jnp.concatenate([w_e1[lyr], w_e3[lyr]], axis=-1).astype(BF16)
    p["w2"] = w_e2[lyr].astype(BF16)

    mk, mv = _memkv(mem_prompt.reshape(-1, D_MODEL), g_mem[lyr], w_mem_kv[lyr].astype(BF16), g_mkn[lyr])
    m_tok = mem_prompt.shape[1]
    mk3, mv3 = mk.reshape(bp, m_tok, MEM_WIDTH), mv.reshape(bp, m_tok, MEM_WIDTH)
    zeros = jnp.zeros((bp, SSM_STATES), F32)
    cnt0 = jnp.zeros((1, ROUTE_LANES), F32)
    (x1_p, h2p_p, route_p, cnt_p), ckv_p, kpe_p, sre_p, sim_p = _layer(
        x_prompt, jnp.arange(lp), zeros, zeros, mk3, mv3, None, cnt0, p, tl=256)

    (x1_s, h2p_s, route_s, cnt), ckv_s, kpe_s, sre_s, sim_s = _layer(
        x_sample, past_len + jnp.arange(ls), cache_ssm_re[lyr].reshape(bs, SSM_STATES),
        cache_ssm_im[lyr].reshape(bs, SSM_STATES), cache_mem_k[lyr].reshape(bs, -1, MEM_WIDTH),
        cache_mem_v[lyr].reshape(bs, -1, MEM_WIDTH), (cache_mla_ckv[lyr], cache_mla_kpe[lyr]), cnt_p, p, tl=ls)

    n_p, n_s = x1_p.shape[0], x1_s.shape[0]
    counts = cnt[0, :N_EXPERTS].astype(jnp.int32)
    padded = (counts + MOE_TILE - 1) // MOE_TILE * MOE_TILE
    ends = jnp.cumsum(padded)
    offsets = ends - padded
    max_tiles = 2 * (n_p + n_s) // MOE_TILE + N_EXPERTS
    tile_start = jnp.arange(max_tiles, dtype=jnp.int32) * MOE_TILE
    tile_expert = jnp.sum((ends[None, :] <= tile_start[:, None]).astype(jnp.int32), axis=1)
    tile_expert = jnp.minimum(tile_expert, N_EXPERTS - 1)
    n_tiles = (ends[-1:] // MOE_TILE).astype(jnp.int32)
    slots_p = _slots(route_p, offsets, MOE_DMA_TILE)
    slots_s = _slots(route_s, offsets, MOE_DMA_TILE)
    hs = jnp.zeros((max_tiles * MOE_TILE, D_MODEL), F32)
    hs = _dispatch(slots_p, h2p_p, hs, MOE_DMA_TILE)
    hs = _dispatch(slots_s, h2p_s, hs, MOE_DMA_TILE)
    ye = _experts(tile_expert, n_tiles, hs, p["w13"], p["w2"])
    yp = _combine(slots_p, x1_p, route_p, ye, MOE_DMA_TILE).reshape(bp, lp, D_MODEL)
    ys = _combine(slots_s, x1_s, route_s, ye, MOE_DMA_TILE).reshape(bs, ls, D_MODEL)

    st = lambda a, bsz: a.reshape(1, bsz, SSM_GROUPS, SSM_STATE)
    mem_shape = (1, bp, m_tok, MEM_HEADS, MEM_HEAD)
    return (yp, ys, ckv_p[None], kpe_p[None], st(sre_p, bp), st(sim_p, bp), mk.reshape(mem_shape),
            mv.reshape(mem_shape), ckv_s[None], kpe_s[None], st(sre_s, bs), st(sim_s, bs))
```

```python
import functools
import math

import jax
import jax.numpy as jnp
import numpy as np
from jax import lax
from jax.experimental import pallas as pl
from jax.experimental.pallas import tpu as pltpu

D_MODEL = 1024
CHUNK = 64
RMS_EPS = 1e-6
SSM_GROUPS = 32
SSM_GROUP_CH = 16
SSM_WIDTH = SSM_GROUPS * SSM_GROUP_CH
SSM_STATE = 64
SSM_STATES = SSM_GROUPS * SSM_STATE
SSM_BLOCKS = 2
MLA_HEADS = 8
QK_NOPE = 64
QK_ROPE = 32
QK_HEAD = QK_NOPE + QK_ROPE
V_HEAD = 64
Q_LORA = 384
KV_LORA = 256
ROPE_BASE = 10000.0
MLA_WIDTH = MLA_HEADS * V_HEAD
MEM_HEADS = 4
MEM_HEAD = 128
MEM_WIDTH = MEM_HEADS * MEM_HEAD
N_BRANCHES = 3
N_EXPERT_GROUPS = 4
EXPERTS_PER_GROUP = 8
N_EXPERTS = N_EXPERT_GROUPS * EXPERTS_PER_GROUP
EXPERT_FF = 256

LANES = 128
SUBLANES = 8
HEAD_PAD = LANES
QK_PAD = MLA_HEADS * HEAD_PAD
VMEM_LIMIT = 56 * 1024 * 1024

BF16 = jnp.bfloat16
F32 = jnp.float32
NEG_INF = -1e30


def _cparams(sem):
    return pltpu.CompilerParams(dimension_semantics=sem, vmem_limit_bytes=VMEM_LIMIT)


def _const_spec(shape):
    nd = len(shape)
    return pl.BlockSpec(shape, lambda *_: (0,) * nd, pipeline_mode=pl.Buffered(1))


def _rms_scale(xf, width):
    return lax.rsqrt(jnp.sum(xf * xf, axis=-1, keepdims=True) * (1.0 / width) + RMS_EPS)


def _dot(a, b):
    return jnp.dot(a, b, preferred_element_type=F32)


def _dot_nt(a, b):
    return lax.dot_general(a, b, (((1,), (1,)), ((), ())), preferred_element_type=F32)


def _zoh_kernel(lr_ref, li_ref, ldt_ref, abr_ref, abi_ref, fr_ref, fi_ref):
    lr = lr_ref[...]
    li = li_ref[...]
    dt = jnp.exp(ldt_ref[...])
    mag = jnp.exp(lr * dt)
    ab_re = mag * jnp.cos(li * dt)
    ab_im = mag * jnp.sin(li * dt)
    den = lr * lr + li * li
    nr = ab_re - 1.0
    ni = ab_im
    abr_ref[...] = ab_re
    abi_ref[...] = ab_im
    fr_ref[...] = (nr * lr + ni * li) / den
    fi_ref[...] = (ni * lr - nr * li) / den


def _zoh(a_re, a_im, log_dt):
    shp = jax.ShapeDtypeStruct((SSM_GROUPS, SSM_STATE), F32)
    return pl.pallas_call(_zoh_kernel, out_shape=(shp, shp, shp, shp), name="zoh")(
        a_re, a_im, log_dt.reshape(SSM_GROUPS, 1))


def _expand_kv(ckv_bf, kpe_p, kpe_s, wk_ref, wv_ref, ak, bk):
    k_nope = _dot(ckv_bf, wk_ref[...])
    v = _dot(ckv_bf, wv_ref[...])
    ss_pe = jnp.sum(kpe_p * kpe_p, axis=-1, keepdims=True)
    rot = kpe_p * ak + kpe_s * bk
    heads = []
    for h in range(MLA_HEADS):
        kh = k_nope[:, h * HEAD_PAD:(h + 1) * HEAD_PAD]
        ss = jnp.sum(kh * kh, axis=-1, keepdims=True) + ss_pe
        rs = lax.rsqrt(ss * (1.0 / QK_HEAD) + RMS_EPS)
        heads.append((rs * (kh * ak + rot)).astype(BF16))
    return jnp.concatenate(heads, axis=-1), v.astype(BF16)


def _memkv_kernel(mem_ref, g_ref, w_ref, gk_ref, k_ref, v_ref):
    x = mem_ref[...]
    h = (x * _rms_scale(x, D_MODEL) * g_ref[...]).astype(BF16)
    kv = _dot(h, w_ref[...])
    gk = gk_ref[...]
    for hd in range(MEM_HEADS):
        kh = kv[:, hd * MEM_HEAD:(hd + 1) * MEM_HEAD]
        k_ref[:, hd * MEM_HEAD:(hd + 1) * MEM_HEAD] = kh * _rms_scale(kh, MEM_HEAD) * gk
    v_ref[...] = kv[:, MEM_WIDTH:]


def _memkv(mem2d, g_mem, w_mem_kv_bf, g_mkn):
    n = mem2d.shape[0]
    tm = 256
    out = jax.ShapeDtypeStruct((n, MEM_WIDTH), F32)
    return pl.pallas_call(
        _memkv_kernel,
        grid=(n // tm,),
        in_specs=[pl.BlockSpec((tm, D_MODEL), lambda i: (i, 0)),
                  _const_spec((1, D_MODEL)),
                  _const_spec((D_MODEL, 2 * MEM_WIDTH)),
                  _const_spec((1, MEM_HEAD))],
        out_specs=(pl.BlockSpec((tm, MEM_WIDTH), lambda i: (i, 0)),
                   pl.BlockSpec((tm, MEM_WIDTH), lambda i: (i, 0))),
        out_shape=(out, out),
        compiler_params=_cparams(("parallel",)),
        name="memkv",
    )(mem2d, g_mem.reshape(1, D_MODEL), w_mem_kv_bf, g_mkn.reshape(1, MEM_HEAD))


def _inproj_kernel(x_ref, mk_ref, mv_ref, aq_ref, bq_ref, ak_ref, bk_ref,
                   g_attn_ref, w_u_ref, w_q_ref, w_kv_ref, w_pe_ref, w_qm_ref, w_g_ref,
                   g_qlat_ref, wq_ref, wqs_ref, g_kvlat_ref, wk_ref, wv_ref, g_mqn_ref,
                   u_ref, q_ref, k_ref, v_ref, ckv_ref, kpe_ref, om_ref, gate_ref):
    x = x_ref[0]
    h = (x * _rms_scale(x, D_MODEL) * g_attn_ref[...]).astype(BF16)

    u_ref[0] = _dot(h, w_u_ref[...])
    gate_ref[0] = jax.nn.sigmoid(_dot(h, w_g_ref[...])).astype(BF16)

    q_lat = _dot(h, w_q_ref[...])
    qn = (q_lat * _rms_scale(q_lat, Q_LORA) * g_qlat_ref[...]).astype(BF16)
    q_up = _dot(qn, wq_ref[...])
    q_sw = _dot(qn, wqs_ref[...])
    aq = aq_ref[...]
    bq = bq_ref[...]
    for hd in range(MLA_HEADS):
        sl = slice(hd * HEAD_PAD, (hd + 1) * HEAD_PAD)
        qh = q_up[:, sl]
        rs = _rms_scale(qh, QK_HEAD)
        q_ref[0, :, sl] = (rs * (qh * aq + q_sw[:, sl] * bq)).astype(BF16)

    kv_lat = _dot(h, w_kv_ref[...])
    c_kv = kv_lat * _rms_scale(kv_lat, KV_LORA) * g_kvlat_ref[...]
    ckv_ref[0] = c_kv
    kpe_p = _dot(h, w_pe_ref[:, :HEAD_PAD])
    kpe_s = _dot(h, w_pe_ref[:, HEAD_PAD:])
    kpe_ref[0] = kpe_p[:, QK_NOPE:QK_HEAD]
    k_all, v_all = _expand_kv(c_kv.astype(BF16), kpe_p, kpe_s, wk_ref, wv_ref, ak_ref[...], bk_ref[...])
    k_ref[0] = k_all
    v_ref[0] = v_all

    q_mem = _dot(h, w_qm_ref[...])
    gq = g_mqn_ref[...] * (1.0 / math.sqrt(MEM_HEAD))
    for hd in range(MEM_HEADS):
        sl = slice(hd * MEM_HEAD, (hd + 1) * MEM_HEAD)
        qh = q_mem[:, sl]
        qh = (qh * _rms_scale(qh, MEM_HEAD) * gq).astype(BF16)
        s = _dot_nt(qh, mk_ref[0, :, sl])
        p = jnp.exp(s - jnp.max(s, axis=-1, keepdims=True))
        o = _dot(p.astype(BF16), mv_ref[0, :, sl])
        om_ref[0, :, sl] = (o / jnp.sum(p, axis=-1, keepdims=True)).astype(BF16)


def _inproj(x, mk_bf, mv_bf, tabs, wts, tl):
    b, l, _ = x.shape
    m = mk_bf.shape[1]
    aq, bq, ak, bk = tabs
    tok = lambda w: pl.BlockSpec((1, tl, w), lambda bi, li: (bi, li, 0))
    tab = pl.BlockSpec((tl, HEAD_PAD), lambda bi, li: (li, 0))
    memspec = pl.BlockSpec((1, m, MEM_WIDTH), lambda bi, li: (bi, 0, 0))
    in_specs = [tok(D_MODEL), memspec, memspec, tab, tab, tab, tab] + [_const_spec(w.shape) for w in wts]
    out_shape = (
        jax.ShapeDtypeStruct((b, l, SSM_WIDTH), F32),
        jax.ShapeDtypeStruct((b, l, QK_PAD), BF16),
        jax.ShapeDtypeStruct((b, l, QK_PAD), BF16),
        jax.ShapeDtypeStruct((b, l, MLA_WIDTH), BF16),
        jax.ShapeDtypeStruct((b, l, KV_LORA), F32),
        jax.ShapeDtypeStruct((b, l, QK_ROPE), F32),
        jax.ShapeDtypeStruct((b, l, MEM_WIDTH), BF16),
        jax.ShapeDtypeStruct((b, l, N_BRANCHES * D_MODEL), BF16),
    )
    out_specs = (
        tok(SSM_WIDTH), tok(QK_PAD), tok(QK_PAD), tok(MLA_WIDTH), tok(KV_LORA), tok(QK_ROPE), tok(MEM_WIDTH),
        tok(N_BRANCHES * D_MODEL),
    )
    return pl.pallas_call(
        _inproj_kernel,
        grid=(b, l // tl),
        in_specs=in_specs,
        out_specs=out_specs,
        out_shape=out_shape,
        compiler_params=_cparams(("parallel", "parallel")),
        name="inproj",
    )(x, mk_bf, mv_bf, aq, bq, ak, bk, *wts)


def _kvexp_kernel(ckv_ref, kpe_p_ref, kpe_s_ref, ak_ref, bk_ref, wk_ref, wv_ref, k_ref, v_ref):
    k_all, v_all = _expand_kv(ckv_ref[0].astype(BF16), kpe_p_ref[0], kpe_s_ref[0], wk_ref, wv_ref,
                              ak_ref[...], bk_ref[...])
    k_ref[0] = k_all
    v_ref[0] = v_all


def _kvexp(ckv, kpe_p, kpe_s, ak, bk, wk, wv, tl):
    b, l, _ = ckv.shape
    tok = lambda w: pl.BlockSpec((1, tl, w), lambda bi, li: (bi, li, 0))
    tab = pl.BlockSpec((tl, HEAD_PAD), lambda bi, li: (li, 0))
    return pl.pallas_call(
        _kvexp_kernel,
        grid=(b, l // tl),
        in_specs=[tok(KV_LORA), tok(HEAD_PAD), tok(HEAD_PAD), tab, tab] + [_const_spec(w.shape) for w in (wk, wv)],
        out_specs=(tok(QK_PAD), tok(MLA_WIDTH)),
        out_shape=(jax.ShapeDtypeStruct((b, l, QK_PAD), BF16), jax.ShapeDtypeStruct((b, l, MLA_WIDTH), BF16)),
        compiler_params=_cparams(("parallel", "parallel")),
        name="kvexp",
    )(ckv, kpe_p, kpe_s, ak, bk, wk, wv)


SCAN_LANES = 1024
S5_STEPS = 64


def _s5_kernel(u_ref, h0r_ref, h0i_ref, ar_ref, ai_ref, bmat_ref, cmat_ref, d_ref, wglu_ref,
               out_ref, hr_ref, hi_ref, ut_ref, sre_ref, sim_ref, ot_ref):
    i = pl.program_id(0)
    batch, steps, _ = u_ref.shape
    rows = batch * steps

    @pl.when(i == 0)
    def _():
        hr_ref[...] = h0r_ref[...]
        hi_ref[...] = h0i_ref[...]

    u_bm = u_ref[...].reshape(rows, SSM_WIDTH)
    for j in range(SSM_WIDTH // LANES):
        ut_ref[j] = u_bm[:, j * LANES:(j + 1) * LANES]
    u = jnp.concatenate(
        [jnp.concatenate([ut_ref[j, pl.ds(t, batch, stride=steps), :] for j in range(SSM_WIDTH // LANES)], axis=-1)
         for t in range(steps)], axis=0)
    ub = u.astype(BF16)
    blk_ch = SSM_WIDTH // SSM_BLOCKS
    blk_st = SSM_STATES // SSM_BLOCKS
    for blk in range(SSM_BLOCKS):
        bu = _dot(ub[:, blk * blk_ch:(blk + 1) * blk_ch], bmat_ref[blk])
        sre_ref[:, blk * blk_st:(blk + 1) * blk_st] = bu[:, :blk_st]
        sim_ref[:, blk * blk_st:(blk + 1) * blk_st] = bu[:, blk_st:]

    for c in range(SSM_STATES // SCAN_LANES):
        sl = slice(c * SCAN_LANES, (c + 1) * SCAN_LANES)
        a_re = ar_ref[:, sl]
        a_im = ai_ref[:, sl]

        def body(t, carry):
            h_re, h_im = carry
            r0 = pl.multiple_of(t * batch, batch)
            n_re = a_re * h_re - a_im * h_im + sre_ref[pl.ds(r0, batch), sl]
            n_im = a_re * h_im + a_im * h_re + sim_ref[pl.ds(r0, batch), sl]
            sre_ref[pl.ds(r0, batch), sl] = n_re
            sim_ref[pl.ds(r0, batch), sl] = n_im
            return n_re, n_im

        h_re, h_im = lax.fori_loop(0, steps, body, (hr_ref[:, sl], hi_ref[:, sl]), unroll=2)
        hr_ref[:, sl] = h_re
        hi_ref[:, sl] = h_im

    ys = []
    for blk in range(SSM_BLOCKS):
        st = slice(blk * blk_st, (blk + 1) * blk_st)
        ys.append(_dot(sre_ref[:, st].astype(BF16), cmat_ref[0, blk])
                  - _dot(sim_ref[:, st].astype(BF16), cmat_ref[1, blk]))
    y = jnp.concatenate(ys, axis=-1) + d_ref[...] * u
    z = jax.nn.gelu(y).astype(BF16)
    zz = _dot(z, wglu_ref[...])
    out = zz[:, :D_MODEL] * jax.nn.sigmoid(zz[:, D_MODEL:])
    for j in range(D_MODEL // LANES):
        ot_ref[j] = out[:, j * LANES:(j + 1) * LANES]
    for b in range(batch):
        out_ref[b] = jnp.concatenate([ot_ref[j, pl.ds(b, steps, stride=batch), :] for j in range(D_MODEL // LANES)],
                                     axis=-1).astype(BF16)


def _s5(u, h0_re, h0_im, a_re8, a_im8, bmat, cmat, ssm_d, w_glu_bf, steps):
    batch, l, _ = u.shape
    rows = batch * steps
    st = jax.ShapeDtypeStruct((batch, SSM_STATES), F32)
    return pl.pallas_call(
        _s5_kernel,
        grid=(l // steps,),
        in_specs=[pl.BlockSpec((batch, steps, SSM_WIDTH), lambda i: (0, i, 0)),
                  _const_spec((batch, SSM_STATES)), _const_spec((batch, SSM_STATES)),
                  _const_spec((batch, SSM_STATES)), _const_spec((batch, SSM_STATES)),
                  _const_spec(bmat.shape), _const_spec(cmat.shape),
                  _const_spec((1, SSM_WIDTH)), _const_spec(w_glu_bf.shape)],
        out_specs=(pl.BlockSpec((batch, steps, D_MODEL), lambda i: (0, i, 0)),
                   pl.BlockSpec((batch, SSM_STATES), lambda i: (0, 0)),
                   pl.BlockSpec((batch, SSM_STATES), lambda i: (0, 0))),
        out_shape=(jax.ShapeDtypeStruct((batch, l, D_MODEL), BF16), st, st),
        scratch_shapes=[pltpu.VMEM((SSM_WIDTH // LANES, rows, LANES), F32),
                        pltpu.VMEM((rows, SSM_STATES), F32), pltpu.VMEM((rows, SSM_STATES), F32),
                        pltpu.VMEM((D_MODEL // LANES, rows, LANES), F32)],
        compiler_params=_cparams(("arbitrary",)),
        name="s5",
    )(u, h0_re, h0_im, a_re8, a_im8, bmat, cmat, ssm_d.reshape(1, SSM_WIDTH), w_glu_bf)


def _attn_causal_kernel(q_ref, k_ref, v_ref, o_ref, s_ref, m_ref, l_ref, acc_ref, *, tq):
    i = pl.program_id(1)
    qc = lax.broadcasted_iota(jnp.int32, (tq, tq), 0) // CHUNK
    kc = lax.broadcasted_iota(jnp.int32, (tq, tq), 1) // CHUNK
    diag_mask = kc <= qc
    m_ref[...] = jnp.full(m_ref.shape, NEG_INF, F32)
    l_ref[...] = jnp.zeros(l_ref.shape, F32)
    acc_ref[...] = jnp.zeros(acc_ref.shape, F32)

    def scores(j, mask):
        r0 = pl.multiple_of(j * tq, tq)
        for hd in range(MLA_HEADS):
            ks = slice(hd * HEAD_PAD, (hd + 1) * HEAD_PAD)
            s = _dot_nt(q_ref[0, :, ks], k_ref[0, pl.ds(r0, tq), ks])
            if mask is not None:
                s = jnp.where(mask, s, NEG_INF)
            s_ref[hd, j] = s
            m_ref[hd] = jnp.maximum(m_ref[hd], jnp.maximum(s[:, :LANES], s[:, LANES:]))

    def pass1(j, c):
        scores(j, None)
        return c

    lax.fori_loop(0, i, pass1, 0)
    scores(i, diag_mask)
    for hd in range(MLA_HEADS):
        m_ref[hd] = jnp.broadcast_to(jnp.max(m_ref[hd], axis=-1, keepdims=True), (tq, LANES))

    def pass2(j, c):
        r0 = pl.multiple_of(j * tq, tq)
        for hd in range(MLA_HEADS):
            vs = slice(hd * V_HEAD, (hd + 1) * V_HEAD)
            s = s_ref[hd, j]
            mb = m_ref[hd]
            p0 = jnp.exp2(s[:, :LANES] - mb)
            p1 = jnp.exp2(s[:, LANES:] - mb)
            l_ref[hd] += p0 + p1
            p = jnp.concatenate([p0, p1], axis=-1).astype(BF16)
            acc_ref[hd] += _dot(p, v_ref[0, pl.ds(r0, tq), vs])
        return c

    lax.fori_loop(0, i + 1, pass2, 0)
    for hd in range(MLA_HEADS):
        vs = slice(hd * V_HEAD, (hd + 1) * V_HEAD)
        o_ref[0, :, vs] = (acc_ref[hd] / jnp.sum(l_ref[hd], axis=-1, keepdims=True)).astype(BF16)


def _attn_causal(q, k, v, tq):
    b, l, _ = q.shape
    assert tq == 2 * LANES and l % tq == 0 and tq % CHUNK == 0
    full = lambda w: pl.BlockSpec((1, l, w), lambda bi, qi: (bi, 0, 0))
    return pl.pallas_call(
        functools.partial(_attn_causal_kernel, tq=tq),
        grid=(b, l // tq),
        in_specs=[pl.BlockSpec((1, tq, QK_PAD), lambda bi, qi: (bi, qi, 0)), full(QK_PAD), full(MLA_WIDTH)],
        out_specs=pl.BlockSpec((1, tq, MLA_WIDTH), lambda bi, qi: (bi, qi, 0)),
        out_shape=jax.ShapeDtypeStruct((b, l, MLA_WIDTH), BF16),
        scratch_shapes=[pltpu.VMEM((MLA_HEADS, l // tq, tq, tq), F32),
                        pltpu.VMEM((MLA_HEADS, tq, LANES), F32),
                        pltpu.VMEM((MLA_HEADS, tq, LANES), F32),
                        pltpu.VMEM((MLA_HEADS, tq, V_HEAD), F32)],
        compiler_params=_cparams(("parallel", "arbitrary")),
        name="attn_prompt",
    )(q, k, v)


def _attn_past_kernel(q_ref, kp_ref, vp_ref, kn_ref, vn_ref, o_ref):
    for hd in range(MLA_HEADS):
        ks = slice(hd * HEAD_PAD, (hd + 1) * HEAD_PAD)
        vs = slice(hd * V_HEAD, (hd + 1) * V_HEAD)
        qh = q_ref[0, :, ks]
        s_past = _dot_nt(qh, kp_ref[0, :, ks])
        s_new = _dot_nt(qh, kn_ref[0, :, ks])
        m = jnp.maximum(jnp.max(s_past, axis=-1, keepdims=True), jnp.max(s_new, axis=-1, keepdims=True))
        p_past = jnp.exp2(s_past - m)
        p_new = jnp.exp2(s_new - m)
        l = jnp.sum(p_past, axis=-1, keepdims=True) + jnp.sum(p_new, axis=-1, keepdims=True)
        o = _dot(p_past.astype(BF16), vp_ref[0, :, vs]) + _dot(p_new.astype(BF16), vn_ref[0, :, vs])
        o_ref[0, :, vs] = (o / l).astype(BF16)


def _attn_past(q, k_past, v_past, k_new, v_new):
    b, lq, _ = q.shape
    lp = k_past.shape[1]
    assert lp % CHUNK == 0 and lq <= CHUNK
    spec = lambda n, w: pl.BlockSpec((1, n, w), lambda bi: (bi, 0, 0))
    return pl.pallas_call(
        _attn_past_kernel,
        grid=(b,),
        in_specs=[spec(lq, QK_PAD), spec(lp, QK_PAD), spec(lp, MLA_WIDTH), spec(lq, QK_PAD), spec(lq, MLA_WIDTH)],
        out_specs=spec(lq, MLA_WIDTH),
        out_shape=jax.ShapeDtypeStruct((b, lq, MLA_WIDTH), BF16),
        compiler_params=_cparams(("parallel",)),
        name="attn_sample",
    )(q, k_past, v_past, k_new, v_new)


ROUTE_LANES = LANES


def _first_argmax(v, lane, width):
    vmax = jnp.max(v, axis=-1, keepdims=True)
    idx = jnp.min(jnp.where(v == vmax, lane, width), axis=-1, keepdims=True)
    return vmax, idx


def _merge_kernel(x_ref, bra_ref, o_ref, om_ref, gate_ref, cnt0_ref, w_omla_ref, w_omem_ref, w_out_ref, g_ffn_ref,
                  w_rt_ref, b_rt_ref, x1_ref, h2_ref, route_ref, cnt_ref):
    first = (pl.program_id(0) == 0) & (pl.program_id(1) == 0)

    @pl.when(first)
    def _():
        cnt_ref[...] = cnt0_ref[...]

    br_b = _dot(o_ref[0], w_omla_ref[...])
    br_c = _dot(om_ref[0], w_omem_ref[...])
    g = gate_ref[0].astype(F32)
    merged = (g[:, :D_MODEL] * bra_ref[0].astype(F32) + g[:, D_MODEL:2 * D_MODEL] * br_b
              + g[:, 2 * D_MODEL:] * br_c)
    x1 = x_ref[0] + _dot(merged.astype(BF16), w_out_ref[...])
    x1_ref[0] = x1
    h2 = x1 * _rms_scale(x1, D_MODEL) * g_ffn_ref[...]
    h2_ref[0] = h2

    h2_hi = h2.astype(BF16)
    h2_lo = (h2 - h2_hi.astype(F32)).astype(BF16)
    hi_both = _dot(h2_hi, w_rt_ref[...])
    logits = (hi_both[:, :ROUTE_LANES] + hi_both[:, ROUTE_LANES:] + _dot(h2_lo, w_rt_ref[:, :ROUTE_LANES])
              + b_rt_ref[...])
    lane = lax.broadcasted_iota(jnp.int32, logits.shape, 1).astype(F32)
    ninf = jnp.float32(-jnp.inf)
    lg = jnp.where(lane < N_EXPERT_GROUPS, logits, ninf)
    lg_max, grp = _first_argmax(lg, lane, float(ROUTE_LANES))
    p_top = 1.0 / jnp.sum(jnp.exp(lg - lg_max), axis=-1, keepdims=True)
    lo = N_EXPERT_GROUPS + grp * EXPERTS_PER_GROUP
    in_grp = (lane >= lo) & (lane < lo + EXPERTS_PER_GROUP)
    le = jnp.where(in_grp, logits, ninf)
    v1, i1 = _first_argmax(le, lane, float(ROUTE_LANES))
    v2, i2 = _first_argmax(jnp.where(lane == i1, ninf, le), lane, float(ROUTE_LANES))
    e2 = jnp.exp(v2 - v1)
    w1 = p_top / (1.0 + e2)
    w2 = p_top * e2 / (1.0 + e2)
    elane = lane + N_EXPERT_GROUPS
    oh1 = elane == i1
    oh2 = elane == i2
    onehot = jnp.where(oh1 | oh2, 1.0, 0.0)
    tl = onehot.shape[0]
    tri = jnp.where(lax.broadcasted_iota(jnp.int32, (tl, tl), 0) > lax.broadcasted_iota(jnp.int32, (tl, tl), 1),
                    1.0, 0.0).astype(BF16)
    before = _dot(tri, onehot.astype(BF16)) + cnt_ref[...]
    r1 = jnp.sum(jnp.where(oh1, before, 0.0), axis=-1, keepdims=True)
    r2 = jnp.sum(jnp.where(oh2, before, 0.0), axis=-1, keepdims=True)
    cnt_ref[...] += jnp.sum(onehot, axis=0, keepdims=True)
    cols = (i1 - N_EXPERT_GROUPS, i2 - N_EXPERT_GROUPS, w1, w2, r1, r2)
    route = jnp.zeros_like(logits)
    for k, col in enumerate(cols):
        route = jnp.where(lane == k, col, route)
    route_ref[0] = route


def _merge(x, bra, o, om, gates, cnt0, wts, tl):
    b, l, _ = x.shape
    tok = lambda w: pl.BlockSpec((1, tl, w), lambda bi, li: (bi, li, 0))
    in_specs = [tok(D_MODEL), tok(D_MODEL), tok(MLA_WIDTH), tok(MEM_WIDTH), tok(N_BRANCHES * D_MODEL),
                _const_spec((1, ROUTE_LANES))] + [_const_spec(w.shape) for w in wts]
    return pl.pallas_call(
        _merge_kernel,
        grid=(b, l // tl),
        in_specs=in_specs,
        out_specs=(tok(D_MODEL), tok(D_MODEL), tok(ROUTE_LANES),
                   pl.BlockSpec((1, ROUTE_LANES), lambda bi, li: (0, 0))),
        out_shape=(jax.ShapeDtypeStruct((b, l, D_MODEL), F32), jax.ShapeDtypeStruct((b, l, D_MODEL), F32),
                   jax.ShapeDtypeStruct((b, l, ROUTE_LANES), F32), jax.ShapeDtypeStruct((1, ROUTE_LANES), F32)),
        compiler_params=_cparams(("arbitrary", "arbitrary")),
        name="merge",
    )(x, bra, o, om, gates, cnt0, *wts)


MOE_TILE = 256
MOE_DMA_TILE = 512


def _row_copy(src_ref, src_row, dst_ref, dst_row, sem):
    return pltpu.make_async_copy(src_ref.at[pl.ds(src_row, 1)], dst_ref.at[pl.ds(dst_row, 1)], sem)


def _dispatch_kernel(slot_ref, h2_ref, hs_in_hbm, hs_hbm, sem):
    del hs_in_hbm
    td = slot_ref.shape[2]

    def start(r, c):
        for k in range(2):
            _row_copy(h2_ref, r, hs_hbm, slot_ref[0, k, r], sem).start()
        return c

    def wait(r, c):
        for k in range(2):
            _row_copy(h2_ref, r, hs_hbm, slot_ref[0, k, r], sem).wait()
        return c

    lax.fori_loop(0, td, start, 0, unroll=8)
    lax.fori_loop(0, td, wait, 0, unroll=8)


def _dma_cparams():
    return pltpu.CompilerParams(dimension_semantics=("arbitrary",), vmem_limit_bytes=VMEM_LIMIT,
                                disable_bounds_checks=True)


def _dispatch(slots, h2, hs, td):
    n = h2.shape[0]
    return pl.pallas_call(
        _dispatch_kernel,
        grid=(n // td,),
        in_specs=[pl.BlockSpec((1, 2, td), lambda i: (i, 0, 0), memory_space=pltpu.SMEM),
                  pl.BlockSpec((td, D_MODEL), lambda i: (i, 0)), pl.BlockSpec(memory_space=pl.ANY)],
        out_specs=pl.BlockSpec(memory_space=pl.ANY),
        out_shape=jax.ShapeDtypeStruct(hs.shape, hs.dtype),
        scratch_shapes=[pltpu.SemaphoreType.DMA],
        input_output_aliases={2: 0},
        compiler_params=_dma_cparams(),
        name="dispatch",
    )(slots, h2, hs)


def _experts_kernel(tile_expert_ref, n_tiles_ref, hs_ref, w1_ref, w3_ref, w2_ref, y_ref, w13_bf_ref, w2_bf_ref):
    t = pl.program_id(0)
    used = t < n_tiles_ref[0]
    new_expert = (t == 0) | (tile_expert_ref[t] != tile_expert_ref[jnp.maximum(t - 1, 0)])

    @pl.when(used & new_expert)
    def _():
        w13_bf_ref[:, :EXPERT_FF] = w1_ref[0].astype(BF16)
        w13_bf_ref[:, EXPERT_FF:] = w3_ref[0].astype(BF16)
        w2_bf_ref[...] = w2_ref[0].astype(BF16)

    @pl.when(used)
    def _():
        a = _dot(hs_ref[...].astype(BF16), w13_bf_ref[...])
        hid = jax.nn.silu(a[:, :EXPERT_FF]) * a[:, EXPERT_FF:]
        y_ref[...] = _dot(hid.astype(BF16), w2_bf_ref[...])

    @pl.when(jnp.logical_not(used))
    def _():
        y_ref[...] = jnp.zeros(y_ref.shape, y_ref.dtype)


def _experts(tile_expert, n_tiles, hs, w_e1, w_e3, w_e2):
    s = hs.shape[0]
    w_in_spec = pl.BlockSpec((1, D_MODEL, EXPERT_FF), lambda t, te, nt: (te[t], 0, 0))
    grid_spec = pltpu.PrefetchScalarGridSpec(
        num_scalar_prefetch=2,
        grid=(s // MOE_TILE,),
        in_specs=[pl.BlockSpec((MOE_TILE, D_MODEL), lambda t, te, nt: (t, 0)), w_in_spec, w_in_spec,
                  pl.BlockSpec((1, EXPERT_FF, D_MODEL), lambda t, te, nt: (te[t], 0, 0))],
        out_specs=pl.BlockSpec((MOE_TILE, D_MODEL), lambda t, te, nt: (t, 0)),
        scratch_shapes=[pltpu.VMEM((D_MODEL, 2 * EXPERT_FF), BF16), pltpu.VMEM((EXPERT_FF, D_MODEL), BF16)],
    )
    return pl.pallas_call(
        _experts_kernel,
        grid_spec=grid_spec,
        out_shape=jax.ShapeDtypeStruct((s, D_MODEL), F32),
        compiler_params=_cparams(("arbitrary",)),
        name="experts",
    )(tile_expert, n_tiles, hs, w_e1, w_e3, w_e2)


def _combine_kernel(slot_ref, x1_ref, route_ref, ye_hbm, y_ref, rows_ref, sem):
    tc = slot_ref.shape[2]

    def start(r, c):
        for k in range(2):
            _row_copy(ye_hbm, slot_ref[0, k, r], rows_ref.at[k], r, sem).start()
        return c

    def wait(r, c):
        for k in range(2):
            _row_copy(ye_hbm, slot_ref[0, k, r], rows_ref.at[k], r, sem).wait()
        return c

    lax.fori_loop(0, tc, start, 0, unroll=8)
    lax.fori_loop(0, tc, wait, 0, unroll=8)
    route = route_ref[...]
    y = x1_ref[...]
    for k in range(2):
        gate = route[:, 2 + k:3 + k]
        y = y + gate * rows_ref[k]
    y_ref[...] = y


def _combine(slots, x1, route, ye, tc):
    n = x1.shape[0]
    return pl.pallas_call(
        _combine_kernel,
        grid=(n // tc,),
        in_specs=[pl.BlockSpec((1, 2, tc), lambda i: (i, 0, 0), memory_space=pltpu.SMEM),
                  pl.BlockSpec((tc, D_MODEL), lambda i: (i, 0)),
                  pl.BlockSpec((tc, ROUTE_LANES), lambda i: (i, 0)),
                  pl.BlockSpec(memory_space=pl.ANY)],
        out_specs=pl.BlockSpec((tc, D_MODEL), lambda i: (i, 0)),
        out_shape=jax.ShapeDtypeStruct((n, D_MODEL), F32),
        scratch_shapes=[pltpu.VMEM((2, tc, D_MODEL), F32), pltpu.SemaphoreType.DMA],
        compiler_params=_dma_cparams(),
        name="combine",
    )(slots, x1, route, ye)


def _rope_tables(pos, g, scale):
    half = QK_ROPE // 2
    inv = ROPE_BASE ** (-jnp.arange(half, dtype=F32) / half)
    ang = pos.astype(F32)[:, None] * inv[None, :]
    cos, sin = jnp.cos(ang), jnp.sin(ang)
    n = pos.shape[0]
    g1, g2 = g[QK_NOPE:QK_NOPE + half], g[QK_NOPE + half:QK_HEAD]
    pad = jnp.zeros((n, HEAD_PAD - QK_HEAD), F32)
    a = jnp.concatenate([jnp.broadcast_to(g[:QK_NOPE], (n, QK_NOPE)), g1 * cos, g2 * cos, pad], axis=-1)
    b = jnp.concatenate([jnp.zeros((n, QK_NOPE), F32), -g2 * sin, g1 * sin, pad], axis=-1)
    return a * scale, b * scale


def _pad_heads(w, per_head, keep):
    k = w.shape[0]
    w = w.reshape(k, MLA_HEADS, per_head)[:, :, :keep]
    return jnp.pad(w, ((0, 0), (0, 0), (0, HEAD_PAD - keep))).reshape(k, QK_PAD)


def _swap_rope_cols(w96):
    k = w96.shape[0]
    w = w96.reshape(k, MLA_HEADS, QK_HEAD)
    half = QK_ROPE // 2
    sw = jnp.concatenate([jnp.zeros((k, MLA_HEADS, QK_NOPE), w.dtype), w[:, :, QK_NOPE + half:],
                          w[:, :, QK_NOPE:QK_NOPE + half]], axis=-1)
    return sw.reshape(k, MLA_HEADS * QK_HEAD)


def _block_diag(w, rows_per_group, cols_per_group):
    gb = SSM_GROUPS // SSM_BLOCKS
    w = w.reshape(SSM_BLOCKS, gb, rows_per_group, cols_per_group)
    eye = jnp.eye(gb, dtype=w.dtype)
    out = jnp.einsum('bgrc,gh->bgrhc', w, eye)
    return out.reshape(SSM_BLOCKS, gb * rows_per_group, gb * cols_per_group)


def _layer(x, pos, h0_re, h0_im, mk, mv, past, cnt0, p, tl):
    b, l, _ = x.shape
    scale = math.log2(math.e) / math.sqrt(QK_HEAD)
    aq, bq = _rope_tables(pos, p["g_qn"], scale)
    ak, bk = _rope_tables(pos, p["g_kn"], 1.0)
    inproj_wts = (p["g_attn"], p["w_u"], p["w_q"], p["w_kv"], p["w_pe"], p["w_qm"], p["w_g"], p["g_qlat"],
                  p["wq_pad"], p["wq_swap"], p["g_kvlat"], p["wk_pad"], p["wv"], p["g_mqn"])
    u, q, k, v, c_kv, k_pe, om, gates = _inproj(x, mk.astype(BF16), mv.astype(BF16), (aq, bq, ak, bk),
                                                inproj_wts, tl)
    a_re = jnp.broadcast_to(p["ab_re"], (b, SSM_STATES))
    a_im = jnp.broadcast_to(p["ab_im"], (b, SSM_STATES))
    bra, h_re, h_im = _s5(u, h0_re, h0_im, a_re, a_im, p["bmat"], p["cmat"], p["ssm_d"], p["w_glu"], S5_STEPS)
    if past is None:
        o = _attn_causal(q, k, v, 256)
    else:
        past_ckv, past_kpe = past
        lp = past_ckv.shape[1]
        akp, bkp = _rope_tables(jnp.arange(lp), p["g_kn"], 1.0)
        half = QK_ROPE // 2
        lane_pad = lambda a: jnp.pad(a, ((0, 0), (0, 0), (QK_NOPE, HEAD_PAD - QK_HEAD)))
        kpe_p = lane_pad(past_kpe)
        kpe_s = lane_pad(jnp.concatenate([past_kpe[..., half:], past_kpe[..., :half]], axis=-1))
        k_past, v_past = _kvexp(past_ckv, kpe_p, kpe_s, akp, bkp, p["wk_pad"], p["wv"], 512)
        o = _attn_past(q, k_past, v_past, k, v)
    merge_wts = (p["w_o_mla"], p["w_o_mem"], p["w_out"], p["g_ffn"], p["w_rt"], p["b_rt"])
    x1, h2, route, cnt = _merge(x, bra, o, om, gates, cnt0, merge_wts, tl)
    n = b * l
    return (x1.reshape(n, D_MODEL), h2.reshape(n, D_MODEL), route.reshape(n, ROUTE_LANES), cnt), c_kv, k_pe, h_re, h_im


def _slots(route, offsets, tile):
    expert = route[:, 0:2].astype(jnp.int32)
    rank = route[:, 4:6].astype(jnp.int32)
    slot = offsets[expert] + rank
    return slot.reshape(-1, tile, 2).transpose(0, 2, 1)


def kernel(x_prompt, x_sample, cache_mla_ckv, cache_mla_kpe, cache_ssm_re, cache_ssm_im, cache_mem_k, cache_mem_v, mem_prompt, g_attn, w_in, ssm_a_re, ssm_a_im, ssm_log_dt, ssm_b_re, ssm_b_im, ssm_c_re, ssm_c_im, ssm_d, w_glu, g_qlat, w_uq, g_kvlat, w_ukv, g_qn, g_kn, w_o_mla, g_mem, w_mem_kv, g_mqn, g_mkn, w_o_mem, w_out, g_ffn, w_rg, b_rg, w_re, b_re, w_e1, w_e3, w_e2):
    assert g_attn.shape[0] == 1, "single-layer step"
    bp, lp, _ = x_prompt.shape
    bs, ls, _ = x_sample.shape
    past_len = cache_mla_ckv.shape[2]
    lyr = 0

    o1 = SSM_WIDTH
    o2 = o1 + Q_LORA
    o3 = o2 + KV_LORA
    o4 = o3 + QK_ROPE
    o5 = o4 + MEM_WIDTH
    w_in_bf = w_in[lyr].astype(BF16)
    row = lambda a: a.reshape(1, -1)
    p = {
        "g_attn": row(g_attn[lyr]), "w_u": w_in_bf[:, :o1], "w_q": w_in_bf[:, o1:o2], "w_kv": w_in_bf[:, o2:o3],
        "w_qm": w_in_bf[:, o4:o5], "w_g": w_in_bf[:, o5:],
        "g_qlat": row(g_qlat[lyr]), "g_kvlat": row(g_kvlat[lyr]), "g_mqn": row(g_mqn[lyr]),
        "g_qn": g_qn[lyr], "g_kn": g_kn[lyr], "g_ffn": row(g_ffn[lyr]),
        "ssm_d": ssm_d[lyr], "w_glu": w_glu[lyr].astype(BF16),
        "w_o_mla": w_o_mla[lyr].astype(BF16), "w_o_mem": w_o_mem[lyr].astype(BF16), "w_out": w_out[lyr].astype(BF16),
    }
    wuq = w_uq[lyr]
    p["wq_pad"] = _pad_heads(wuq, QK_HEAD, QK_HEAD).astype(BF16)
    p["wq_swap"] = _pad_heads(_swap_rope_cols(wuq), QK_HEAD, QK_HEAD).astype(BF16)
    wukv = w_ukv[lyr]
    p["wk_pad"] = _pad_heads(wukv, QK_NOPE + V_HEAD, QK_NOPE).astype(BF16)
    p["wv"] = wukv.reshape(KV_LORA, MLA_HEADS, QK_NOPE + V_HEAD)[:, :, QK_NOPE:].reshape(KV_LORA, MLA_WIDTH).astype(BF16)
    half = QK_ROPE // 2
    w_pe = w_in_bf[:, o3:o4]
    col_pad = lambda w: jnp.pad(w, ((0, 0), (QK_NOPE, HEAD_PAD - QK_HEAD)))
    p["w_pe"] = jnp.concatenate([col_pad(w_pe), col_pad(jnp.concatenate([w_pe[:, half:], w_pe[:, :half]], axis=-1))],
                                axis=-1)

    ab_re, ab_im, f_re, f_im = _zoh(ssm_a_re[lyr], ssm_a_im[lyr], ssm_log_dt[lyr])
    b_re_, b_im_ = ssm_b_re[lyr], ssm_b_im[lyr]
    bb_re = f_re[..., None] * b_re_ - f_im[..., None] * b_im_
    bb_im = f_re[..., None] * b_im_ + f_im[..., None] * b_re_
    to_cp = lambda w: jnp.swapaxes(w, 1, 2)
    p["bmat"] = jnp.concatenate([_block_diag(to_cp(bb_re), SSM_GROUP_CH, SSM_STATE),
                                 _block_diag(to_cp(bb_im), SSM_GROUP_CH, SSM_STATE)], axis=-1).astype(BF16)
    to_pc = lambda w: jnp.swapaxes(w, 1, 2)
    p["cmat"] = jnp.stack([_block_diag(to_pc(ssm_c_re[lyr]), SSM_STATE, SSM_GROUP_CH),
                           _block_diag(to_pc(ssm_c_im[lyr]), SSM_STATE, SSM_GROUP_CH)]).astype(BF16)
    p["ab_re"] = ab_re.reshape(1, SSM_STATES)
    p["ab_im"] = ab_im.reshape(1, SSM_STATES)

    w_rt = jnp.concatenate([w_rg[lyr], w_re[lyr]], axis=-1)
    w_rt = jnp.pad(w_rt, ((0, 0), (0, ROUTE_LANES - w_rt.shape[1])))
    w_rt_hi = w_rt.astype(BF16)
    p["w_rt"] = jnp.concatenate([w_rt_hi, (w_rt - w_rt_hi.astype(F32)).astype(BF16)], axis=-1)
    b_rt = jnp.concatenate([b_rg[lyr], b_re[lyr].reshape(-1)])
    p["b_rt"] = jnp.pad(b_rt, (0, ROUTE_LANES - b_rt.shape[0])).reshape(1, ROUTE_LANES)

    mk, mv = _memkv(mem_prompt.reshape(-1, D_MODEL), g_mem[lyr], w_mem_kv[lyr].astype(BF16), g_mkn[lyr])
    m_tok = mem_prompt.shape[1]
    mk3, mv3 = mk.reshape(bp, m_tok, MEM_WIDTH), mv.reshape(bp, m_tok, MEM_WIDTH)
    zeros = jnp.zeros((bp, SSM_STATES), F32)
    cnt0 = jnp.zeros((1, ROUTE_LANES), F32)
    (x1_p, h2p_p, route_p, cnt_p), ckv_p, kpe_p, sre_p, sim_p = _layer(
        x_prompt, jnp.arange(lp), zeros, zeros, mk3, mv3, None, cnt0, p, tl=256)

    (x1_s, h2p_s, route_s, cnt), ckv_s, kpe_s, sre_s, sim_s = _layer(
        x_sample, past_len + jnp.arange(ls), cache_ssm_re[lyr].reshape(bs, SSM_STATES),
        cache_ssm_im[lyr].reshape(bs, SSM_STATES), cache_mem_k[lyr].reshape(bs, -1, MEM_WIDTH),
        cache_mem_v[lyr].reshape(bs, -1, MEM_WIDTH), (cache_mla_ckv[lyr], cache_mla_kpe[lyr]), cnt_p, p, tl=ls)

    n_p, n_s = x1_p.shape[0], x1_s.shape[0]
    counts = cnt[0, :N_EXPERTS].astype(jnp.int32)
    padded = (counts + MOE_TILE - 1) // MOE_TILE * MOE_TILE
    ends = jnp.cumsum(padded)
    offsets = ends - padded
    max_tiles = 2 * (n_p + n_s) // MOE_TILE + N_EXPERTS
    tile_start = jnp.arange(max_tiles, dtype=jnp.int32) * MOE_TILE
    tile_expert = jnp.sum((ends[None, :] <= tile_start[:, None]).astype(jnp.int32), axis=1)
    tile_expert = jnp.minimum(tile_expert, N_EXPERTS - 1)
    n_tiles = (ends[-1:] // MOE_TILE).astype(jnp.int32)
    slots_p = _slots(route_p, offsets, MOE_DMA_TILE)
    slots_s = _slots(route_s, offsets, MOE_DMA_TILE)
    hs = jnp.zeros((max_tiles * MOE_TILE, D_MODEL), F32)
    hs = _dispatch(slots_p, h2p_p, hs, MOE_DMA_TILE)
    hs = _dispatch(slots_s, h2p_s, hs, MOE_DMA_TILE)
    ye = _experts(tile_expert, n_tiles, hs, w_e1[lyr], w_e3[lyr], w_e2[lyr])
    yp = _combine(slots_p, x1_p, route_p, ye, MOE_DMA_TILE).reshape(bp, lp, D_MODEL)
    ys = _combine(slots_s, x1_s, route_s, ye, MOE_DMA_TILE).reshape(bs, ls, D_MODEL)

    st = lambda a, bsz: a.reshape(1, bsz, SSM_GROUPS, SSM_STATE)
    mem_shape = (1, bp, m_tok, MEM_HEADS, MEM_HEAD)
    return (yp, ys, ckv_p[None], kpe_p[None], st(sre_p, bp), st(sim_p, bp), mk.reshape(mem_shape),
            mv.reshape(mem_shape), ckv_s[None], kpe_s[None], st(sre_s, bs), st(sim_s, bs))
```

```python
import functools
import math

import jax
import jax.numpy as jnp
import numpy as np
from jax import lax
from jax.experimental import pallas as pl
from jax.experimental.pallas import tpu as pltpu

D_MODEL = 1024
CHUNK = 64
RMS_EPS = 1e-6
SSM_GROUPS = 32
SSM_GROUP_CH = 16
SSM_WIDTH = SSM_GROUPS * SSM_GROUP_CH
SSM_STATE = 64
SSM_STATES = SSM_GROUPS * SSM_STATE
SSM_BLOCKS = 2
MLA_HEADS = 8
QK_NOPE = 64
QK_ROPE = 32
QK_HEAD = QK_NOPE + QK_ROPE
V_HEAD = 64
Q_LORA = 384
KV_LORA = 256
ROPE_BASE = 10000.0
MLA_WIDTH = MLA_HEADS * V_HEAD
MEM_HEADS = 4
MEM_HEAD = 128
MEM_WIDTH = MEM_HEADS * MEM_HEAD
N_BRANCHES = 3
N_EXPERT_GROUPS = 4
EXPERTS_PER_GROUP = 8
N_EXPERTS = N_EXPERT_GROUPS * EXPERTS_PER_GROUP
EXPERT_FF = 256

LANES = 128
SUBLANES = 8
HEAD_PAD = LANES
QK_PAD = MLA_HEADS * HEAD_PAD
VMEM_LIMIT = 56 * 1024 * 1024
PROMPT_TILE = 512

BF16 = jnp.bfloat16
F32 = jnp.float32
NEG_INF = -1e30


def _cparams(sem):
    return pltpu.CompilerParams(dimension_semantics=sem, vmem_limit_bytes=VMEM_LIMIT)


def _const_spec(shape):
    nd = len(shape)
    return pl.BlockSpec(shape, lambda *_: (0,) * nd, pipeline_mode=pl.Buffered(1))


def _rms_scale(xf, width):
    return lax.rsqrt(jnp.sum(xf * xf, axis=-1, keepdims=True) * (1.0 / width) + RMS_EPS)


def _dot(a, b):
    return jnp.dot(a, b, preferred_element_type=F32)


def _dot_nt(a, b):
    return lax.dot_general(a, b, (((1,), (1,)), ((), ())), preferred_element_type=F32)


def _zoh_kernel(lr_ref, li_ref, ldt_ref, abr_ref, abi_ref, fr_ref, fi_ref):
    lr = lr_ref[...]
    li = li_ref[...]
    dt = jnp.exp(ldt_ref[...])
    mag = jnp.exp(lr * dt)
    ab_re = mag * jnp.cos(li * dt)
    ab_im = mag * jnp.sin(li * dt)
    den = lr * lr + li * li
    nr = ab_re - 1.0
    ni = ab_im
    abr_ref[...] = ab_re
    abi_ref[...] = ab_im
    fr_ref[...] = (nr * lr + ni * li) / den
    fi_ref[...] = (ni * lr - nr * li) / den


def _zoh(a_re, a_im, log_dt):
    shp = jax.ShapeDtypeStruct((SSM_GROUPS, SSM_STATE), F32)
    return pl.pallas_call(_zoh_kernel, out_shape=(shp, shp, shp, shp), name="zoh")(
        a_re, a_im, log_dt.reshape(SSM_GROUPS, 1))


def _expand_kv(ckv_bf, kpe_p, kpe_s, wk_ref, wv_ref, vone_ref, ak, bk):
    k_nope = _dot(ckv_bf, wk_ref[...])
    v = _dot(ckv_bf, wv_ref[...]) + vone_ref[...]
    ss_pe = jnp.sum(kpe_p * kpe_p, axis=-1, keepdims=True)
    rot = kpe_p * ak + kpe_s * bk
    heads = []
    for h in range(MLA_HEADS):
        kh = k_nope[:, h * HEAD_PAD:(h + 1) * HEAD_PAD]
        ss = jnp.sum(kh * kh, axis=-1, keepdims=True) + ss_pe
        rs = lax.rsqrt(ss * (1.0 / QK_HEAD) + RMS_EPS)
        heads.append((rs * (kh * ak + rot)).astype(BF16))
    return jnp.concatenate(heads, axis=-1), v.astype(BF16)


def _memkv_kernel(mem_ref, g_ref, w_ref, gk_ref, k_ref, v_ref):
    x = mem_ref[...]
    h = (x * _rms_scale(x, D_MODEL) * g_ref[...]).astype(BF16)
    kv = _dot(h, w_ref[...])
    gk = gk_ref[...]
    for hd in range(MEM_HEADS):
        kh = kv[:, hd * MEM_HEAD:(hd + 1) * MEM_HEAD]
        k_ref[:, hd * MEM_HEAD:(hd + 1) * MEM_HEAD] = kh * _rms_scale(kh, MEM_HEAD) * gk
    v_ref[...] = kv[:, MEM_WIDTH:]


def _memkv(mem2d, g_mem, w_mem_kv_bf, g_mkn):
    n = mem2d.shape[0]
    tm = 256
    out = jax.ShapeDtypeStruct((n, MEM_WIDTH), F32)
    return pl.pallas_call(
        _memkv_kernel,
        grid=(n // tm,),
        in_specs=[pl.BlockSpec((tm, D_MODEL), lambda i: (i, 0)),
                  _const_spec((1, D_MODEL)),
                  _const_spec((D_MODEL, 2 * MEM_WIDTH)),
                  _const_spec((1, MEM_HEAD))],
        out_specs=(pl.BlockSpec((tm, MEM_WIDTH), lambda i: (i, 0)),
                   pl.BlockSpec((tm, MEM_WIDTH), lambda i: (i, 0))),
        out_shape=(out, out),
        compiler_params=_cparams(("parallel",)),
        name="memkv",
    )(mem2d, g_mem.reshape(1, D_MODEL), w_mem_kv_bf, g_mkn.reshape(1, MEM_HEAD))


def _inproj_kernel(x_ref, mk_ref, mv_ref, aq_ref, bq_ref, ak_ref, bk_ref,
                   g_attn_ref, w_u_ref, w_q_ref, w_kv_ref, w_pe_ref, w_qm_ref, w_g_ref,
                   g_qlat_ref, wq_ref, wqs_ref, g_kvlat_ref, wk_ref, wv_ref, vone_ref, g_mqn_ref,
                   u_ref, q_ref, k_ref, v_ref, ckv_ref, kpe_ref, om_ref, gate_ref):
    x = x_ref[0]
    h = (x * _rms_scale(x, D_MODEL) * g_attn_ref[...]).astype(BF16)

    u_ref[0] = _dot(h, w_u_ref[...])
    gate_ref[0] = jax.nn.sigmoid(_dot(h, w_g_ref[...])).astype(BF16)

    q_lat = _dot(h, w_q_ref[...])
    qn = (q_lat * _rms_scale(q_lat, Q_LORA) * g_qlat_ref[...]).astype(BF16)
    q_up = _dot(qn, wq_ref[...])
    q_sw = _dot(qn, wqs_ref[...])
    aq = aq_ref[...]
    bq = bq_ref[...]
    for hd in range(MLA_HEADS):
        sl = slice(hd * HEAD_PAD, (hd + 1) * HEAD_PAD)
        qh = q_up[:, sl]
        rs = _rms_scale(qh, QK_HEAD)
        q_ref[0, :, sl] = (rs * (qh * aq + q_sw[:, sl] * bq)).astype(BF16)

    kv_lat = _dot(h, w_kv_ref[...])
    c_kv = kv_lat * _rms_scale(kv_lat, KV_LORA) * g_kvlat_ref[...]
    ckv_ref[0] = c_kv
    kpe_p = _dot(h, w_pe_ref[:, :HEAD_PAD])
    kpe_s = _dot(h, w_pe_ref[:, HEAD_PAD:])
    kpe_ref[0] = kpe_p[:, QK_NOPE:QK_HEAD]
    k_all, v_all = _expand_kv(c_kv.astype(BF16), kpe_p, kpe_s, wk_ref, wv_ref, vone_ref, ak_ref[...], bk_ref[...])
    k_ref[0] = k_all
    v_ref[0] = v_all

    q_mem = _dot(h, w_qm_ref[...])
    gq = g_mqn_ref[...] * (1.0 / math.sqrt(MEM_HEAD))
    for hd in range(MEM_HEADS):
        sl = slice(hd * MEM_HEAD, (hd + 1) * MEM_HEAD)
        qh = q_mem[:, sl]
        qh = (qh * _rms_scale(qh, MEM_HEAD) * gq).astype(BF16)
        s = _dot_nt(qh, mk_ref[0, :, sl])
        p = jnp.exp(s - jnp.max(s, axis=-1, keepdims=True))
        o = _dot(p.astype(BF16), mv_ref[0, :, sl])
        om_ref[0, :, sl] = (o / jnp.sum(p, axis=-1, keepdims=True)).astype(BF16)


def _inproj(x, mk_bf, mv_bf, tabs, wts, tl):
    b, l, _ = x.shape
    m = mk_bf.shape[1]
    aq, bq, ak, bk = tabs
    tok = lambda w: pl.BlockSpec((1, tl, w), lambda bi, li: (bi, li, 0))
    tab = pl.BlockSpec((tl, HEAD_PAD), lambda bi, li: (li, 0))
    memspec = pl.BlockSpec((1, m, MEM_WIDTH), lambda bi, li: (bi, 0, 0))
    in_specs = [tok(D_MODEL), memspec, memspec, tab, tab, tab, tab] + [_const_spec(w.shape) for w in wts]
    out_shape = (
        jax.ShapeDtypeStruct((b, l, SSM_WIDTH), F32),
        jax.ShapeDtypeStruct((b, l, QK_PAD), BF16),
        jax.ShapeDtypeStruct((b, l, QK_PAD), BF16),
        jax.ShapeDtypeStruct((b, l, QK_PAD), BF16),
        jax.ShapeDtypeStruct((b, l, KV_LORA), F32),
        jax.ShapeDtypeStruct((b, l, QK_ROPE), F32),
        jax.ShapeDtypeStruct((b, l, MEM_WIDTH), BF16),
        jax.ShapeDtypeStruct((b, l, N_BRANCHES * D_MODEL), BF16),
    )
    out_specs = (
        tok(SSM_WIDTH), tok(QK_PAD), tok(QK_PAD), tok(QK_PAD), tok(KV_LORA), tok(QK_ROPE), tok(MEM_WIDTH),
        tok(N_BRANCHES * D_MODEL),
    )
    return pl.pallas_call(
        _inproj_kernel,
        grid=(b, l // tl),
        in_specs=in_specs,
        out_specs=out_specs,
        out_shape=out_shape,
        compiler_params=_cparams(("parallel", "parallel")),
        name="inproj",
    )(x, mk_bf, mv_bf, aq, bq, ak, bk, *wts)


def _kvexp_kernel(ckv_ref, kpe_p_ref, kpe_s_ref, ak_ref, bk_ref, wk_ref, wv_ref, vone_ref, k_ref, v_ref):
    k_all, v_all = _expand_kv(ckv_ref[0].astype(BF16), kpe_p_ref[0], kpe_s_ref[0], wk_ref, wv_ref, vone_ref,
                              ak_ref[...], bk_ref[...])
    k_ref[0] = k_all
    v_ref[0] = v_all


def _kvexp(ckv, kpe_p, kpe_s, ak, bk, wk, wv, vone, tl):
    b, l, _ = ckv.shape
    tok = lambda w: pl.BlockSpec((1, tl, w), lambda bi, li: (bi, li, 0))
    tab = pl.BlockSpec((tl, HEAD_PAD), lambda bi, li: (li, 0))
    return pl.pallas_call(
        _kvexp_kernel,
        grid=(b, l // tl),
        in_specs=[tok(KV_LORA), tok(HEAD_PAD), tok(HEAD_PAD), tab, tab]
                 + [_const_spec(w.shape) for w in (wk, wv, vone)],
        out_specs=(tok(QK_PAD), tok(QK_PAD)),
        out_shape=(jax.ShapeDtypeStruct((b, l, QK_PAD), BF16), jax.ShapeDtypeStruct((b, l, QK_PAD), BF16)),
        compiler_params=_cparams(("parallel", "parallel")),
        name="kvexp",
    )(ckv, kpe_p, kpe_s, ak, bk, wk, wv, vone)


SCAN_LANES = 1024
S5_STEPS = 64


def _s5_kernel(u_ref, h0r_ref, h0i_ref, ar_ref, ai_ref, bmat_ref, cmat_ref, d_ref, wglu_ref,
               out_ref, hr_ref, hi_ref, ut_ref, sre_ref, sim_ref, ot_ref):
    i = pl.program_id(0)
    batch, steps, _ = u_ref.shape
    rows = batch * steps

    @pl.when(i == 0)
    def _():
        hr_ref[...] = h0r_ref[...]
        hi_ref[...] = h0i_ref[...]

    u_bm = u_ref[...].reshape(rows, SSM_WIDTH)
    for j in range(SSM_WIDTH // LANES):
        ut_ref[j] = u_bm[:, j * LANES:(j + 1) * LANES]
    u = jnp.concatenate(
        [jnp.concatenate([ut_ref[j, pl.ds(t, batch, stride=steps), :] for j in range(SSM_WIDTH // LANES)], axis=-1)
         for t in range(steps)], axis=0)
    ub = u.astype(BF16)
    blk_ch = SSM_WIDTH // SSM_BLOCKS
    blk_st = SSM_STATES // SSM_BLOCKS
    for blk in range(SSM_BLOCKS):
        bu = _dot(ub[:, blk * blk_ch:(blk + 1) * blk_ch], bmat_ref[blk])
        sre_ref[:, blk * blk_st:(blk + 1) * blk_st] = bu[:, :blk_st]
        sim_ref[:, blk * blk_st:(blk + 1) * blk_st] = bu[:, blk_st:]

    for c in range(SSM_STATES // SCAN_LANES):
        sl = slice(c * SCAN_LANES, (c + 1) * SCAN_LANES)
        a_re = ar_ref[:, sl]
        a_im = ai_ref[:, sl]

        def body(t, carry):
            h_re, h_im = carry
            r0 = pl.multiple_of(t * batch, batch)
            n_re = a_re * h_re - a_im * h_im + sre_ref[pl.ds(r0, batch), sl]
            n_im = a_re * h_im + a_im * h_re + sim_ref[pl.ds(r0, batch), sl]
            sre_ref[pl.ds(r0, batch), sl] = n_re
            sim_ref[pl.ds(r0, batch), sl] = n_im
            return n_re, n_im

        h_re, h_im = lax.fori_loop(0, steps, body, (hr_ref[:, sl], hi_ref[:, sl]), unroll=2)
        hr_ref[:, sl] = h_re
        hi_ref[:, sl] = h_im

    ys = []
    for blk in range(SSM_BLOCKS):
        st = slice(blk * blk_st, (blk + 1) * blk_st)
        ys.append(_dot(sre_ref[:, st].astype(BF16), cmat_ref[0, blk])
                  - _dot(sim_ref[:, st].astype(BF16), cmat_ref[1, blk]))
    y = jnp.concatenate(ys, axis=-1) + d_ref[...] * u
    z = jax.nn.gelu(y).astype(BF16)
    zz = _dot(z, wglu_ref[...])
    out = zz[:, :D_MODEL] * jax.nn.sigmoid(zz[:, D_MODEL:])
    for j in range(D_MODEL // LANES):
        ot_ref[j] = out[:, j * LANES:(j + 1) * LANES]
    for b in range(batch):
        out_ref[b] = jnp.concatenate([ot_ref[j, pl.ds(b, steps, stride=batch), :] for j in range(D_MODEL // LANES)],
                                     axis=-1).astype(BF16)


def _s5(u, h0_re, h0_im, a_re8, a_im8, bmat, cmat, ssm_d, w_glu_bf, steps):
    batch, l, _ = u.shape
    rows = batch * steps
    st = jax.ShapeDtypeStruct((batch, SSM_STATES), F32)
    return pl.pallas_call(
        _s5_kernel,
        grid=(l // steps,),
        in_specs=[pl.BlockSpec((batch, steps, SSM_WIDTH), lambda i: (0, i, 0)),
                  _const_spec((batch, SSM_STATES)), _const_spec((batch, SSM_STATES)),
                  _const_spec((batch, SSM_STATES)), _const_spec((batch, SSM_STATES)),
                  _const_spec(bmat.shape), _const_spec(cmat.shape),
                  _const_spec((1, SSM_WIDTH)), _const_spec(w_glu_bf.shape)],
        out_specs=(pl.BlockSpec((batch, steps, D_MODEL), lambda i: (0, i, 0)),
                   pl.BlockSpec((batch, SSM_STATES), lambda i: (0, 0)),
                   pl.BlockSpec((batch, SSM_STATES), lambda i: (0, 0))),
        out_shape=(jax.ShapeDtypeStruct((batch, l, D_MODEL), BF16), st, st),
        scratch_shapes=[pltpu.VMEM((SSM_WIDTH // LANES, rows, LANES), F32),
                        pltpu.VMEM((rows, SSM_STATES), F32), pltpu.VMEM((rows, SSM_STATES), F32),
                        pltpu.VMEM((D_MODEL // LANES, rows, LANES), F32)],
        compiler_params=_cparams(("arbitrary",)),
        name="s5",
    )(u, h0_re, h0_im, a_re8, a_im8, bmat, cmat, ssm_d.reshape(1, SSM_WIDTH), w_glu_bf)


def _attn_causal_kernel(q_ref, k_ref, v_ref, o_ref, s_ref, m_ref, acc_ref, *, tq):
    i = pl.program_id(1)
    qc = lax.broadcasted_iota(jnp.int32, (tq, tq), 0) // CHUNK
    kc = lax.broadcasted_iota(jnp.int32, (tq, tq), 1) // CHUNK
    diag_mask = kc <= qc
    m_ref[...] = jnp.full(m_ref.shape, NEG_INF, F32)
    acc_ref[...] = jnp.zeros(acc_ref.shape, F32)

    def scores(j, mask):
        r0 = pl.multiple_of(j * tq, tq)
        for hd in range(MLA_HEADS):
            ks = slice(hd * HEAD_PAD, (hd + 1) * HEAD_PAD)
            s = _dot_nt(q_ref[0, :, ks], k_ref[0, pl.ds(r0, tq), ks])
            if mask is not None:
                s = jnp.where(mask, s, NEG_INF)
            s_ref[hd, j] = s
            m_ref[hd] = jnp.maximum(m_ref[hd], jnp.maximum(s[:, :LANES], s[:, LANES:]))

    def pass1(j, c):
        scores(j, None)
        return c

    lax.fori_loop(0, i, pass1, 0)
    scores(i, diag_mask)
    for hd in range(MLA_HEADS):
        m_ref[hd] = jnp.broadcast_to(jnp.max(m_ref[hd], axis=-1, keepdims=True), (tq, LANES))

    def pass2(j, c):
        r0 = pl.multiple_of(j * tq, tq)
        for hd in range(MLA_HEADS):
            ks = slice(hd * HEAD_PAD, (hd + 1) * HEAD_PAD)
            s = s_ref[hd, j]
            mb = m_ref[hd]
            p = jnp.concatenate([jnp.exp2(s[:, :LANES] - mb), jnp.exp2(s[:, LANES:] - mb)], axis=-1).astype(BF16)
            acc_ref[hd] += _dot(p, v_ref[0, pl.ds(r0, tq), ks])
        return c

    lax.fori_loop(0, i + 1, pass2, 0)
    for hd in range(MLA_HEADS):
        acc = acc_ref[hd]
        o_ref[0, :, hd * V_HEAD:(hd + 1) * V_HEAD] = (acc[:, :V_HEAD] / acc[:, V_HEAD:V_HEAD + 1]).astype(BF16)


def _attn_causal(q, k, v, tq):
    b, l, _ = q.shape
    assert tq == 2 * LANES and l % tq == 0 and tq % CHUNK == 0
    full = lambda w: pl.BlockSpec((1, l, w), lambda bi, qi: (bi, 0, 0))
    return pl.pallas_call(
        functools.partial(_attn_causal_kernel, tq=tq),
        grid=(b, l // tq),
        in_specs=[pl.BlockSpec((1, tq, QK_PAD), lambda bi, qi: (bi, qi, 0)), full(QK_PAD), full(QK_PAD)],
        out_specs=pl.BlockSpec((1, tq, MLA_WIDTH), lambda bi, qi: (bi, qi, 0)),
        out_shape=jax.ShapeDtypeStruct((b, l, MLA_WIDTH), BF16),
        scratch_shapes=[pltpu.VMEM((MLA_HEADS, l // tq, tq, tq), F32),
                        pltpu.VMEM((MLA_HEADS, tq, LANES), F32),
                        pltpu.VMEM((MLA_HEADS, tq, HEAD_PAD), F32)],
        compiler_params=_cparams(("parallel", "arbitrary")),
        name="attn_prompt",
    )(q, k, v)


def _attn_past_kernel(q_ref, kp_ref, vp_ref, kn_ref, vn_ref, o_ref):
    for hd in range(MLA_HEADS):
        ks = slice(hd * HEAD_PAD, (hd + 1) * HEAD_PAD)
        qh = q_ref[0, :, ks]
        s_past = _dot_nt(qh, kp_ref[0, :, ks])
        s_new = _dot_nt(qh, kn_ref[0, :, ks])
        m = jnp.maximum(jnp.max(s_past, axis=-1, keepdims=True), jnp.max(s_new, axis=-1, keepdims=True))
        p_past = jnp.exp2(s_past - m)
        p_new = jnp.exp2(s_new - m)
        o = _dot(p_past.astype(BF16), vp_ref[0, :, ks]) + _dot(p_new.astype(BF16), vn_ref[0, :, ks])
        o_ref[0, :, hd * V_HEAD:(hd + 1) * V_HEAD] = (o[:, :V_HEAD] / o[:, V_HEAD:V_HEAD + 1]).astype(BF16)


def _attn_past(q, k_past, v_past, k_new, v_new):
    b, lq, _ = q.shape
    lp = k_past.shape[1]
    assert lp % CHUNK == 0 and lq <= CHUNK
    spec = lambda n, w: pl.BlockSpec((1, n, w), lambda bi: (bi, 0, 0))
    return pl.pallas_call(
        _attn_past_kernel,
        grid=(b,),
        in_specs=[spec(lq, QK_PAD), spec(lp, QK_PAD), spec(lp, QK_PAD), spec(lq, QK_PAD), spec(lq, QK_PAD)],
        out_specs=spec(lq, MLA_WIDTH),
        out_shape=jax.ShapeDtypeStruct((b, lq, MLA_WIDTH), BF16),
        compiler_params=_cparams(("parallel",)),
        name="attn_sample",
    )(q, k_past, v_past, k_new, v_new)


ROUTE_LANES = LANES


def _first_argmax(v, lane, width):
    vmax = jnp.max(v, axis=-1, keepdims=True)
    idx = jnp.min(jnp.where(v == vmax, lane, width), axis=-1, keepdims=True)
    return vmax, idx


def _merge_kernel(x_ref, bra_ref, o_ref, om_ref, gate_ref, cnt0_ref, w_omla_ref, w_omem_ref, w_out_ref, g_ffn_ref,
                  w_rt_ref, b_rt_ref, x1_ref, h2_ref, route_ref, cnt_ref):
    first = (pl.program_id(0) == 0) & (pl.program_id(1) == 0)

    @pl.when(first)
    def _():
        cnt_ref[...] = cnt0_ref[...]

    br_b = _dot(o_ref[0], w_omla_ref[...])
    br_c = _dot(om_ref[0], w_omem_ref[...])
    g = gate_ref[0].astype(F32)
    merged = (g[:, :D_MODEL] * bra_ref[0].astype(F32) + g[:, D_MODEL:2 * D_MODEL] * br_b
              + g[:, 2 * D_MODEL:] * br_c)
    x1 = x_ref[0] + _dot(merged.astype(BF16), w_out_ref[...])
    x1_ref[0] = x1
    h2 = x1 * _rms_scale(x1, D_MODEL) * g_ffn_ref[...]
    h2_ref[0] = h2

    h2_hi = h2.astype(BF16)
    h2_lo = (h2 - h2_hi.astype(F32)).astype(BF16)
    hi_both = _dot(h2_hi, w_rt_ref[...])
    logits = (hi_both[:, :ROUTE_LANES] + hi_both[:, ROUTE_LANES:] + _dot(h2_lo, w_rt_ref[:, :ROUTE_LANES])
              + b_rt_ref[...])
    lane = lax.broadcasted_iota(jnp.int32, logits.shape, 1).astype(F32)
    ninf = jnp.float32(-jnp.inf)
    lg = jnp.where(lane < N_EXPERT_GROUPS, logits, ninf)
    lg_max, grp = _first_argmax(lg, lane, float(ROUTE_LANES))
    p_top = 1.0 / jnp.sum(jnp.exp(lg - lg_max), axis=-1, keepdims=True)
    lo = N_EXPERT_GROUPS + grp * EXPERTS_PER_GROUP
    in_grp = (lane >= lo) & (lane < lo + EXPERTS_PER_GROUP)
    le = jnp.where(in_grp, logits, ninf)
    v1, i1 = _first_argmax(le, lane, float(ROUTE_LANES))
    v2, i2 = _first_argmax(jnp.where(lane == i1, ninf, le), lane, float(ROUTE_LANES))
    e2 = jnp.exp(v2 - v1)
    w1 = p_top / (1.0 + e2)
    w2 = p_top * e2 / (1.0 + e2)
    elane = lane + N_EXPERT_GROUPS
    oh1 = elane == i1
    oh2 = elane == i2
    onehot = jnp.where(oh1 | oh2, 1.0, 0.0)
    tl = onehot.shape[0]
    tri = jnp.where(lax.broadcasted_iota(jnp.int32, (tl, tl), 0) > lax.broadcasted_iota(jnp.int32, (tl, tl), 1),
                    1.0, 0.0).astype(BF16)
    before = _dot(tri, onehot.astype(BF16)) + cnt_ref[...]
    r1 = jnp.sum(jnp.where(oh1, before, 0.0), axis=-1, keepdims=True)
    r2 = jnp.sum(jnp.where(oh2, before, 0.0), axis=-1, keepdims=True)
    cnt_ref[...] += jnp.sum(onehot, axis=0, keepdims=True)
    cols = (i1 - N_EXPERT_GROUPS, i2 - N_EXPERT_GROUPS, w1, w2, r1, r2)
    route = jnp.zeros_like(logits)
    for k, col in enumerate(cols):
        route = jnp.where(lane == k, col, route)
    route_ref[0] = route


def _merge(x, bra, o, om, gates, cnt0, wts, tl):
    b, l, _ = x.shape
    tok = lambda w: pl.BlockSpec((1, tl, w), lambda bi, li: (bi, li, 0))
    in_specs = [tok(D_MODEL), tok(D_MODEL), tok(MLA_WIDTH), tok(MEM_WIDTH), tok(N_BRANCHES * D_MODEL),
                _const_spec((1, ROUTE_LANES))] + [_const_spec(w.shape) for w in wts]
    return pl.pallas_call(
        _merge_kernel,
        grid=(b, l // tl),
        in_specs=in_specs,
        out_specs=(tok(D_MODEL), tok(D_MODEL), tok(ROUTE_LANES),
                   pl.BlockSpec((1, ROUTE_LANES), lambda bi, li: (0, 0))),
        out_shape=(jax.ShapeDtypeStruct((b, l, D_MODEL), F32), jax.ShapeDtypeStruct((b, l, D_MODEL), F32),
                   jax.ShapeDtypeStruct((b, l, ROUTE_LANES), F32), jax.ShapeDtypeStruct((1, ROUTE_LANES), F32)),
        compiler_params=_cparams(("arbitrary", "arbitrary")),
        name="merge",
    )(x, bra, o, om, gates, cnt0, *wts)


MOE_TILE = 256
MOE_DMA_TILE = 512


def _row_copy(src_ref, src_row, dst_ref, dst_row, sem):
    return pltpu.make_async_copy(src_ref.at[pl.ds(src_row, 1)], dst_ref.at[pl.ds(dst_row, 1)], sem)


def _dispatch_kernel(slot_ref, h2_ref, hs_in_hbm, hs_hbm, sem):
    del hs_in_hbm
    td = slot_ref.shape[2]

    def start(r, c):
        for k in range(2):
            _row_copy(h2_ref, r, hs_hbm, slot_ref[0, k, r], sem).start()
        return c

    def wait(r, c):
        for k in range(2):
            _row_copy(h2_ref, r, hs_hbm, slot_ref[0, k, r], sem).wait()
        return c

    lax.fori_loop(0, td, start, 0, unroll=8)
    lax.fori_loop(0, td, wait, 0, unroll=8)


def _dma_cparams():
    return pltpu.CompilerParams(dimension_semantics=("arbitrary",), vmem_limit_bytes=VMEM_LIMIT,
                                disable_bounds_checks=True)


def _dispatch(slots, h2, hs, td):
    n = h2.shape[0]
    return pl.pallas_call(
        _dispatch_kernel,
        grid=(n // td,),
        in_specs=[pl.BlockSpec((1, 2, td), lambda i: (i, 0, 0), memory_space=pltpu.SMEM),
                  pl.BlockSpec((td, D_MODEL), lambda i: (i, 0)), pl.BlockSpec(memory_space=pl.ANY)],
        out_specs=pl.BlockSpec(memory_space=pl.ANY),
        out_shape=jax.ShapeDtypeStruct(hs.shape, hs.dtype),
        scratch_shapes=[pltpu.SemaphoreType.DMA],
        input_output_aliases={2: 0},
        compiler_params=_dma_cparams(),
        name="dispatch",
    )(slots, h2, hs)


def _experts_kernel(tile_expert_ref, n_tiles_ref, hs_ref, w1_ref, w3_ref, w2_ref, y_ref, w13_bf_ref, w2_bf_ref):
    t = pl.program_id(0)
    used = t < n_tiles_ref[0]
    new_expert = (t == 0) | (tile_expert_ref[t] != tile_expert_ref[jnp.maximum(t - 1, 0)])

    @pl.when(used & new_expert)
    def _():
        w13_bf_ref[:, :EXPERT_FF] = w1_ref[0].astype(BF16)
        w13_bf_ref[:, EXPERT_FF:] = w3_ref[0].astype(BF16)
        w2_bf_ref[...] = w2_ref[0].astype(BF16)

    @pl.when(used)
    def _():
        a = _dot(hs_ref[...].astype(BF16), w13_bf_ref[...])
        hid = jax.nn.silu(a[:, :EXPERT_FF]) * a[:, EXPERT_FF:]
        y_ref[...] = _dot(hid.astype(BF16), w2_bf_ref[...])

    @pl.when(jnp.logical_not(used))
    def _():
        y_ref[...] = jnp.zeros(y_ref.shape, y_ref.dtype)


def _experts(tile_expert, n_tiles, hs, w_e1, w_e3, w_e2):
    s = hs.shape[0]
    w_in_spec = pl.BlockSpec((1, D_MODEL, EXPERT_FF), lambda t, te, nt: (te[t], 0, 0))
    grid_spec = pltpu.PrefetchScalarGridSpec(
        num_scalar_prefetch=2,
        grid=(s // MOE_TILE,),
        in_specs=[pl.BlockSpec((MOE_TILE, D_MODEL), lambda t, te, nt: (t, 0)), w_in_spec, w_in_spec,
                  pl.BlockSpec((1, EXPERT_FF, D_MODEL), lambda t, te, nt: (te[t], 0, 0))],
        out_specs=pl.BlockSpec((MOE_TILE, D_MODEL), lambda t, te, nt: (t, 0)),
        scratch_shapes=[pltpu.VMEM((D_MODEL, 2 * EXPERT_FF), BF16), pltpu.VMEM((EXPERT_FF, D_MODEL), BF16)],
    )
    return pl.pallas_call(
        _experts_kernel,
        grid_spec=grid_spec,
        out_shape=jax.ShapeDtypeStruct((s, D_MODEL), F32),
        compiler_params=_cparams(("arbitrary",)),
        name="experts",
    )(tile_expert, n_tiles, hs, w_e1, w_e3, w_e2)


def _combine_kernel(slot_ref, x1_ref, route_ref, ye_hbm, y_ref, rows_ref, sem):
    tc = slot_ref.shape[2]

    def start(r, c):
        for k in range(2):
            _row_copy(ye_hbm, slot_ref[0, k, r], rows_ref.at[k], r, sem).start()
        return c

    def wait(r, c):
        for k in range(2):
            _row_copy(ye_hbm, slot_ref[0, k, r], rows_ref.at[k], r, sem).wait()
        return c

    lax.fori_loop(0, tc, start, 0, unroll=8)
    lax.fori_loop(0, tc, wait, 0, unroll=8)
    route = route_ref[...]
    y = x1_ref[...]
    for k in range(2):
        gate = route[:, 2 + k:3 + k]
        y = y + gate * rows_ref[k]
    y_ref[...] = y


def _combine(slots, x1, route, ye, tc):
    n = x1.shape[0]
    return pl.pallas_call(
        _combine_kernel,
        grid=(n // tc,),
        in_specs=[pl.BlockSpec((1, 2, tc), lambda i: (i, 0, 0), memory_space=pltpu.SMEM),
                  pl.BlockSpec((tc, D_MODEL), lambda i: (i, 0)),
                  pl.BlockSpec((tc, ROUTE_LANES), lambda i: (i, 0)),
                  pl.BlockSpec(memory_space=pl.ANY)],
        out_specs=pl.BlockSpec((tc, D_MODEL), lambda i: (i, 0)),
        out_shape=jax.ShapeDtypeStruct((n, D_MODEL), F32),
        scratch_shapes=[pltpu.VMEM((2, tc, D_MODEL), F32), pltpu.SemaphoreType.DMA],
        compiler_params=_dma_cparams(),
        name="combine",
    )(slots, x1, route, ye)


def _rope_tables(pos, g, scale):
    half = QK_ROPE // 2
    inv = ROPE_BASE ** (-jnp.arange(half, dtype=F32) / half)
    ang = pos.astype(F32)[:, None] * inv[None, :]
    cos, sin = jnp.cos(ang), jnp.sin(ang)
    n = pos.shape[0]
    g1, g2 = g[QK_NOPE:QK_NOPE + half], g[QK_NOPE + half:QK_HEAD]
    pad = jnp.zeros((n, HEAD_PAD - QK_HEAD), F32)
    a = jnp.concatenate([jnp.broadcast_to(g[:QK_NOPE], (n, QK_NOPE)), g1 * cos, g2 * cos, pad], axis=-1)
    b = jnp.concatenate([jnp.zeros((n, QK_NOPE), F32), -g2 * sin, g1 * sin, pad], axis=-1)
    return a * scale, b * scale


def _pad_heads(w, per_head, keep):
    k = w.shape[0]
    w = w.reshape(k, MLA_HEADS, per_head)[:, :, :keep]
    return jnp.pad(w, ((0, 0), (0, 0), (0, HEAD_PAD - keep))).reshape(k, QK_PAD)


def _swap_rope_cols(w96):
    k = w96.shape[0]
    w = w96.reshape(k, MLA_HEADS, QK_HEAD)
    half = QK_ROPE // 2
    sw = jnp.concatenate([jnp.zeros((k, MLA_HEADS, QK_NOPE), w.dtype), w[:, :, QK_NOPE + half:],
                          w[:, :, QK_NOPE:QK_NOPE + half]], axis=-1)
    return sw.reshape(k, MLA_HEADS * QK_HEAD)


def _block_diag(w, rows_per_group, cols_per_group):
    gb = SSM_GROUPS // SSM_BLOCKS
    w = w.reshape(SSM_BLOCKS, gb, rows_per_group, cols_per_group)
    eye = jnp.eye(gb, dtype=w.dtype)
    out = jnp.einsum('bgrc,gh->bgrhc', w, eye)
    return out.reshape(SSM_BLOCKS, gb * rows_per_group, gb * cols_per_group)


def _layer(x, pos, h0_re, h0_im, mk, mv, past, cnt0, p, tl):
    b, l, _ = x.shape
    scale = math.log2(math.e) / math.sqrt(QK_HEAD)
    aq, bq = _rope_tables(pos, p["g_qn"], scale)
    ak, bk = _rope_tables(pos, p["g_kn"], 1.0)
    inproj_wts = (p["g_attn"], p["w_u"], p["w_q"], p["w_kv"], p["w_pe"], p["w_qm"], p["w_g"], p["g_qlat"],
                  p["wq_pad"], p["wq_swap"], p["g_kvlat"], p["wk_pad"], p["wv"], p["vone"], p["g_mqn"])
    u, q, k, v, c_kv, k_pe, om, gates = _inproj(x, mk.astype(BF16), mv.astype(BF16), (aq, bq, ak, bk),
                                                inproj_wts, tl)
    a_re = jnp.broadcast_to(p["ab_re"], (b, SSM_STATES))
    a_im = jnp.broadcast_to(p["ab_im"], (b, SSM_STATES))
    bra, h_re, h_im = _s5(u, h0_re, h0_im, a_re, a_im, p["bmat"], p["cmat"], p["ssm_d"], p["w_glu"], S5_STEPS)
    if past is None:
        o = _attn_causal(q, k, v, 256)
    else:
        past_ckv, past_kpe = past
        lp = past_ckv.shape[1]
        akp, bkp = _rope_tables(jnp.arange(lp), p["g_kn"], 1.0)
        half = QK_ROPE // 2
        lane_pad = lambda a: jnp.pad(a, ((0, 0), (0, 0), (QK_NOPE, HEAD_PAD - QK_HEAD)))
        kpe_p = lane_pad(past_kpe)
        kpe_s = lane_pad(jnp.concatenate([past_kpe[..., half:], past_kpe[..., :half]], axis=-1))
        k_past, v_past = _kvexp(past_ckv, kpe_p, kpe_s, akp, bkp, p["wk_pad"], p["wv"], p["vone"], 512)
        o = _attn_past(q, k_past, v_past, k, v)
    merge_wts = (p["w_o_mla"], p["w_o_mem"], p["w_out"], p["g_ffn"], p["w_rt"], p["b_rt"])
    x1, h2, route, cnt = _merge(x, bra, o, om, gates, cnt0, merge_wts, tl)
    n = b * l
    return (x1.reshape(n, D_MODEL), h2.reshape(n, D_MODEL), route.reshape(n, ROUTE_LANES), cnt), c_kv, k_pe, h_re, h_im


def _slots(route, offsets, tile):
    expert = route[:, 0:2].astype(jnp.int32)
    rank = route[:, 4:6].astype(jnp.int32)
    onehot = expert[:, :, None] == jnp.arange(N_EXPERTS, dtype=jnp.int32)
    slot = jnp.sum(jnp.where(onehot, offsets, 0), axis=-1) + rank
    return slot.reshape(-1, tile, 2).transpose(0, 2, 1)


def kernel(x_prompt, x_sample, cache_mla_ckv, cache_mla_kpe, cache_ssm_re, cache_ssm_im, cache_mem_k, cache_mem_v, mem_prompt, g_attn, w_in, ssm_a_re, ssm_a_im, ssm_log_dt, ssm_b_re, ssm_b_im, ssm_c_re, ssm_c_im, ssm_d, w_glu, g_qlat, w_uq, g_kvlat, w_ukv, g_qn, g_kn, w_o_mla, g_mem, w_mem_kv, g_mqn, g_mkn, w_o_mem, w_out, g_ffn, w_rg, b_rg, w_re, b_re, w_e1, w_e3, w_e2):
    assert g_attn.shape[0] == 1, "single-layer step"
    bp, lp, _ = x_prompt.shape
    bs, ls, _ = x_sample.shape
    past_len = cache_mla_ckv.shape[2]
    lyr = 0

    o1 = SSM_WIDTH
    o2 = o1 + Q_LORA
    o3 = o2 + KV_LORA
    o4 = o3 + QK_ROPE
    o5 = o4 + MEM_WIDTH
    w_in_bf = w_in[lyr].astype(BF16)
    row = lambda a: a.reshape(1, -1)
    p = {
        "g_attn": row(g_attn[lyr]), "w_u": w_in_bf[:, :o1], "w_q": w_in_bf[:, o1:o2], "w_kv": w_in_bf[:, o2:o3],
        "w_qm": w_in_bf[:, o4:o5], "w_g": w_in_bf[:, o5:],
        "g_qlat": row(g_qlat[lyr]), "g_kvlat": row(g_kvlat[lyr]), "g_mqn": row(g_mqn[lyr]),
        "g_qn": g_qn[lyr], "g_kn": g_kn[lyr], "g_ffn": row(g_ffn[lyr]),
        "ssm_d": ssm_d[lyr], "w_glu": w_glu[lyr].astype(BF16),
        "w_o_mla": w_o_mla[lyr].astype(BF16), "w_o_mem": w_o_mem[lyr].astype(BF16), "w_out": w_out[lyr].astype(BF16),
    }
    wuq = w_uq[lyr]
    p["wq_pad"] = _pad_heads(wuq, QK_HEAD, QK_HEAD).astype(BF16)
    p["wq_swap"] = _pad_heads(_swap_rope_cols(wuq), QK_HEAD, QK_HEAD).astype(BF16)
    wukv = w_ukv[lyr]
    p["wk_pad"] = _pad_heads(wukv, QK_NOPE + V_HEAD, QK_NOPE).astype(BF16)
    p["wv"] = _pad_heads(jnp.roll(wukv.reshape(KV_LORA, MLA_HEADS, QK_NOPE + V_HEAD), -QK_NOPE, axis=-1)
                         .reshape(KV_LORA, -1), QK_NOPE + V_HEAD, V_HEAD).astype(BF16)
    p["vone"] = jnp.asarray((np.arange(QK_PAD) % HEAD_PAD == V_HEAD).astype(np.float32).reshape(1, QK_PAD))
    half = QK_ROPE // 2
    w_pe = w_in_bf[:, o3:o4]
    col_pad = lambda w: jnp.pad(w, ((0, 0), (QK_NOPE, HEAD_PAD - QK_HEAD)))
    p["w_pe"] = jnp.concatenate([col_pad(w_pe), col_pad(jnp.concatenate([w_pe[:, half:], w_pe[:, :half]], axis=-1))],
                                axis=-1)

    ab_re, ab_im, f_re, f_im = _zoh(ssm_a_re[lyr], ssm_a_im[lyr], ssm_log_dt[lyr])
    b_re_, b_im_ = ssm_b_re[lyr], ssm_b_im[lyr]
    bb_re = f_re[..., None] * b_re_ - f_im[..., None] * b_im_
    bb_im = f_re[..., None] * b_im_ + f_im[..., None] * b_re_
    to_cp = lambda w: jnp.swapaxes(w, 1, 2)
    p["bmat"] = jnp.concatenate([_block_diag(to_cp(bb_re), SSM_GROUP_CH, SSM_STATE),
                                 _block_diag(to_cp(bb_im), SSM_GROUP_CH, SSM_STATE)], axis=-1).astype(BF16)
    to_pc = lambda w: jnp.swapaxes(w, 1, 2)
    p["cmat"] = jnp.stack([_block_diag(to_pc(ssm_c_re[lyr]), SSM_STATE, SSM_GROUP_CH),
                           _block_diag(to_pc(ssm_c_im[lyr]), SSM_STATE, SSM_GROUP_CH)]).astype(BF16)
    p["ab_re"] = ab_re.reshape(1, SSM_STATES)
    p["ab_im"] = ab_im.reshape(1, SSM_STATES)

    w_rt = jnp.concatenate([w_rg[lyr], w_re[lyr]], axis=-1)
    w_rt = jnp.pad(w_rt, ((0, 0), (0, ROUTE_LANES - w_rt.shape[1])))
    w_rt_hi = w_rt.astype(BF16)
    p["w_rt"] = jnp.concatenate([w_rt_hi, (w_rt - w_rt_hi.astype(F32)).astype(BF16)], axis=-1)
    b_rt = jnp.concatenate([b_rg[lyr], b_re[lyr].reshape(-1)])
    p["b_rt"] = jnp.pad(b_rt, (0, ROUTE_LANES - b_rt.shape[0])).reshape(1, ROUTE_LANES)

    mk, mv = _memkv(mem_prompt.reshape(-1, D_MODEL), g_mem[lyr], w_mem_kv[lyr].astype(BF16), g_mkn[lyr])
    m_tok = mem_prompt.shape[1]
    mk3, mv3 = mk.reshape(bp, m_tok, MEM_WIDTH), mv.reshape(bp, m_tok, MEM_WIDTH)
    zeros = jnp.zeros((bp, SSM_STATES), F32)
    cnt0 = jnp.zeros((1, ROUTE_LANES), F32)
    (x1_p, h2p_p, route_p, cnt_p), ckv_p, kpe_p, sre_p, sim_p = _layer(
        x_prompt, jnp.arange(lp), zeros, zeros, mk3, mv3, None, cnt0, p, tl=PROMPT_TILE)

    (x1_s, h2p_s, route_s, cnt), ckv_s, kpe_s, sre_s, sim_s = _layer(
        x_sample, past_len + jnp.arange(ls), cache_ssm_re[lyr].reshape(bs, SSM_STATES),
        cache_ssm_im[lyr].reshape(bs, SSM_STATES), cache_mem_k[lyr].reshape(bs, -1, MEM_WIDTH),
        cache_mem_v[lyr].reshape(bs, -1, MEM_WIDTH), (cache_mla_ckv[lyr], cache_mla_kpe[lyr]), cnt_p, p, tl=ls)

    n_p, n_s = x1_p.shape[0], x1_s.shape[0]
    counts = cnt[0, :N_EXPERTS].astype(jnp.int32)
    padded = (counts + MOE_TILE - 1) // MOE_TILE * MOE_TILE
    ends = jnp.cumsum(padded)
    offsets = ends - padded
    max_tiles = 2 * (n_p + n_s) // MOE_TILE + N_EXPERTS
    tile_start = jnp.arange(max_tiles, dtype=jnp.int32) * MOE_TILE
    tile_expert = jnp.sum((ends[None, :] <= tile_start[:, None]).astype(jnp.int32), axis=1)
    tile_expert = jnp.minimum(tile_expert, N_EXPERTS - 1)
    n_tiles = (ends[-1:] // MOE_TILE).astype(jnp.int32)
    slots_p = _slots(route_p, offsets, MOE_DMA_TILE)
    slots_s = _slots(route_s, offsets, MOE_DMA_TILE)
    hs = jnp.zeros((max_tiles * MOE_TILE, D_MODEL), F32)
    hs = _dispatch(slots_p, h2p_p, hs, MOE_DMA_TILE)
    hs = _dispatch(slots_s, h2p_s, hs, MOE_DMA_TILE)
    ye = _experts(tile_expert, n_tiles, hs, w_e1[lyr], w_e3[lyr], w_e2[lyr])
    yp = _combine(slots_p, x1_p, route_p, ye, MOE_DMA_TILE).reshape(bp, lp, D_MODEL)
    ys = _combine(slots_s, x1_s, route_s, ye, MOE_DMA_TILE).reshape(bs, ls, D_MODEL)

    st = lambda a, bsz: a.reshape(1, bsz, SSM_GROUPS, SSM_STATE)
    mem_shape = (1, bp, m_tok, MEM_HEADS, MEM_HEAD)
    return (yp, ys, ckv_p[None], kpe_p[None], st(sre_p, bp), st(sim_p, bp), mk.reshape(mem_shape),
            mv.reshape(mem_shape), ckv_s[None], kpe_s[None], st(sre_s, bs), st(sim_s, bs))
```

```python
import functools
import math

import jax
import jax.numpy as jnp
import numpy as np
from jax import lax
from jax.experimental import pallas as pl
from jax.experimental.pallas import tpu as pltpu

D_MODEL = 1024
CHUNK = 64
RMS_EPS = 1e-6
SSM_GROUPS = 32
SSM_GROUP_CH = 16
SSM_WIDTH = SSM_GROUPS * SSM_GROUP_CH
SSM_STATE = 64
SSM_STATES = SSM_GROUPS * SSM_STATE
SSM_BLOCKS = 2
MLA_HEADS = 8
QK_NOPE = 64
QK_ROPE = 32
QK_HEAD = QK_NOPE + QK_ROPE
V_HEAD = 64
Q_LORA = 384
KV_LORA = 256
ROPE_BASE = 10000.0
MLA_WIDTH = MLA_HEADS * V_HEAD
MEM_HEADS = 4
MEM_HEAD = 128
MEM_WIDTH = MEM_HEADS * MEM_HEAD
N_BRANCHES = 3
N_EXPERT_GROUPS = 4
EXPERTS_PER_GROUP = 8
N_EXPERTS = N_EXPERT_GROUPS * EXPERTS_PER_GROUP
EXPERT_FF = 256

LANES = 128
SUBLANES = 8
HEAD_PAD = LANES
QK_PAD = MLA_HEADS * HEAD_PAD
VMEM_LIMIT = 56 * 1024 * 1024
PROMPT_TILE = 512

BF16 = jnp.bfloat16
F32 = jnp.float32
NEG_INF = -1e30


def _cparams(sem):
    return pltpu.CompilerParams(dimension_semantics=sem, vmem_limit_bytes=VMEM_LIMIT)


def _const_spec(shape):
    nd = len(shape)
    return pl.BlockSpec(shape, lambda *_: (0,) * nd, pipeline_mode=pl.Buffered(1))


def _rms_scale(xf, width):
    return lax.rsqrt(jnp.sum(xf * xf, axis=-1, keepdims=True) * (1.0 / width) + RMS_EPS)


def _dot(a, b):
    return jnp.dot(a, b, preferred_element_type=F32)


def _dot_nt(a, b):
    return lax.dot_general(a, b, (((1,), (1,)), ((), ())), preferred_element_type=F32)


def _zoh_kernel(lr_ref, li_ref, ldt_ref, abr_ref, abi_ref, fr_ref, fi_ref):
    lr = lr_ref[...]
    li = li_ref[...]
    dt = jnp.exp(ldt_ref[...])
    mag = jnp.exp(lr * dt)
    ab_re = mag * jnp.cos(li * dt)
    ab_im = mag * jnp.sin(li * dt)
    den = lr * lr + li * li
    nr = ab_re - 1.0
    ni = ab_im
    abr_ref[...] = ab_re
    abi_ref[...] = ab_im
    fr_ref[...] = (nr * lr + ni * li) / den
    fi_ref[...] = (ni * lr - nr * li) / den


def _zoh(a_re, a_im, log_dt):
    shp = jax.ShapeDtypeStruct((SSM_GROUPS, SSM_STATE), F32)
    return pl.pallas_call(_zoh_kernel, out_shape=(shp, shp, shp, shp), name="zoh")(
        a_re, a_im, log_dt.reshape(SSM_GROUPS, 1))


def _expand_kv(ckv_bf, kpe_p, kpe_s, wk_ref, wv_ref, vone_ref, ak, bk):
    k_nope = _dot(ckv_bf, wk_ref[...])
    v = _dot(ckv_bf, wv_ref[...]) + vone_ref[...]
    ss_pe = jnp.sum(kpe_p * kpe_p, axis=-1, keepdims=True)
    rot = kpe_p * ak + kpe_s * bk
    heads = []
    for h in range(MLA_HEADS):
        kh = k_nope[:, h * HEAD_PAD:(h + 1) * HEAD_PAD]
        ss = jnp.sum(kh * kh, axis=-1, keepdims=True) + ss_pe
        rs = lax.rsqrt(ss * (1.0 / QK_HEAD) + RMS_EPS)
        heads.append((rs * (kh * ak + rot)).astype(BF16))
    return jnp.concatenate(heads, axis=-1), v.astype(BF16)


def _memkv_kernel(mem_ref, g_ref, w_ref, gk_ref, k_ref, v_ref):
    x = mem_ref[...]
    h = (x * _rms_scale(x, D_MODEL) * g_ref[...]).astype(BF16)
    kv = _dot(h, w_ref[...])
    gk = gk_ref[...]
    for hd in range(MEM_HEADS):
        kh = kv[:, hd * MEM_HEAD:(hd + 1) * MEM_HEAD]
        k_ref[:, hd * MEM_HEAD:(hd + 1) * MEM_HEAD] = kh * _rms_scale(kh, MEM_HEAD) * gk
    v_ref[...] = kv[:, MEM_WIDTH:]


def _memkv(mem2d, g_mem, w_mem_kv_bf, g_mkn):
    n = mem2d.shape[0]
    tm = 256
    out = jax.ShapeDtypeStruct((n, MEM_WIDTH), F32)
    return pl.pallas_call(
        _memkv_kernel,
        grid=(n // tm,),
        in_specs=[pl.BlockSpec((tm, D_MODEL), lambda i: (i, 0)),
                  _const_spec((1, D_MODEL)),
                  _const_spec((D_MODEL, 2 * MEM_WIDTH)),
                  _const_spec((1, MEM_HEAD))],
        out_specs=(pl.BlockSpec((tm, MEM_WIDTH), lambda i: (i, 0)),
                   pl.BlockSpec((tm, MEM_WIDTH), lambda i: (i, 0))),
        out_shape=(out, out),
        compiler_params=_cparams(("parallel",)),
        name="memkv",
    )(mem2d, g_mem.reshape(1, D_MODEL), w_mem_kv_bf, g_mkn.reshape(1, MEM_HEAD))


def _inproj_kernel(x_ref, mk_ref, mv_ref, aq_ref, bq_ref, ak_ref, bk_ref,
                   g_attn_ref, w_u_ref, w_q_ref, w_kv_ref, w_pe_ref, w_qm_ref, w_g_ref,
                   g_qlat_ref, wq_ref, wqs_ref, g_kvlat_ref, wk_ref, wv_ref, vone_ref, g_mqn_ref,
                   u_ref, q_ref, k_ref, v_ref, ckv_ref, kpe_ref, om_ref, gate_ref):
    x = x_ref[0]
    h = (x * _rms_scale(x, D_MODEL) * g_attn_ref[...]).astype(BF16)

    u_ref[0] = _dot(h, w_u_ref[...])
    gate_ref[0] = jax.nn.sigmoid(_dot(h, w_g_ref[...])).astype(BF16)

    q_lat = _dot(h, w_q_ref[...])
    qn = (q_lat * _rms_scale(q_lat, Q_LORA) * g_qlat_ref[...]).astype(BF16)
    q_up = _dot(qn, wq_ref[...])
    q_sw = _dot(qn, wqs_ref[...])
    aq = aq_ref[...]
    bq = bq_ref[...]
    for hd in range(MLA_HEADS):
        sl = slice(hd * HEAD_PAD, (hd + 1) * HEAD_PAD)
        qh = q_up[:, sl]
        rs = _rms_scale(qh, QK_HEAD)
        q_ref[0, :, sl] = (rs * (qh * aq + q_sw[:, sl] * bq)).astype(BF16)

    kv_lat = _dot(h, w_kv_ref[...])
    c_kv = kv_lat * _rms_scale(kv_lat, KV_LORA) * g_kvlat_ref[...]
    ckv_ref[0] = c_kv
    kpe_p = _dot(h, w_pe_ref[:, :HEAD_PAD])
    kpe_s = _dot(h, w_pe_ref[:, HEAD_PAD:])
    kpe_ref[0] = kpe_p[:, QK_NOPE:QK_HEAD]
    k_all, v_all = _expand_kv(c_kv.astype(BF16), kpe_p, kpe_s, wk_ref, wv_ref, vone_ref, ak_ref[...], bk_ref[...])
    k_ref[0] = k_all
    v_ref[0] = v_all

    q_mem = _dot(h, w_qm_ref[...])
    gq = g_mqn_ref[...] * (1.0 / math.sqrt(MEM_HEAD))
    for hd in range(MEM_HEADS):
        sl = slice(hd * MEM_HEAD, (hd + 1) * MEM_HEAD)
        qh = q_mem[:, sl]
        qh = (qh * _rms_scale(qh, MEM_HEAD) * gq).astype(BF16)
        s = _dot_nt(qh, mk_ref[0, :, sl])
        p = jnp.exp(s - jnp.max(s, axis=-1, keepdims=True))
        o = _dot(p.astype(BF16), mv_ref[0, :, sl])
        om_ref[0, :, sl] = (o / jnp.sum(p, axis=-1, keepdims=True)).astype(BF16)


def _inproj(x, mk_bf, mv_bf, tabs, wts, tl):
    b, l, _ = x.shape
    m = mk_bf.shape[1]
    aq, bq, ak, bk = tabs
    tok = lambda w: pl.BlockSpec((1, tl, w), lambda bi, li: (bi, li, 0))
    tab = pl.BlockSpec((tl, HEAD_PAD), lambda bi, li: (li, 0))
    memspec = pl.BlockSpec((1, m, MEM_WIDTH), lambda bi, li: (bi, 0, 0))
    in_specs = [tok(D_MODEL), memspec, memspec, tab, tab, tab, tab] + [_const_spec(w.shape) for w in wts]
    out_shape = (
        jax.ShapeDtypeStruct((b, l, SSM_WIDTH), F32),
        jax.ShapeDtypeStruct((b, l, QK_PAD), BF16),
        jax.ShapeDtypeStruct((b, l, QK_PAD), BF16),
        jax.ShapeDtypeStruct((b, l, QK_PAD), BF16),
        jax.ShapeDtypeStruct((b, l, KV_LORA), F32),
        jax.ShapeDtypeStruct((b, l, QK_ROPE), F32),
        jax.ShapeDtypeStruct((b, l, MEM_WIDTH), BF16),
        jax.ShapeDtypeStruct((b, l, N_BRANCHES * D_MODEL), BF16),
    )
    out_specs = (
        tok(SSM_WIDTH), tok(QK_PAD), tok(QK_PAD), tok(QK_PAD), tok(KV_LORA), tok(QK_ROPE), tok(MEM_WIDTH),
        tok(N_BRANCHES * D_MODEL),
    )
    return pl.pallas_call(
        _inproj_kernel,
        grid=(b, l // tl),
        in_specs=in_specs,
        out_specs=out_specs,
        out_shape=out_shape,
        compiler_params=_cparams(("parallel", "parallel")),
        name="inproj",
    )(x, mk_bf, mv_bf, aq, bq, ak, bk, *wts)


def _kvexp_kernel(ckv_ref, kpe_p_ref, kpe_s_ref, ak_ref, bk_ref, wk_ref, wv_ref, vone_ref, k_ref, v_ref):
    k_all, v_all = _expand_kv(ckv_ref[0].astype(BF16), kpe_p_ref[0], kpe_s_ref[0], wk_ref, wv_ref, vone_ref,
                              ak_ref[...], bk_ref[...])
    k_ref[0] = k_all
    v_ref[0] = v_all


def _kvexp(ckv, kpe_p, kpe_s, ak, bk, wk, wv, vone, tl):
    b, l, _ = ckv.shape
    tok = lambda w: pl.BlockSpec((1, tl, w), lambda bi, li: (bi, li, 0))
    tab = pl.BlockSpec((tl, HEAD_PAD), lambda bi, li: (li, 0))
    return pl.pallas_call(
        _kvexp_kernel,
        grid=(b, l // tl),
        in_specs=[tok(KV_LORA), tok(HEAD_PAD), tok(HEAD_PAD), tab, tab]
                 + [_const_spec(w.shape) for w in (wk, wv, vone)],
        out_specs=(tok(QK_PAD), tok(QK_PAD)),
        out_shape=(jax.ShapeDtypeStruct((b, l, QK_PAD), BF16), jax.ShapeDtypeStruct((b, l, QK_PAD), BF16)),
        compiler_params=_cparams(("parallel", "parallel")),
        name="kvexp",
    )(ckv, kpe_p, kpe_s, ak, bk, wk, wv, vone)


SCAN_LANES = 1024
S5_STEPS = 64


def _s5_kernel(u_ref, h0r_ref, h0i_ref, ar_ref, ai_ref, bmat_ref, cmat_ref, d_ref, wglu_ref,
               out_ref, hr_ref, hi_ref, ut_ref, sre_ref, sim_ref, ot_ref):
    i = pl.program_id(0)
    batch, steps, _ = u_ref.shape
    rows = batch * steps

    @pl.when(i == 0)
    def _():
        hr_ref[...] = h0r_ref[...]
        hi_ref[...] = h0i_ref[...]

    u_bm = u_ref[...].reshape(rows, SSM_WIDTH)
    for j in range(SSM_WIDTH // LANES):
        ut_ref[j] = u_bm[:, j * LANES:(j + 1) * LANES]
    u = jnp.concatenate(
        [jnp.concatenate([ut_ref[j, pl.ds(t, batch, stride=steps), :] for j in range(SSM_WIDTH // LANES)], axis=-1)
         for t in range(steps)], axis=0)
    ub = u.astype(BF16)
    blk_ch = SSM_WIDTH // SSM_BLOCKS
    blk_st = SSM_STATES // SSM_BLOCKS
    for blk in range(SSM_BLOCKS):
        bu = _dot(ub[:, blk * blk_ch:(blk + 1) * blk_ch], bmat_ref[blk])
        sre_ref[:, blk * blk_st:(blk + 1) * blk_st] = bu[:, :blk_st]
        sim_ref[:, blk * blk_st:(blk + 1) * blk_st] = bu[:, blk_st:]

    for c in range(SSM_STATES // SCAN_LANES):
        sl = slice(c * SCAN_LANES, (c + 1) * SCAN_LANES)
        a_re = ar_ref[:, sl]
        a_im = ai_ref[:, sl]

        def body(t, carry):
            h_re, h_im = carry
            r0 = pl.multiple_of(t * batch, batch)
            n_re = a_re * h_re - a_im * h_im + sre_ref[pl.ds(r0, batch), sl]
            n_im = a_re * h_im + a_im * h_re + sim_ref[pl.ds(r0, batch), sl]
            sre_ref[pl.ds(r0, batch), sl] = n_re
            sim_ref[pl.ds(r0, batch), sl] = n_im
            return n_re, n_im

        h_re, h_im = lax.fori_loop(0, steps, body, (hr_ref[:, sl], hi_ref[:, sl]), unroll=2)
        hr_ref[:, sl] = h_re
        hi_ref[:, sl] = h_im

    ys = []
    for blk in range(SSM_BLOCKS):
        st = slice(blk * blk_st, (blk + 1) * blk_st)
        ys.append(_dot(sre_ref[:, st].astype(BF16), cmat_ref[0, blk])
                  - _dot(sim_ref[:, st].astype(BF16), cmat_ref[1, blk]))
    y = jnp.concatenate(ys, axis=-1) + d_ref[...] * u
    z = jax.nn.gelu(y).astype(BF16)
    zz = _dot(z, wglu_ref[...])
    out = zz[:, :D_MODEL] * jax.nn.sigmoid(zz[:, D_MODEL:])
    for j in range(D_MODEL // LANES):
        ot_ref[j] = out[:, j * LANES:(j + 1) * LANES]
    for b in range(batch):
        out_ref[b] = jnp.concatenate([ot_ref[j, pl.ds(b, steps, stride=batch), :] for j in range(D_MODEL // LANES)],
                                     axis=-1).astype(BF16)


def _s5(u, h0_re, h0_im, a_re8, a_im8, bmat, cmat, ssm_d, w_glu_bf, steps):
    batch, l, _ = u.shape
    rows = batch * steps
    st = jax.ShapeDtypeStruct((batch, SSM_STATES), F32)
    return pl.pallas_call(
        _s5_kernel,
        grid=(l // steps,),
        in_specs=[pl.BlockSpec((batch, steps, SSM_WIDTH), lambda i: (0, i, 0)),
                  _const_spec((batch, SSM_STATES)), _const_spec((batch, SSM_STATES)),
                  _const_spec((batch, SSM_STATES)), _const_spec((batch, SSM_STATES)),
                  _const_spec(bmat.shape), _const_spec(cmat.shape),
                  _const_spec((1, SSM_WIDTH)), _const_spec(w_glu_bf.shape)],
        out_specs=(pl.BlockSpec((batch, steps, D_MODEL), lambda i: (0, i, 0)),
                   pl.BlockSpec((batch, SSM_STATES), lambda i: (0, 0)),
                   pl.BlockSpec((batch, SSM_STATES), lambda i: (0, 0))),
        out_shape=(jax.ShapeDtypeStruct((batch, l, D_MODEL), BF16), st, st),
        scratch_shapes=[pltpu.VMEM((SSM_WIDTH // LANES, rows, LANES), F32),
                        pltpu.VMEM((rows, SSM_STATES), F32), pltpu.VMEM((rows, SSM_STATES), F32),
                        pltpu.VMEM((D_MODEL // LANES, rows, LANES), F32)],
        compiler_params=_cparams(("arbitrary",)),
        name="s5",
    )(u, h0_re, h0_im, a_re8, a_im8, bmat, cmat, ssm_d.reshape(1, SSM_WIDTH), w_glu_bf)


def _attn_causal_kernel(q_ref, k_ref, v_ref, o_ref, s_ref, m_ref, acc_ref, *, tq):
    i = pl.program_id(1)
    qc = lax.broadcasted_iota(jnp.int32, (tq, tq), 0) // CHUNK
    kc = lax.broadcasted_iota(jnp.int32, (tq, tq), 1) // CHUNK
    diag_mask = kc <= qc
    m_ref[...] = jnp.full(m_ref.shape, NEG_INF, F32)
    acc_ref[...] = jnp.zeros(acc_ref.shape, F32)

    def scores(j, mask):
        r0 = pl.multiple_of(j * tq, tq)
        for hd in range(MLA_HEADS):
            ks = slice(hd * HEAD_PAD, (hd + 1) * HEAD_PAD)
            s = _dot_nt(q_ref[0, :, ks], k_ref[0, pl.ds(r0, tq), ks])
            if mask is not None:
                s = jnp.where(mask, s, NEG_INF)
            s_ref[hd, j] = s
            m_ref[hd] = jnp.maximum(m_ref[hd], jnp.maximum(s[:, :LANES], s[:, LANES:]))

    def pass1(j, c):
        scores(j, None)
        return c

    lax.fori_loop(0, i, pass1, 0)
    scores(i, diag_mask)
    for hd in range(MLA_HEADS):
        m_ref[hd] = jnp.broadcast_to(jnp.max(m_ref[hd], axis=-1, keepdims=True), (tq, LANES))

    def pass2(j, c):
        r0 = pl.multiple_of(j * tq, tq)
        for hd in range(MLA_HEADS):
            ks = slice(hd * HEAD_PAD, (hd + 1) * HEAD_PAD)
            s = s_ref[hd, j]
            mb = m_ref[hd]
            p = jnp.concatenate([jnp.exp2(s[:, :LANES] - mb), jnp.exp2(s[:, LANES:] - mb)], axis=-1).astype(BF16)
            acc_ref[hd] += _dot(p, v_ref[0, pl.ds(r0, tq), ks])
        return c

    lax.fori_loop(0, i + 1, pass2, 0)
    for hd in range(MLA_HEADS):
        acc = acc_ref[hd]
        o_ref[0, :, hd * V_HEAD:(hd + 1) * V_HEAD] = (acc[:, :V_HEAD] / acc[:, V_HEAD:V_HEAD + 1]).astype(BF16)


def _attn_causal(q, k, v, tq):
    b, l, _ = q.shape
    assert tq == 2 * LANES and l % tq == 0 and tq % CHUNK == 0
    full = lambda w: pl.BlockSpec((1, l, w), lambda bi, qi: (bi, 0, 0))
    return pl.pallas_call(
        functools.partial(_attn_causal_kernel, tq=tq),
        grid=(b, l // tq),
        in_specs=[pl.BlockSpec((1, tq, QK_PAD), lambda bi, qi: (bi, qi, 0)), full(QK_PAD), full(QK_PAD)],
        out_specs=pl.BlockSpec((1, tq, MLA_WIDTH), lambda bi, qi: (bi, qi, 0)),
        out_shape=jax.ShapeDtypeStruct((b, l, MLA_WIDTH), BF16),
        scratch_shapes=[pltpu.VMEM((MLA_HEADS, l // tq, tq, tq), F32),
                        pltpu.VMEM((MLA_HEADS, tq, LANES), F32),
                        pltpu.VMEM((MLA_HEADS, tq, HEAD_PAD), F32)],
        compiler_params=_cparams(("parallel", "arbitrary")),
        name="attn_prompt",
    )(q, k, v)


def _attn_past_kernel(q_ref, kp_ref, vp_ref, kn_ref, vn_ref, o_ref):
    for hd in range(MLA_HEADS):
        ks = slice(hd * HEAD_PAD, (hd + 1) * HEAD_PAD)
        qh = q_ref[0, :, ks]
        s_past = _dot_nt(qh, kp_ref[0, :, ks])
        s_new = _dot_nt(qh, kn_ref[0, :, ks])
        m = jnp.maximum(jnp.max(s_past, axis=-1, keepdims=True), jnp.max(s_new, axis=-1, keepdims=True))
        p_past = jnp.exp2(s_past - m)
        p_new = jnp.exp2(s_new - m)
        o = _dot(p_past.astype(BF16), vp_ref[0, :, ks]) + _dot(p_new.astype(BF16), vn_ref[0, :, ks])
        o_ref[0, :, hd * V_HEAD:(hd + 1) * V_HEAD] = (o[:, :V_HEAD] / o[:, V_HEAD:V_HEAD + 1]).astype(BF16)


def _attn_past(q, k_past, v_past, k_new, v_new):
    b, lq, _ = q.shape
    lp = k_past.shape[1]
    assert lp % CHUNK == 0 and lq <= CHUNK
    spec = lambda n, w: pl.BlockSpec((1, n, w), lambda bi: (bi, 0, 0))
    return pl.pallas_call(
        _attn_past_kernel,
        grid=(b,),
        in_specs=[spec(lq, QK_PAD), spec(lp, QK_PAD), spec(lp, QK_PAD), spec(lq, QK_PAD), spec(lq, QK_PAD)],
        out_specs=spec(lq, MLA_WIDTH),
        out_shape=jax.ShapeDtypeStruct((b, lq, MLA_WIDTH), BF16),
        compiler_params=_cparams(("parallel",)),
        name="attn_sample",
    )(q, k_past, v_past, k_new, v_new)


ROUTE_LANES = LANES


def _first_argmax(v, lane, width):
    vmax = jnp.max(v, axis=-1, keepdims=True)
    idx = jnp.min(jnp.where(v == vmax, lane, width), axis=-1, keepdims=True)
    return vmax, idx


def _merge_kernel(x_ref, bra_ref, o_ref, om_ref, gate_ref, cnt0_ref, w_omla_ref, w_omem_ref, w_out_ref, g_ffn_ref,
                  w_rt_ref, b_rt_ref, x1_ref, h2_ref, route_ref, cnt_ref):
    first = (pl.program_id(0) == 0) & (pl.program_id(1) == 0)

    @pl.when(first)
    def _():
        cnt_ref[...] = cnt0_ref[...]

    br_b = _dot(o_ref[0], w_omla_ref[...])
    br_c = _dot(om_ref[0], w_omem_ref[...])
    g = gate_ref[0].astype(F32)
    merged = (g[:, :D_MODEL] * bra_ref[0].astype(F32) + g[:, D_MODEL:2 * D_MODEL] * br_b
              + g[:, 2 * D_MODEL:] * br_c)
    x1 = x_ref[0] + _dot(merged.astype(BF16), w_out_ref[...])
    x1_ref[0] = x1
    h2 = x1 * _rms_scale(x1, D_MODEL) * g_ffn_ref[...]
    h2_ref[0] = h2

    h2_hi = h2.astype(BF16)
    h2_lo = (h2 - h2_hi.astype(F32)).astype(BF16)
    hi_both = _dot(h2_hi, w_rt_ref[...])
    logits = (hi_both[:, :ROUTE_LANES] + hi_both[:, ROUTE_LANES:] + _dot(h2_lo, w_rt_ref[:, :ROUTE_LANES])
              + b_rt_ref[...])
    lane = lax.broadcasted_iota(jnp.int32, logits.shape, 1).astype(F32)
    ninf = jnp.float32(-jnp.inf)
    lg = jnp.where(lane < N_EXPERT_GROUPS, logits, ninf)
    lg_max, grp = _first_argmax(lg, lane, float(ROUTE_LANES))
    p_top = 1.0 / jnp.sum(jnp.exp(lg - lg_max), axis=-1, keepdims=True)
    lo = N_EXPERT_GROUPS + grp * EXPERTS_PER_GROUP
    in_grp = (lane >= lo) & (lane < lo + EXPERTS_PER_GROUP)
    le = jnp.where(in_grp, logits, ninf)
    v1, i1 = _first_argmax(le, lane, float(ROUTE_LANES))
    v2, i2 = _first_argmax(jnp.where(lane == i1, ninf, le), lane, float(ROUTE_LANES))
    e2 = jnp.exp(v2 - v1)
    w1 = p_top / (1.0 + e2)
    w2 = p_top * e2 / (1.0 + e2)
    elane = lane + N_EXPERT_GROUPS
    oh1 = elane == i1
    oh2 = elane == i2
    onehot = jnp.where(oh1 | oh2, 1.0, 0.0)
    tl = onehot.shape[0]
    tri = jnp.where(lax.broadcasted_iota(jnp.int32, (tl, tl), 0) > lax.broadcasted_iota(jnp.int32, (tl, tl), 1),
                    1.0, 0.0).astype(BF16)
    before = _dot(tri, onehot.astype(BF16)) + cnt_ref[...]
    r1 = jnp.sum(jnp.where(oh1, before, 0.0), axis=-1, keepdims=True)
    r2 = jnp.sum(jnp.where(oh2, before, 0.0), axis=-1, keepdims=True)
    cnt_ref[...] += jnp.sum(onehot, axis=0, keepdims=True)
    cols = (i1 - N_EXPERT_GROUPS, i2 - N_EXPERT_GROUPS, w1, w2, r1, r2)
    route = jnp.zeros_like(logits)
    for k, col in enumerate(cols):
        route = jnp.where(lane == k, col, route)
    route_ref[0] = route


def _merge(x, bra, o, om, gates, cnt0, wts, tl):
    b, l, _ = x.shape
    tok = lambda w: pl.BlockSpec((1, tl, w), lambda bi, li: (bi, li, 0))
    in_specs = [tok(D_MODEL), tok(D_MODEL), tok(MLA_WIDTH), tok(MEM_WIDTH), tok(N_BRANCHES * D_MODEL),
                _const_spec((1, ROUTE_LANES))] + [_const_spec(w.shape) for w in wts]
    return pl.pallas_call(
        _merge_kernel,
        grid=(b, l // tl),
        in_specs=in_specs,
        out_specs=(tok(D_MODEL), tok(D_MODEL), tok(ROUTE_LANES),
                   pl.BlockSpec((1, ROUTE_LANES), lambda bi, li: (0, 0))),
        out_shape=(jax.ShapeDtypeStruct((b, l, D_MODEL), F32), jax.ShapeDtypeStruct((b, l, D_MODEL), F32),
                   jax.ShapeDtypeStruct((b, l, ROUTE_LANES), F32), jax.ShapeDtypeStruct((1, ROUTE_LANES), F32)),
        compiler_params=_cparams(("arbitrary", "arbitrary")),
        name="merge",
    )(x, bra, o, om, gates, cnt0, *wts)


MOE_TILE = 256
MOE_DMA_TILE = 256


def _row_copy(src_ref, src_row, dst_ref, dst_row, sem):
    return pltpu.make_async_copy(src_ref.at[pl.ds(src_row, 1)], dst_ref.at[pl.ds(dst_row, 1)], sem)


def _dispatch_kernel(slot_ref, h2_ref, hs_in_hbm, hs_hbm, sem):
    del hs_in_hbm
    td = h2_ref.shape[0]
    for r in range(td):
        for k in range(2):
            _row_copy(h2_ref, r, hs_hbm, slot_ref[0, 0, 2 * r + k], sem).start()

    def wait(r, c):
        for k in range(2):
            _row_copy(h2_ref, r, hs_hbm, slot_ref[0, 0, 2 * r + k], sem).wait()
        return c

    lax.fori_loop(0, td, wait, 0, unroll=8)


def _dma_cparams():
    return pltpu.CompilerParams(dimension_semantics=("arbitrary",), vmem_limit_bytes=VMEM_LIMIT,
                                disable_bounds_checks=True)


def _dispatch(slots, h2, hs, td):
    n = h2.shape[0]
    return pl.pallas_call(
        _dispatch_kernel,
        grid=(n // td,),
        in_specs=[pl.BlockSpec((1, 1, 2 * td), lambda i: (i, 0, 0), memory_space=pltpu.SMEM),
                  pl.BlockSpec((td, D_MODEL), lambda i: (i, 0)), pl.BlockSpec(memory_space=pl.ANY)],
        out_specs=pl.BlockSpec(memory_space=pl.ANY),
        out_shape=jax.ShapeDtypeStruct(hs.shape, hs.dtype),
        scratch_shapes=[pltpu.SemaphoreType.DMA],
        input_output_aliases={2: 0},
        compiler_params=_dma_cparams(),
        name="dispatch",
    )(slots, h2, hs)


def _experts_kernel(tile_expert_ref, n_tiles_ref, hs_ref, w1_ref, w3_ref, w2_ref, y_ref, w13_bf_ref, w2_bf_ref):
    t = pl.program_id(0)
    used = t < n_tiles_ref[0]
    new_expert = (t == 0) | (tile_expert_ref[t] != tile_expert_ref[jnp.maximum(t - 1, 0)])

    @pl.when(used & new_expert)
    def _():
        w13_bf_ref[:, :EXPERT_FF] = w1_ref[0].astype(BF16)
        w13_bf_ref[:, EXPERT_FF:] = w3_ref[0].astype(BF16)
        w2_bf_ref[...] = w2_ref[0].astype(BF16)

    @pl.when(used)
    def _():
        a = _dot(hs_ref[...].astype(BF16), w13_bf_ref[...])
        hid = jax.nn.silu(a[:, :EXPERT_FF]) * a[:, EXPERT_FF:]
        y_ref[...] = _dot(hid.astype(BF16), w2_bf_ref[...])

    @pl.when(jnp.logical_not(used))
    def _():
        y_ref[...] = jnp.zeros(y_ref.shape, y_ref.dtype)


def _experts(tile_expert, n_tiles, hs, w_e1, w_e3, w_e2):
    s = hs.shape[0]
    w_in_spec = pl.BlockSpec((1, D_MODEL, EXPERT_FF), lambda t, te, nt: (te[t], 0, 0))
    grid_spec = pltpu.PrefetchScalarGridSpec(
        num_scalar_prefetch=2,
        grid=(s // MOE_TILE,),
        in_specs=[pl.BlockSpec((MOE_TILE, D_MODEL), lambda t, te, nt: (t, 0)), w_in_spec, w_in_spec,
                  pl.BlockSpec((1, EXPERT_FF, D_MODEL), lambda t, te, nt: (te[t], 0, 0))],
        out_specs=pl.BlockSpec((MOE_TILE, D_MODEL), lambda t, te, nt: (t, 0)),
        scratch_shapes=[pltpu.VMEM((D_MODEL, 2 * EXPERT_FF), BF16), pltpu.VMEM((EXPERT_FF, D_MODEL), BF16)],
    )
    return pl.pallas_call(
        _experts_kernel,
        grid_spec=grid_spec,
        out_shape=jax.ShapeDtypeStruct((s, D_MODEL), F32),
        compiler_params=_cparams(("arbitrary",)),
        name="experts",
    )(tile_expert, n_tiles, hs, w_e1, w_e3, w_e2)


def _combine_kernel(slot_ref, x1_ref, route_ref, ye_hbm, y_ref, rows_ref, sem):
    tc = x1_ref.shape[0]
    for r in range(tc):
        for k in range(2):
            _row_copy(ye_hbm, slot_ref[0, 0, 2 * r + k], rows_ref.at[k], r, sem).start()

    def wait(r, c):
        for k in range(2):
            _row_copy(ye_hbm, slot_ref[0, 0, 2 * r + k], rows_ref.at[k], r, sem).wait()
        return c

    lax.fori_loop(0, tc, wait, 0, unroll=8)
    route = route_ref[...]
    y = x1_ref[...]
    for k in range(2):
        gate = route[:, 2 + k:3 + k]
        y = y + gate * rows_ref[k]
    y_ref[...] = y


def _combine(slots, x1, route, ye, tc):
    n = x1.shape[0]
    return pl.pallas_call(
        _combine_kernel,
        grid=(n // tc,),
        in_specs=[pl.BlockSpec((1, 1, 2 * tc), lambda i: (i, 0, 0), memory_space=pltpu.SMEM),
                  pl.BlockSpec((tc, D_MODEL), lambda i: (i, 0)),
                  pl.BlockSpec((tc, ROUTE_LANES), lambda i: (i, 0)),
                  pl.BlockSpec(memory_space=pl.ANY)],
        out_specs=pl.BlockSpec((tc, D_MODEL), lambda i: (i, 0)),
        out_shape=jax.ShapeDtypeStruct((n, D_MODEL), F32),
        scratch_shapes=[pltpu.VMEM((2, tc, D_MODEL), F32), pltpu.SemaphoreType.DMA],
        compiler_params=_dma_cparams(),
        name="combine",
    )(slots, x1, route, ye)


def _rope_tables(pos, g, scale):
    half = QK_ROPE // 2
    inv = ROPE_BASE ** (-jnp.arange(half, dtype=F32) / half)
    ang = pos.astype(F32)[:, None] * inv[None, :]
    cos, sin = jnp.cos(ang), jnp.sin(ang)
    n = pos.shape[0]
    g1, g2 = g[QK_NOPE:QK_NOPE + half], g[QK_NOPE + half:QK_HEAD]
    pad = jnp.zeros((n, HEAD_PAD - QK_HEAD), F32)
    a = jnp.concatenate([jnp.broadcast_to(g[:QK_NOPE], (n, QK_NOPE)), g1 * cos, g2 * cos, pad], axis=-1)
    b = jnp.concatenate([jnp.zeros((n, QK_NOPE), F32), -g2 * sin, g1 * sin, pad], axis=-1)
    return a * scale, b * scale


def _pad_heads(w, per_head, keep):
    k = w.shape[0]
    w = w.reshape(k, MLA_HEADS, per_head)[:, :, :keep]
    return jnp.pad(w, ((0, 0), (0, 0), (0, HEAD_PAD - keep))).reshape(k, QK_PAD)


def _swap_rope_cols(w96):
    k = w96.shape[0]
    w = w96.reshape(k, MLA_HEADS, QK_HEAD)
    half = QK_ROPE // 2
    sw = jnp.concatenate([jnp.zeros((k, MLA_HEADS, QK_NOPE), w.dtype), w[:, :, QK_NOPE + half:],
                          w[:, :, QK_NOPE:QK_NOPE + half]], axis=-1)
    return sw.reshape(k, MLA_HEADS * QK_HEAD)


def _block_diag(w, rows_per_group, cols_per_group):
    gb = SSM_GROUPS // SSM_BLOCKS
    w = w.reshape(SSM_BLOCKS, gb, rows_per_group, cols_per_group)
    eye = jnp.eye(gb, dtype=w.dtype)
    out = jnp.einsum('bgrc,gh->bgrhc', w, eye)
    return out.reshape(SSM_BLOCKS, gb * rows_per_group, gb * cols_per_group)


def _layer(x, pos, h0_re, h0_im, mk, mv, past, cnt0, p, tl):
    b, l, _ = x.shape
    scale = math.log2(math.e) / math.sqrt(QK_HEAD)
    aq, bq = _rope_tables(pos, p["g_qn"], scale)
    ak, bk = _rope_tables(pos, p["g_kn"], 1.0)
    inproj_wts = (p["g_attn"], p["w_u"], p["w_q"], p["w_kv"], p["w_pe"], p["w_qm"], p["w_g"], p["g_qlat"],
                  p["wq_pad"], p["wq_swap"], p["g_kvlat"], p["wk_pad"], p["wv"], p["vone"], p["g_mqn"])
    u, q, k, v, c_kv, k_pe, om, gates = _inproj(x, mk.astype(BF16), mv.astype(BF16), (aq, bq, ak, bk),
                                                inproj_wts, tl)
    a_re = jnp.broadcast_to(p["ab_re"], (b, SSM_STATES))
    a_im = jnp.broadcast_to(p["ab_im"], (b, SSM_STATES))
    bra, h_re, h_im = _s5(u, h0_re, h0_im, a_re, a_im, p["bmat"], p["cmat"], p["ssm_d"], p["w_glu"], S5_STEPS)
    if past is None:
        o = _attn_causal(q, k, v, 256)
    else:
        past_ckv, past_kpe = past
        lp = past_ckv.shape[1]
        akp, bkp = _rope_tables(jnp.arange(lp), p["g_kn"], 1.0)
        half = QK_ROPE // 2
        lane_pad = lambda a: jnp.pad(a, ((0, 0), (0, 0), (QK_NOPE, HEAD_PAD - QK_HEAD)))
        kpe_p = lane_pad(past_kpe)
        kpe_s = lane_pad(jnp.concatenate([past_kpe[..., half:], past_kpe[..., :half]], axis=-1))
        k_past, v_past = _kvexp(past_ckv, kpe_p, kpe_s, akp, bkp, p["wk_pad"], p["wv"], p["vone"], 512)
        o = _attn_past(q, k_past, v_past, k, v)
    merge_wts = (p["w_o_mla"], p["w_o_mem"], p["w_out"], p["g_ffn"], p["w_rt"], p["b_rt"])
    x1, h2, route, cnt = _merge(x, bra, o, om, gates, cnt0, merge_wts, tl)
    n = b * l
    return (x1.reshape(n, D_MODEL), h2.reshape(n, D_MODEL), route.reshape(n, ROUTE_LANES), cnt), c_kv, k_pe, h_re, h_im


def _slots(route, offsets, tile):
    expert = route[:, 0:2].astype(jnp.int32)
    rank = route[:, 4:6].astype(jnp.int32)
    onehot = expert[:, :, None] == jnp.arange(N_EXPERTS, dtype=jnp.int32)
    slot = jnp.sum(jnp.where(onehot, offsets, 0), axis=-1) + rank
    return slot.reshape(-1, 1, 2 * tile)


def kernel(x_prompt, x_sample, cache_mla_ckv, cache_mla_kpe, cache_ssm_re, cache_ssm_im, cache_mem_k, cache_mem_v, mem_prompt, g_attn, w_in, ssm_a_re, ssm_a_im, ssm_log_dt, ssm_b_re, ssm_b_im, ssm_c_re, ssm_c_im, ssm_d, w_glu, g_qlat, w_uq, g_kvlat, w_ukv, g_qn, g_kn, w_o_mla, g_mem, w_mem_kv, g_mqn, g_mkn, w_o_mem, w_out, g_ffn, w_rg, b_rg, w_re, b_re, w_e1, w_e3, w_e2):
    assert g_attn.shape[0] == 1, "single-layer step"
    bp, lp, _ = x_prompt.shape
    bs, ls, _ = x_sample.shape
    past_len = cache_mla_ckv.shape[2]
    lyr = 0

    o1 = SSM_WIDTH
    o2 = o1 + Q_LORA
    o3 = o2 + KV_LORA
    o4 = o3 + QK_ROPE
    o5 = o4 + MEM_WIDTH
    w_in_bf = w_in[lyr].astype(BF16)
    row = lambda a: a.reshape(1, -1)
    p = {
        "g_attn": row(g_attn[lyr]), "w_u": w_in_bf[:, :o1], "w_q": w_in_bf[:, o1:o2], "w_kv": w_in_bf[:, o2:o3],
        "w_qm": w_in_bf[:, o4:o5], "w_g": w_in_bf[:, o5:],
        "g_qlat": row(g_qlat[lyr]), "g_kvlat": row(g_kvlat[lyr]), "g_mqn": row(g_mqn[lyr]),
        "g_qn": g_qn[lyr], "g_kn": g_kn[lyr], "g_ffn": row(g_ffn[lyr]),
        "ssm_d": ssm_d[lyr], "w_glu": w_glu[lyr].astype(BF16),
        "w_o_mla": w_o_mla[lyr].astype(BF16), "w_o_mem": w_o_mem[lyr].astype(BF16), "w_out": w_out[lyr].astype(BF16),
    }
    wuq = w_uq[lyr]
    p["wq_pad"] = _pad_heads(wuq, QK_HEAD, QK_HEAD).astype(BF16)
    p["wq_swap"] = _pad_heads(_swap_rope_cols(wuq), QK_HEAD, QK_HEAD).astype(BF16)
    wukv = w_ukv[lyr]
    p["wk_pad"] = _pad_heads(wukv, QK_NOPE + V_HEAD, QK_NOPE).astype(BF16)
    p["wv"] = _pad_heads(jnp.roll(wukv.reshape(KV_LORA, MLA_HEADS, QK_NOPE + V_HEAD), -QK_NOPE, axis=-1)
                         .reshape(KV_LORA, -1), QK_NOPE + V_HEAD, V_HEAD).astype(BF16)
    p["vone"] = jnp.asarray((np.arange(QK_PAD) % HEAD_PAD == V_HEAD).astype(np.float32).reshape(1, QK_PAD))
    half = QK_ROPE // 2
    w_pe = w_in_bf[:, o3:o4]
    col_pad = lambda w: jnp.pad(w, ((0, 0), (QK_NOPE, HEAD_PAD - QK_HEAD)))
    p["w_pe"] = jnp.concatenate([col_pad(w_pe), col_pad(jnp.concatenate([w_pe[:, half:], w_pe[:, :half]], axis=-1))],
                                axis=-1)

    ab_re, ab_im, f_re, f_im = _zoh(ssm_a_re[lyr], ssm_a_im[lyr], ssm_log_dt[lyr])
    b_re_, b_im_ = ssm_b_re[lyr], ssm_b_im[lyr]
    bb_re = f_re[..., None] * b_re_ - f_im[..., None] * b_im_
    bb_im = f_re[..., None] * b_im_ + f_im[..., None] * b_re_
    to_cp = lambda w: jnp.swapaxes(w, 1, 2)
    p["bmat"] = jnp.concatenate([_block_diag(to_cp(bb_re), SSM_GROUP_CH, SSM_STATE),
                                 _block_diag(to_cp(bb_im), SSM_GROUP_CH, SSM_STATE)], axis=-1).astype(BF16)
    to_pc = lambda w: jnp.swapaxes(w, 1, 2)
    p["cmat"] = jnp.stack([_block_diag(to_pc(ssm_c_re[lyr]), SSM_STATE, SSM_GROUP_CH),
                           _block_diag(to_pc(ssm_c_im[lyr]), SSM_STATE, SSM_GROUP_CH)]).astype(BF16)
    p["ab_re"] = ab_re.reshape(1, SSM_STATES)
    p["ab_im"] = ab_im.reshape(1, SSM_STATES)

    w_rt = jnp.concatenate([w_rg[lyr], w_re[lyr]], axis=-1)
    w_rt = jnp.pad(w_rt, ((0, 0), (0, ROUTE_LANES - w_rt.shape[1])))
    w_rt_hi = w_rt.astype(BF16)
    p["w_rt"] = jnp.concatenate([w_rt_hi, (w_rt - w_rt_hi.astype(F32)).astype(BF16)], axis=-1)
    b_rt = jnp.concatenate([b_rg[lyr], b_re[lyr].reshape(-1)])
    p["b_rt"] = jnp.pad(b_rt, (0, ROUTE_LANES - b_rt.shape[0])).reshape(1, ROUTE_LANES)

    mk, mv = _memkv(mem_prompt.reshape(-1, D_MODEL), g_mem[lyr], w_mem_kv[lyr].astype(BF16), g_mkn[lyr])
    m_tok = mem_prompt.shape[1]
    mk3, mv3 = mk.reshape(bp, m_tok, MEM_WIDTH), mv.reshape(bp, m_tok, MEM_WIDTH)
    zeros = jnp.zeros((bp, SSM_STATES), F32)
    cnt0 = jnp.zeros((1, ROUTE_LANES), F32)
    (x1_p, h2p_p, route_p, cnt_p), ckv_p, kpe_p, sre_p, sim_p = _layer(
        x_prompt, jnp.arange(lp), zeros, zeros, mk3, mv3, None, cnt0, p, tl=PROMPT_TILE)

    (x1_s, h2p_s, route_s, cnt), ckv_s, kpe_s, sre_s, sim_s = _layer(
        x_sample, past_len + jnp.arange(ls), cache_ssm_re[lyr].reshape(bs, SSM_STATES),
        cache_ssm_im[lyr].reshape(bs, SSM_STATES), cache_mem_k[lyr].reshape(bs, -1, MEM_WIDTH),
        cache_mem_v[lyr].reshape(bs, -1, MEM_WIDTH), (cache_mla_ckv[lyr], cache_mla_kpe[lyr]), cnt_p, p, tl=ls)

    n_p, n_s = x1_p.shape[0], x1_s.shape[0]
    counts = cnt[0, :N_EXPERTS].astype(jnp.int32)
    padded = (counts + MOE_TILE - 1) // MOE_TILE * MOE_TILE
    ends = jnp.cumsum(padded)
    offsets = ends - padded
    max_tiles = 2 * (n_p + n_s) // MOE_TILE + N_EXPERTS
    tile_start = jnp.arange(max_tiles, dtype=jnp.int32) * MOE_TILE
    tile_expert = jnp.sum((ends[None, :] <= tile_start[:, None]).astype(jnp.int32), axis=1)
    tile_expert = jnp.minimum(tile_expert, N_EXPERTS - 1)
    n_tiles = (ends[-1:] // MOE_TILE).astype(jnp.int32)
    slots_p = _slots(route_p, offsets, MOE_DMA_TILE)
    slots_s = _slots(route_s, offsets, MOE_DMA_TILE)
    hs = jnp.zeros((max_tiles * MOE_TILE, D_MODEL), F32)
    hs = _dispatch(slots_p, h2p_p, hs, MOE_DMA_TILE)
    hs = _dispatch(slots_s, h2p_s, hs, MOE_DMA_TILE)
    ye = _experts(tile_expert, n_tiles, hs, w_e1[lyr], w_e3[lyr], w_e2[lyr])
    yp = _combine(slots_p, x1_p, route_p, ye, MOE_DMA_TILE).reshape(bp, lp, D_MODEL)
    ys = _combine(slots_s, x1_s, route_s, ye, MOE_DMA_TILE).reshape(bs, ls, D_MODEL)

    st = lambda a, bsz: a.reshape(1, bsz, SSM_GROUPS, SSM_STATE)
    mem_shape = (1, bp, m_tok, MEM_HEADS, MEM_HEAD)
    return (yp, ys, ckv_p[None], kpe_p[None], st(sre_p, bp), st(sim_p, bp), mk.reshape(mem_shape),
            mv.reshape(mem_shape), ckv_s[None], kpe_s[None], st(sre_s, bs), st(sim_s, bs))
```

```python
import functools
import math

import jax
import jax.numpy as jnp
import numpy as np
from jax import lax
from jax.experimental import pallas as pl
from jax.experimental.pallas import tpu as pltpu

D_MODEL = 1024
CHUNK = 64
RMS_EPS = 1e-6
SSM_GROUPS = 32
SSM_GROUP_CH = 16
SSM_WIDTH = SSM_GROUPS * SSM_GROUP_CH
SSM_STATE = 64
SSM_STATES = SSM_GROUPS * SSM_STATE
SSM_BLOCKS = 2
MLA_HEADS = 8
QK_NOPE = 64
QK_ROPE = 32
QK_HEAD = QK_NOPE + QK_ROPE
V_HEAD = 64
Q_LORA = 384
KV_LORA = 256
ROPE_BASE = 10000.0
MLA_WIDTH = MLA_HEADS * V_HEAD
MEM_HEADS = 4
MEM_HEAD = 128
MEM_WIDTH = MEM_HEADS * MEM_HEAD
N_BRANCHES = 3
N_EXPERT_GROUPS = 4
EXPERTS_PER_GROUP = 8
N_EXPERTS = N_EXPERT_GROUPS * EXPERTS_PER_GROUP
EXPERT_FF = 256

LANES = 128
SUBLANES = 8
HEAD_PAD = LANES
QK_PAD = MLA_HEADS * HEAD_PAD
VMEM_LIMIT = 56 * 1024 * 1024
PROMPT_TILE = 512

BF16 = jnp.bfloat16
F32 = jnp.float32
NEG_INF = -1e30


def _cparams(sem):
    return pltpu.CompilerParams(dimension_semantics=sem, vmem_limit_bytes=VMEM_LIMIT)


def _const_spec(shape):
    nd = len(shape)
    return pl.BlockSpec(shape, lambda *_: (0,) * nd, pipeline_mode=pl.Buffered(1))


def _rms_scale(xf, width):
    return lax.rsqrt(jnp.sum(xf * xf, axis=-1, keepdims=True) * (1.0 / width) + RMS_EPS)


def _dot(a, b):
    return jnp.dot(a, b, preferred_element_type=F32)


def _dot_nt(a, b):
    return lax.dot_general(a, b, (((1,), (1,)), ((), ())), preferred_element_type=F32)


def _zoh_kernel(lr_ref, li_ref, ldt_ref, abr_ref, abi_ref, fr_ref, fi_ref):
    lr = lr_ref[...]
    li = li_ref[...]
    dt = jnp.exp(ldt_ref[...])
    mag = jnp.exp(lr * dt)
    ab_re = mag * jnp.cos(li * dt)
    ab_im = mag * jnp.sin(li * dt)
    den = lr * lr + li * li
    nr = ab_re - 1.0
    ni = ab_im
    abr_ref[...] = ab_re
    abi_ref[...] = ab_im
    fr_ref[...] = (nr * lr + ni * li) / den
    fi_ref[...] = (ni * lr - nr * li) / den


def _zoh(a_re, a_im, log_dt):
    shp = jax.ShapeDtypeStruct((SSM_GROUPS, SSM_STATE), F32)
    return pl.pallas_call(_zoh_kernel, out_shape=(shp, shp, shp, shp), name="zoh")(
        a_re, a_im, log_dt.reshape(SSM_GROUPS, 1))


def _expand_kv(ckv_bf, kpe_p, kpe_s, wk_ref, wv_ref, vone_ref, ak, bk):
    k_nope = _dot(ckv_bf, wk_ref[...])
    v = _dot(ckv_bf, wv_ref[...]) + vone_ref[...]
    ss_pe = jnp.sum(kpe_p * kpe_p, axis=-1, keepdims=True)
    rot = kpe_p * ak + kpe_s * bk
    heads = []
    for h in range(MLA_HEADS):
        kh = k_nope[:, h * HEAD_PAD:(h + 1) * HEAD_PAD]
        ss = jnp.sum(kh * kh, axis=-1, keepdims=True) + ss_pe
        rs = lax.rsqrt(ss * (1.0 / QK_HEAD) + RMS_EPS)
        heads.append((rs * (kh * ak + rot)).astype(BF16))
    return jnp.concatenate(heads, axis=-1), v.astype(BF16)


def _memkv_kernel(mem_ref, g_ref, w_ref, gk_ref, k_ref, v_ref):
    x = mem_ref[...]
    h = (x * _rms_scale(x, D_MODEL) * g_ref[...]).astype(BF16)
    kv = _dot(h, w_ref[...])
    gk = gk_ref[...]
    for hd in range(MEM_HEADS):
        kh = kv[:, hd * MEM_HEAD:(hd + 1) * MEM_HEAD]
        k_ref[:, hd * MEM_HEAD:(hd + 1) * MEM_HEAD] = kh * _rms_scale(kh, MEM_HEAD) * gk
    v_ref[...] = kv[:, MEM_WIDTH:]


def _memkv(mem2d, g_mem, w_mem_kv_bf, g_mkn):
    n = mem2d.shape[0]
    tm = 256
    out = jax.ShapeDtypeStruct((n, MEM_WIDTH), F32)
    return pl.pallas_call(
        _memkv_kernel,
        grid=(n // tm,),
        in_specs=[pl.BlockSpec((tm, D_MODEL), lambda i: (i, 0)),
                  _const_spec((1, D_MODEL)),
                  _const_spec((D_MODEL, 2 * MEM_WIDTH)),
                  _const_spec((1, MEM_HEAD))],
        out_specs=(pl.BlockSpec((tm, MEM_WIDTH), lambda i: (i, 0)),
                   pl.BlockSpec((tm, MEM_WIDTH), lambda i: (i, 0))),
        out_shape=(out, out),
        compiler_params=_cparams(("parallel",)),
        name="memkv",
    )(mem2d, g_mem.reshape(1, D_MODEL), w_mem_kv_bf, g_mkn.reshape(1, MEM_HEAD))


def _inproj_kernel(x_ref, mk_ref, mv_ref, aq_ref, bq_ref, ak_ref, bk_ref,
                   g_attn_ref, w_u_ref, w_q_ref, w_kv_ref, w_pe_ref, w_qm_ref, w_g_ref,
                   g_qlat_ref, wq_ref, wqs_ref, g_kvlat_ref, wk_ref, wv_ref, vone_ref, g_mqn_ref,
                   u_ref, q_ref, k_ref, v_ref, ckv_ref, kpe_ref, om_ref, gate_ref):
    x = x_ref[0]
    h = (x * _rms_scale(x, D_MODEL) * g_attn_ref[...]).astype(BF16)

    u_ref[0] = _dot(h, w_u_ref[...])
    gate_ref[0] = jax.nn.sigmoid(_dot(h, w_g_ref[...])).astype(BF16)

    q_lat = _dot(h, w_q_ref[...])
    qn = (q_lat * _rms_scale(q_lat, Q_LORA) * g_qlat_ref[...]).astype(BF16)
    q_up = _dot(qn, wq_ref[...])
    q_sw = _dot(qn, wqs_ref[...])
    aq = aq_ref[...]
    bq = bq_ref[...]
    for hd in range(MLA_HEADS):
        sl = slice(hd * HEAD_PAD, (hd + 1) * HEAD_PAD)
        qh = q_up[:, sl]
        rs = _rms_scale(qh, QK_HEAD)
        q_ref[0, :, sl] = (rs * (qh * aq + q_sw[:, sl] * bq)).astype(BF16)

    kv_lat = _dot(h, w_kv_ref[...])
    c_kv = kv_lat * _rms_scale(kv_lat, KV_LORA) * g_kvlat_ref[...]
    ckv_ref[0] = c_kv
    kpe_p = _dot(h, w_pe_ref[:, :HEAD_PAD])
    kpe_s = _dot(h, w_pe_ref[:, HEAD_PAD:])
    kpe_ref[0] = kpe_p[:, QK_NOPE:QK_HEAD]
    k_all, v_all = _expand_kv(c_kv.astype(BF16), kpe_p, kpe_s, wk_ref, wv_ref, vone_ref, ak_ref[...], bk_ref[...])
    k_ref[0] = k_all
    v_ref[0] = v_all

    q_mem = _dot(h, w_qm_ref[...])
    gq = g_mqn_ref[...] * (1.0 / math.sqrt(MEM_HEAD))
    for hd in range(MEM_HEADS):
        sl = slice(hd * MEM_HEAD, (hd + 1) * MEM_HEAD)
        qh = q_mem[:, sl]
        qh = (qh * _rms_scale(qh, MEM_HEAD) * gq).astype(BF16)
        s = _dot_nt(qh, mk_ref[0, :, sl])
        p = jnp.exp(s - jnp.max(s, axis=-1, keepdims=True))
        o = _dot(p.astype(BF16), mv_ref[0, :, sl])
        om_ref[0, :, sl] = (o / jnp.sum(p, axis=-1, keepdims=True)).astype(BF16)


def _inproj(x, mk_bf, mv_bf, tabs, wts, tl):
    b, l, _ = x.shape
    m = mk_bf.shape[1]
    aq, bq, ak, bk = tabs
    tok = lambda w: pl.BlockSpec((1, tl, w), lambda bi, li: (bi, li, 0))
    tab = pl.BlockSpec((tl, HEAD_PAD), lambda bi, li: (li, 0))
    memspec = pl.BlockSpec((1, m, MEM_WIDTH), lambda bi, li: (bi, 0, 0))
    in_specs = [tok(D_MODEL), memspec, memspec, tab, tab, tab, tab] + [_const_spec(w.shape) for w in wts]
    out_shape = (
        jax.ShapeDtypeStruct((b, l, SSM_WIDTH), F32),
        jax.ShapeDtypeStruct((b, l, QK_PAD), BF16),
        jax.ShapeDtypeStruct((b, l, QK_PAD), BF16),
        jax.ShapeDtypeStruct((b, l, QK_PAD), BF16),
        jax.ShapeDtypeStruct((b, l, KV_LORA), F32),
        jax.ShapeDtypeStruct((b, l, QK_ROPE), F32),
        jax.ShapeDtypeStruct((b, l, MEM_WIDTH), BF16),
        jax.ShapeDtypeStruct((b, l, N_BRANCHES * D_MODEL), BF16),
    )
    out_specs = (
        tok(SSM_WIDTH), tok(QK_PAD), tok(QK_PAD), tok(QK_PAD), tok(KV_LORA), tok(QK_ROPE), tok(MEM_WIDTH),
        tok(N_BRANCHES * D_MODEL),
    )
    return pl.pallas_call(
        _inproj_kernel,
        grid=(b, l // tl),
        in_specs=in_specs,
        out_specs=out_specs,
        out_shape=out_shape,
        compiler_params=_cparams(("parallel", "parallel")),
        name="inproj",
    )(x, mk_bf, mv_bf, aq, bq, ak, bk, *wts)


def _kvexp_kernel(ckv_ref, kpe_p_ref, kpe_s_ref, ak_ref, bk_ref, wk_ref, wv_ref, vone_ref, k_ref, v_ref):
    k_all, v_all = _expand_kv(ckv_ref[0].astype(BF16), kpe_p_ref[0], kpe_s_ref[0], wk_ref, wv_ref, vone_ref,
                              ak_ref[...], bk_ref[...])
    k_ref[0] = k_all
    v_ref[0] = v_all


def _kvexp(ckv, kpe_p, kpe_s, ak, bk, wk, wv, vone, tl):
    b, l, _ = ckv.shape
    tok = lambda w: pl.BlockSpec((1, tl, w), lambda bi, li: (bi, li, 0))
    tab = pl.BlockSpec((tl, HEAD_PAD), lambda bi, li: (li, 0))
    return pl.pallas_call(
        _kvexp_kernel,
        grid=(b, l // tl),
        in_specs=[tok(KV_LORA), tok(HEAD_PAD), tok(HEAD_PAD), tab, tab]
                 + [_const_spec(w.shape) for w in (wk, wv, vone)],
        out_specs=(tok(QK_PAD), tok(QK_PAD)),
        out_shape=(jax.ShapeDtypeStruct((b, l, QK_PAD), BF16), jax.ShapeDtypeStruct((b, l, QK_PAD), BF16)),
        compiler_params=_cparams(("parallel", "parallel")),
        name="kvexp",
    )(ckv, kpe_p, kpe_s, ak, bk, wk, wv, vone)


SCAN_LANES = 1024
S5_STEPS = 64


def _s5_kernel(u_ref, h0r_ref, h0i_ref, ar_ref, ai_ref, bmat_ref, cmat_ref, d_ref, wglu_ref,
               out_ref, hr_ref, hi_ref, ut_ref, sre_ref, sim_ref, ot_ref):
    i = pl.program_id(0)
    batch, steps, _ = u_ref.shape
    rows = batch * steps

    @pl.when(i == 0)
    def _():
        hr_ref[...] = h0r_ref[...]
        hi_ref[...] = h0i_ref[...]

    u_bm = u_ref[...].reshape(rows, SSM_WIDTH)
    for j in range(SSM_WIDTH // LANES):
        ut_ref[j] = u_bm[:, j * LANES:(j + 1) * LANES]
    u = jnp.concatenate(
        [jnp.concatenate([ut_ref[j, pl.ds(t, batch, stride=steps), :] for j in range(SSM_WIDTH // LANES)], axis=-1)
         for t in range(steps)], axis=0)
    ub = u.astype(BF16)
    blk_ch = SSM_WIDTH // SSM_BLOCKS
    blk_st = SSM_STATES // SSM_BLOCKS
    for blk in range(SSM_BLOCKS):
        bu = _dot(ub[:, blk * blk_ch:(blk + 1) * blk_ch], bmat_ref[blk])
        sre_ref[:, blk * blk_st:(blk + 1) * blk_st] = bu[:, :blk_st]
        sim_ref[:, blk * blk_st:(blk + 1) * blk_st] = bu[:, blk_st:]

    for c in range(SSM_STATES // SCAN_LANES):
        sl = slice(c * SCAN_LANES, (c + 1) * SCAN_LANES)
        a_re = ar_ref[:, sl]
        a_im = ai_ref[:, sl]

        def body(t, carry):
            h_re, h_im = carry
            r0 = pl.multiple_of(t * batch, batch)
            n_re = a_re * h_re - a_im * h_im + sre_ref[pl.ds(r0, batch), sl]
            n_im = a_re * h_im + a_im * h_re + sim_ref[pl.ds(r0, batch), sl]
            sre_ref[pl.ds(r0, batch), sl] = n_re
            sim_ref[pl.ds(r0, batch), sl] = n_im
            return n_re, n_im

        h_re, h_im = lax.fori_loop(0, steps, body, (hr_ref[:, sl], hi_ref[:, sl]), unroll=2)
        hr_ref[:, sl] = h_re
        hi_ref[:, sl] = h_im

    ys = []
    for blk in range(SSM_BLOCKS):
        st = slice(blk * blk_st, (blk + 1) * blk_st)
        ys.append(_dot(sre_ref[:, st].astype(BF16), cmat_ref[0, blk])
                  - _dot(sim_ref[:, st].astype(BF16), cmat_ref[1, blk]))
    y = jnp.concatenate(ys, axis=-1) + d_ref[...] * u
    z = jax.nn.gelu(y).astype(BF16)
    zz = _dot(z, wglu_ref[...])
    out = zz[:, :D_MODEL] * jax.nn.sigmoid(zz[:, D_MODEL:])
    for j in range(D_MODEL // LANES):
        ot_ref[j] = out[:, j * LANES:(j + 1) * LANES]
    for b in range(batch):
        out_ref[b] = jnp.concatenate([ot_ref[j, pl.ds(b, steps, stride=batch), :] for j in range(D_MODEL // LANES)],
                                     axis=-1).astype(BF16)


def _s5(u, h0_re, h0_im, a_re8, a_im8, bmat, cmat, ssm_d, w_glu_bf, steps):
    batch, l, _ = u.shape
    rows = batch * steps
    st = jax.ShapeDtypeStruct((batch, SSM_STATES), F32)
    return pl.pallas_call(
        _s5_kernel,
        grid=(l // steps,),
        in_specs=[pl.BlockSpec((batch, steps, SSM_WIDTH), lambda i: (0, i, 0)),
                  _const_spec((batch, SSM_STATES)), _const_spec((batch, SSM_STATES)),
                  _const_spec((batch, SSM_STATES)), _const_spec((batch, SSM_STATES)),
                  _const_spec(bmat.shape), _const_spec(cmat.shape),
                  _const_spec((1, SSM_WIDTH)), _const_spec(w_glu_bf.shape)],
        out_specs=(pl.BlockSpec((batch, steps, D_MODEL), lambda i: (0, i, 0)),
                   pl.BlockSpec((batch, SSM_STATES), lambda i: (0, 0)),
                   pl.BlockSpec((batch, SSM_STATES), lambda i: (0, 0))),
        out_shape=(jax.ShapeDtypeStruct((batch, l, D_MODEL), BF16), st, st),
        scratch_shapes=[pltpu.VMEM((SSM_WIDTH // LANES, rows, LANES), F32),
                        pltpu.VMEM((rows, SSM_STATES), F32), pltpu.VMEM((rows, SSM_STATES), F32),
                        pltpu.VMEM((D_MODEL // LANES, rows, LANES), F32)],
        compiler_params=_cparams(("arbitrary",)),
        name="s5",
    )(u, h0_re, h0_im, a_re8, a_im8, bmat, cmat, ssm_d.reshape(1, SSM_WIDTH), w_glu_bf)


def _attn_causal_kernel(q_ref, k_ref, v_ref, o_ref, s_ref, m_ref, acc_ref, *, tq):
    i = pl.program_id(1)
    qc = lax.broadcasted_iota(jnp.int32, (tq, tq), 0) // CHUNK
    kc = lax.broadcasted_iota(jnp.int32, (tq, tq), 1) // CHUNK
    diag_mask = kc <= qc
    m_ref[...] = jnp.full(m_ref.shape, NEG_INF, F32)
    acc_ref[...] = jnp.zeros(acc_ref.shape, F32)

    def scores(j, mask):
        r0 = pl.multiple_of(j * tq, tq)
        for hd in range(MLA_HEADS):
            ks = slice(hd * HEAD_PAD, (hd + 1) * HEAD_PAD)
            s = _dot_nt(q_ref[0, :, ks], k_ref[0, pl.ds(r0, tq), ks])
            if mask is not None:
                s = jnp.where(mask, s, NEG_INF)
            s_ref[hd, j] = s
            m_ref[hd] = jnp.maximum(m_ref[hd], jnp.maximum(s[:, :LANES], s[:, LANES:]))

    def pass1(j, c):
        scores(j, None)
        return c

    lax.fori_loop(0, i, pass1, 0)
    scores(i, diag_mask)
    for hd in range(MLA_HEADS):
        m_ref[hd] = jnp.broadcast_to(jnp.max(m_ref[hd], axis=-1, keepdims=True), (tq, LANES))

    def pass2(j, c):
        r0 = pl.multiple_of(j * tq, tq)
        for hd in range(MLA_HEADS):
            ks = slice(hd * HEAD_PAD, (hd + 1) * HEAD_PAD)
            s = s_ref[hd, j]
            mb = m_ref[hd]
            p = jnp.concatenate([jnp.exp2(s[:, :LANES] - mb), jnp.exp2(s[:, LANES:] - mb)], axis=-1).astype(BF16)
            acc_ref[hd] += _dot(p, v_ref[0, pl.ds(r0, tq), ks])
        return c

    lax.fori_loop(0, i + 1, pass2, 0)
    for hd in range(MLA_HEADS):
        acc = acc_ref[hd]
        o_ref[0, :, hd * V_HEAD:(hd + 1) * V_HEAD] = (acc[:, :V_HEAD] / acc[:, V_HEAD:V_HEAD + 1]).astype(BF16)


def _attn_causal(q, k, v, tq):
    b, l, _ = q.shape
    assert tq == 2 * LANES and l % tq == 0 and tq % CHUNK == 0
    full = lambda w: pl.BlockSpec((1, l, w), lambda bi, qi: (bi, 0, 0))
    return pl.pallas_call(
        functools.partial(_attn_causal_kernel, tq=tq),
        grid=(b, l // tq),
        in_specs=[pl.BlockSpec((1, tq, QK_PAD), lambda bi, qi: (bi, qi, 0)), full(QK_PAD), full(QK_PAD)],
        out_specs=pl.BlockSpec((1, tq, MLA_WIDTH), lambda bi, qi: (bi, qi, 0)),
        out_shape=jax.ShapeDtypeStruct((b, l, MLA_WIDTH), BF16),
        scratch_shapes=[pltpu.VMEM((MLA_HEADS, l // tq, tq, tq), F32),
                        pltpu.VMEM((MLA_HEADS, tq, LANES), F32),
                        pltpu.VMEM((MLA_HEADS, tq, HEAD_PAD), F32)],
        compiler_params=_cparams(("parallel", "arbitrary")),
        name="attn_prompt",
    )(q, k, v)


def _attn_past_kernel(q_ref, kp_ref, vp_ref, kn_ref, vn_ref, o_ref):
    for hd in range(MLA_HEADS):
        ks = slice(hd * HEAD_PAD, (hd + 1) * HEAD_PAD)
        qh = q_ref[0, :, ks]
        s_past = _dot_nt(qh, kp_ref[0, :, ks])
        s_new = _dot_nt(qh, kn_ref[0, :, ks])
        m = jnp.maximum(jnp.max(s_past, axis=-1, keepdims=True), jnp.max(s_new, axis=-1, keepdims=True))
        p_past = jnp.exp2(s_past - m)
        p_new = jnp.exp2(s_new - m)
        o = _dot(p_past.astype(BF16), vp_ref[0, :, ks]) + _dot(p_new.astype(BF16), vn_ref[0, :, ks])
        o_ref[0, :, hd * V_HEAD:(hd + 1) * V_HEAD] = (o[:, :V_HEAD] / o[:, V_HEAD:V_HEAD + 1]).astype(BF16)


def _attn_past(q, k_past, v_past, k_new, v_new):
    b, lq, _ = q.shape
    lp = k_past.shape[1]
    assert lp % CHUNK == 0 and lq <= CHUNK
    spec = lambda n, w: pl.BlockSpec((1, n, w), lambda bi: (bi, 0, 0))
    return pl.pallas_call(
        _attn_past_kernel,
        grid=(b,),
        in_specs=[spec(lq, QK_PAD), spec(lp, QK_PAD), spec(lp, QK_PAD), spec(lq, QK_PAD), spec(lq, QK_PAD)],
        out_specs=spec(lq, MLA_WIDTH),
        out_shape=jax.ShapeDtypeStruct((b, lq, MLA_WIDTH), BF16),
        compiler_params=_cparams(("parallel",)),
        name="attn_sample",
    )(q, k_past, v_past, k_new, v_new)


ROUTE_LANES = LANES


def _first_argmax(v, lane, width):
    vmax = jnp.max(v, axis=-1, keepdims=True)
    idx = jnp.min(jnp.where(v == vmax, lane, width), axis=-1, keepdims=True)
    return vmax, idx


def _merge_kernel(x_ref, bra_ref, o_ref, om_ref, gate_ref, cnt0_ref, w_omla_ref, w_omem_ref, w_out_ref, g_ffn_ref,
                  w_rt_ref, b_rt_ref, x1_ref, h2_ref, route_ref, cnt_ref):
    first = (pl.program_id(0) == 0) & (pl.program_id(1) == 0)

    @pl.when(first)
    def _():
        cnt_ref[...] = cnt0_ref[...]

    br_b = _dot(o_ref[0], w_omla_ref[...])
    br_c = _dot(om_ref[0], w_omem_ref[...])
    g = gate_ref[0].astype(F32)
    merged = (g[:, :D_MODEL] * bra_ref[0].astype(F32) + g[:, D_MODEL:2 * D_MODEL] * br_b
              + g[:, 2 * D_MODEL:] * br_c)
    x1 = x_ref[0] + _dot(merged.astype(BF16), w_out_ref[...])
    x1_ref[0] = x1
    h2 = x1 * _rms_scale(x1, D_MODEL) * g_ffn_ref[...]
    h2_ref[0] = h2

    h2_hi = h2.astype(BF16)
    h2_lo = (h2 - h2_hi.astype(F32)).astype(BF16)
    hi_both = _dot(h2_hi, w_rt_ref[...])
    logits = (hi_both[:, :ROUTE_LANES] + hi_both[:, ROUTE_LANES:] + _dot(h2_lo, w_rt_ref[:, :ROUTE_LANES])
              + b_rt_ref[...])
    lane = lax.broadcasted_iota(jnp.int32, logits.shape, 1).astype(F32)
    ninf = jnp.float32(-jnp.inf)
    lg = jnp.where(lane < N_EXPERT_GROUPS, logits, ninf)
    lg_max, grp = _first_argmax(lg, lane, float(ROUTE_LANES))
    p_top = 1.0 / jnp.sum(jnp.exp(lg - lg_max), axis=-1, keepdims=True)
    lo = N_EXPERT_GROUPS + grp * EXPERTS_PER_GROUP
    in_grp = (lane >= lo) & (lane < lo + EXPERTS_PER_GROUP)
    le = jnp.where(in_grp, logits, ninf)
    v1, i1 = _first_argmax(le, lane, float(ROUTE_LANES))
    v2, i2 = _first_argmax(jnp.where(lane == i1, ninf, le), lane, float(ROUTE_LANES))
    e2 = jnp.exp(v2 - v1)
    w1 = p_top / (1.0 + e2)
    w2 = p_top * e2 / (1.0 + e2)
    elane = lane + N_EXPERT_GROUPS
    oh1 = elane == i1
    oh2 = elane == i2
    onehot = jnp.where(oh1 | oh2, 1.0, 0.0)
    tl = onehot.shape[0]
    tri = jnp.where(lax.broadcasted_iota(jnp.int32, (tl, tl), 0) > lax.broadcasted_iota(jnp.int32, (tl, tl), 1),
                    1.0, 0.0).astype(BF16)
    before = _dot(tri, onehot.astype(BF16)) + cnt_ref[...]
    r1 = jnp.sum(jnp.where(oh1, before, 0.0), axis=-1, keepdims=True)
    r2 = jnp.sum(jnp.where(oh2, before, 0.0), axis=-1, keepdims=True)
    cnt_ref[...] += jnp.sum(onehot, axis=0, keepdims=True)
    cols = (i1 - N_EXPERT_GROUPS, i2 - N_EXPERT_GROUPS, w1, w2, r1, r2)
    route = jnp.zeros_like(logits)
    for k, col in enumerate(cols):
        route = jnp.where(lane == k, col, route)
    route_ref[0] = route


def _merge(x, bra, o, om, gates, cnt0, wts, tl):
    b, l, _ = x.shape
    tok = lambda w: pl.BlockSpec((1, tl, w), lambda bi, li: (bi, li, 0))
    in_specs = [tok(D_MODEL), tok(D_MODEL), tok(MLA_WIDTH), tok(MEM_WIDTH), tok(N_BRANCHES * D_MODEL),
                _const_spec((1, ROUTE_LANES))] + [_const_spec(w.shape) for w in wts]
    return pl.pallas_call(
        _merge_kernel,
        grid=(b, l // tl),
        in_specs=in_specs,
        out_specs=(tok(D_MODEL), tok(D_MODEL), tok(ROUTE_LANES),
                   pl.BlockSpec((1, ROUTE_LANES), lambda bi, li: (0, 0))),
        out_shape=(jax.ShapeDtypeStruct((b, l, D_MODEL), F32), jax.ShapeDtypeStruct((b, l, D_MODEL), F32),
                   jax.ShapeDtypeStruct((b, l, ROUTE_LANES), F32), jax.ShapeDtypeStruct((1, ROUTE_LANES), F32)),
        compiler_params=_cparams(("arbitrary", "arbitrary")),
        name="merge",
    )(x, bra, o, om, gates, cnt0, *wts)


MOE_TILE = 256
MOE_DMA_TILE = 256
SMEM_1D_TILE = 1024


def _row_copy(src_ref, src_row, dst_ref, dst_row, sem):
    return pltpu.make_async_copy(src_ref.at[pl.ds(src_row, 1)], dst_ref.at[pl.ds(dst_row, 1)], sem)


def _dma_cparams():
    return pltpu.CompilerParams(dimension_semantics=("arbitrary",), vmem_limit_bytes=VMEM_LIMIT,
                                disable_bounds_checks=True)


def _slot_source_kernel(slot_ref, src0_hbm, src_hbm, src_smem, sem):
    i = pl.program_id(0)
    td = slot_ref.shape[2] // 2

    @pl.when(i == 0)
    def _():
        cp = pltpu.make_async_copy(src0_hbm, src_smem, sem)
        cp.start()
        cp.wait()

    def body(r, c):
        src_smem[slot_ref[0, 0, 2 * r]] = i * td + r
        src_smem[slot_ref[0, 0, 2 * r + 1]] = i * td + r
        return c

    lax.fori_loop(0, td, body, 0, unroll=8)

    @pl.when(i == pl.num_programs(0) - 1)
    def _():
        cp = pltpu.make_async_copy(src_smem, src_hbm, sem)
        cp.start()
        cp.wait()


def _slot_source(slots, n_slots):
    src0 = jnp.zeros((n_slots,), jnp.int32)
    return pl.pallas_call(
        _slot_source_kernel,
        grid=(slots.shape[0],),
        in_specs=[pl.BlockSpec((1, 1, slots.shape[2]), lambda i: (i, 0, 0), memory_space=pltpu.SMEM),
                  pl.BlockSpec(memory_space=pl.ANY)],
        out_specs=pl.BlockSpec(memory_space=pl.ANY),
        out_shape=jax.ShapeDtypeStruct((n_slots,), jnp.int32),
        scratch_shapes=[pltpu.SMEM((n_slots,), jnp.int32), pltpu.SemaphoreType.DMA],
        compiler_params=_dma_cparams(),
        name="slot_source",
    )(slots, src0)


def _experts_kernel(tile_expert_ref, n_tiles_ref, src_ref, h2_hbm, w1_ref, w3_ref, w2_ref, y_ref,
                    rows0_ref, rows1_ref, hb_ref, w13_bf_ref, w2_bf_ref, sem0, sem1):
    t = pl.program_id(0)
    n_tiles = n_tiles_ref[0]
    used = t < n_tiles
    bufs = ((rows0_ref, sem0), (rows1_ref, sem1))

    def gather(tile, par, wait):
        rows_ref, sem = bufs[par]
        base = tile * MOE_TILE
        if wait:
            def body(r, c):
                _row_copy(h2_hbm, src_ref[base + r], rows_ref, r, sem).wait()
                return c
            lax.fori_loop(0, MOE_TILE, body, 0, unroll=8)
        else:
            for r in range(MOE_TILE):
                _row_copy(h2_hbm, src_ref[base + r], rows_ref, r, sem).start()

    @pl.when(t == 0)
    def _():
        gather(0, 0, wait=False)

    new_expert = (t == 0) | (tile_expert_ref[t] != tile_expert_ref[jnp.maximum(t - 1, 0)])

    @pl.when(used & new_expert)
    def _():
        w13_bf_ref[:, :EXPERT_FF] = w1_ref[0].astype(BF16)
        w13_bf_ref[:, EXPERT_FF:] = w3_ref[0].astype(BF16)
        w2_bf_ref[...] = w2_ref[0].astype(BF16)

    this_tile = jnp.minimum(t, n_tiles - 1)
    next_tile = jnp.minimum(t + 1, n_tiles - 1)
    for par in range(2):
        @pl.when((t <= n_tiles) & (t % 2 == par))
        def _():
            gather(this_tile, par, wait=True)

        @pl.when(used & (t % 2 == par))
        def _():
            hb_ref[...] = bufs[par][0][...].astype(BF16)
            gather(next_tile, 1 - par, wait=False)
            a = _dot(hb_ref[...], w13_bf_ref[...])
            hid = jax.nn.silu(a[:, :EXPERT_FF]) * a[:, EXPERT_FF:]
            y_ref[...] = _dot(hid.astype(BF16), w2_bf_ref[...])

    @pl.when(jnp.logical_not(used))
    def _():
        y_ref[...] = jnp.zeros(y_ref.shape, y_ref.dtype)


def _experts(tile_expert, n_tiles, slot_src, h2, w_e1, w_e3, w_e2):
    n_grid = tile_expert.shape[0]
    w_in_spec = pl.BlockSpec((1, D_MODEL, EXPERT_FF), lambda t, te, nt, src: (te[t], 0, 0))
    grid_spec = pltpu.PrefetchScalarGridSpec(
        num_scalar_prefetch=3,
        grid=(n_grid,),
        in_specs=[pl.BlockSpec(memory_space=pl.ANY), w_in_spec, w_in_spec,
                  pl.BlockSpec((1, EXPERT_FF, D_MODEL), lambda t, te, nt, src: (te[t], 0, 0))],
        out_specs=pl.BlockSpec((MOE_TILE, D_MODEL), lambda t, te, nt, src: (t, 0)),
        scratch_shapes=[pltpu.VMEM((MOE_TILE, D_MODEL), F32), pltpu.VMEM((MOE_TILE, D_MODEL), F32),
                        pltpu.VMEM((MOE_TILE, D_MODEL), BF16),
                        pltpu.VMEM((D_MODEL, 2 * EXPERT_FF), BF16), pltpu.VMEM((EXPERT_FF, D_MODEL), BF16),
                        pltpu.SemaphoreType.DMA, pltpu.SemaphoreType.DMA],
    )
    return pl.pallas_call(
        _experts_kernel,
        grid_spec=grid_spec,
        out_shape=jax.ShapeDtypeStruct((n_grid * MOE_TILE, D_MODEL), F32),
        compiler_params=_dma_cparams(),
        name="experts",
    )(tile_expert, n_tiles, slot_src, h2, w_e1, w_e3, w_e2)


def _combine_kernel(slot_ref, x1_ref, route_ref, ye_hbm, y_ref, rows_ref, sem):
    tc = x1_ref.shape[0]
    for r in range(tc):
        for k in range(2):
            _row_copy(ye_hbm, slot_ref[0, 0, 2 * r + k], rows_ref.at[k], r, sem).start()

    def wait(r, c):
        for k in range(2):
            _row_copy(ye_hbm, slot_ref[0, 0, 2 * r + k], rows_ref.at[k], r, sem).wait()
        return c

    lax.fori_loop(0, tc, wait, 0, unroll=8)
    route = route_ref[...]
    y = x1_ref[...]
    for k in range(2):
        gate = route[:, 2 + k:3 + k]
        y = y + gate * rows_ref[k]
    y_ref[...] = y


def _combine(slots, x1, route, ye, tc):
    n = x1.shape[0]
    return pl.pallas_call(
        _combine_kernel,
        grid=(n // tc,),
        in_specs=[pl.BlockSpec((1, 1, 2 * tc), lambda i: (i, 0, 0), memory_space=pltpu.SMEM),
                  pl.BlockSpec((tc, D_MODEL), lambda i: (i, 0)),
                  pl.BlockSpec((tc, ROUTE_LANES), lambda i: (i, 0)),
                  pl.BlockSpec(memory_space=pl.ANY)],
        out_specs=pl.BlockSpec((tc, D_MODEL), lambda i: (i, 0)),
        out_shape=jax.ShapeDtypeStruct((n, D_MODEL), F32),
        scratch_shapes=[pltpu.VMEM((2, tc, D_MODEL), F32), pltpu.SemaphoreType.DMA],
        compiler_params=_dma_cparams(),
        name="combine",
    )(slots, x1, route, ye)


def _rope_tables(pos, g, scale):
    half = QK_ROPE // 2
    inv = ROPE_BASE ** (-jnp.arange(half, dtype=F32) / half)
    ang = pos.astype(F32)[:, None] * inv[None, :]
    cos, sin = jnp.cos(ang), jnp.sin(ang)
    n = pos.shape[0]
    g1, g2 = g[QK_NOPE:QK_NOPE + half], g[QK_NOPE + half:QK_HEAD]
    pad = jnp.zeros((n, HEAD_PAD - QK_HEAD), F32)
    a = jnp.concatenate([jnp.broadcast_to(g[:QK_NOPE], (n, QK_NOPE)), g1 * cos, g2 * cos, pad], axis=-1)
    b = jnp.concatenate([jnp.zeros((n, QK_NOPE), F32), -g2 * sin, g1 * sin, pad], axis=-1)
    return a * scale, b * scale


def _pad_heads(w, per_head, keep):
    k = w.shape[0]
    w = w.reshape(k, MLA_HEADS, per_head)[:, :, :keep]
    return jnp.pad(w, ((0, 0), (0, 0), (0, HEAD_PAD - keep))).reshape(k, QK_PAD)


def _swap_rope_cols(w96):
    k = w96.shape[0]
    w = w96.reshape(k, MLA_HEADS, QK_HEAD)
    half = QK_ROPE // 2
    sw = jnp.concatenate([jnp.zeros((k, MLA_HEADS, QK_NOPE), w.dtype), w[:, :, QK_NOPE + half:],
                          w[:, :, QK_NOPE:QK_NOPE + half]], axis=-1)
    return sw.reshape(k, MLA_HEADS * QK_HEAD)


def _block_diag(w, rows_per_group, cols_per_group):
    gb = SSM_GROUPS // SSM_BLOCKS
    w = w.reshape(SSM_BLOCKS, gb, rows_per_group, cols_per_group)
    eye = jnp.eye(gb, dtype=w.dtype)
    out = jnp.einsum('bgrc,gh->bgrhc', w, eye)
    return out.reshape(SSM_BLOCKS, gb * rows_per_group, gb * cols_per_group)


def _layer(x, pos, h0_re, h0_im, mk, mv, past, cnt0, p, tl):
    b, l, _ = x.shape
    scale = math.log2(math.e) / math.sqrt(QK_HEAD)
    aq, bq = _rope_tables(pos, p["g_qn"], scale)
    ak, bk = _rope_tables(pos, p["g_kn"], 1.0)
    inproj_wts = (p["g_attn"], p["w_u"], p["w_q"], p["w_kv"], p["w_pe"], p["w_qm"], p["w_g"], p["g_qlat"],
                  p["wq_pad"], p["wq_swap"], p["g_kvlat"], p["wk_pad"], p["wv"], p["vone"], p["g_mqn"])
    u, q, k, v, c_kv, k_pe, om, gates = _inproj(x, mk.astype(BF16), mv.astype(BF16), (aq, bq, ak, bk),
                                                inproj_wts, tl)
    a_re = jnp.broadcast_to(p["ab_re"], (b, SSM_STATES))
    a_im = jnp.broadcast_to(p["ab_im"], (b, SSM_STATES))
    bra, h_re, h_im = _s5(u, h0_re, h0_im, a_re, a_im, p["bmat"], p["cmat"], p["ssm_d"], p["w_glu"], S5_STEPS)
    if past is None:
        o = _attn_causal(q, k, v, 256)
    else:
        past_ckv, past_kpe = past
        lp = past_ckv.shape[1]
        akp, bkp = _rope_tables(jnp.arange(lp), p["g_kn"], 1.0)
        half = QK_ROPE // 2
        lane_pad = lambda a: jnp.pad(a, ((0, 0), (0, 0), (QK_NOPE, HEAD_PAD - QK_HEAD)))
        kpe_p = lane_pad(past_kpe)
        kpe_s = lane_pad(jnp.concatenate([past_kpe[..., half:], past_kpe[..., :half]], axis=-1))
        k_past, v_past = _kvexp(past_ckv, kpe_p, kpe_s, akp, bkp, p["wk_pad"], p["wv"], p["vone"], 512)
        o = _attn_past(q, k_past, v_past, k, v)
    merge_wts = (p["w_o_mla"], p["w_o_mem"], p["w_out"], p["g_ffn"], p["w_rt"], p["b_rt"])
    x1, h2, route, cnt = _merge(x, bra, o, om, gates, cnt0, merge_wts, tl)
    n = b * l
    return (x1.reshape(n, D_MODEL), h2.reshape(n, D_MODEL), route.reshape(n, ROUTE_LANES), cnt), c_kv, k_pe, h_re, h_im


def _slots(route, offsets, tile):
    expert = route[:, 0:2].astype(jnp.int32)
    rank = route[:, 4:6].astype(jnp.int32)
    onehot = expert[:, :, None] == jnp.arange(N_EXPERTS, dtype=jnp.int32)
    slot = jnp.sum(jnp.where(onehot, offsets, 0), axis=-1) + rank
    return slot.reshape(-1, 1, 2 * tile)


def kernel(x_prompt, x_sample, cache_mla_ckv, cache_mla_kpe, cache_ssm_re, cache_ssm_im, cache_mem_k, cache_mem_v, mem_prompt, g_attn, w_in, ssm_a_re, ssm_a_im, ssm_log_dt, ssm_b_re, ssm_b_im, ssm_c_re, ssm_c_im, ssm_d, w_glu, g_qlat, w_uq, g_kvlat, w_ukv, g_qn, g_kn, w_o_mla, g_mem, w_mem_kv, g_mqn, g_mkn, w_o_mem, w_out, g_ffn, w_rg, b_rg, w_re, b_re, w_e1, w_e3, w_e2):
    assert g_attn.shape[0] == 1, "single-layer step"
    bp, lp, _ = x_prompt.shape
    bs, ls, _ = x_sample.shape
    past_len = cache_mla_ckv.shape[2]
    lyr = 0

    o1 = SSM_WIDTH
    o2 = o1 + Q_LORA
    o3 = o2 + KV_LORA
    o4 = o3 + QK_ROPE
    o5 = o4 + MEM_WIDTH
    w_in_bf = w_in[lyr].astype(BF16)
    row = lambda a: a.reshape(1, -1)
    p = {
        "g_attn": row(g_attn[lyr]), "w_u": w_in_bf[:, :o1], "w_q": w_in_bf[:, o1:o2], "w_kv": w_in_bf[:, o2:o3],
        "w_qm": w_in_bf[:, o4:o5], "w_g": w_in_bf[:, o5:],
        "g_qlat": row(g_qlat[lyr]), "g_kvlat": row(g_kvlat[lyr]), "g_mqn": row(g_mqn[lyr]),
        "g_qn": g_qn[lyr], "g_kn": g_kn[lyr], "g_ffn": row(g_ffn[lyr]),
        "ssm_d": ssm_d[lyr], "w_glu": w_glu[lyr].astype(BF16),
        "w_o_mla": w_o_mla[lyr].astype(BF16), "w_o_mem": w_o_mem[lyr].astype(BF16), "w_out": w_out[lyr].astype(BF16),
    }
    wuq = w_uq[lyr]
    p["wq_pad"] = _pad_heads(wuq, QK_HEAD, QK_HEAD).astype(BF16)
    p["wq_swap"] = _pad_heads(_swap_rope_cols(wuq), QK_HEAD, QK_HEAD).astype(BF16)
    wukv = w_ukv[lyr]
    p["wk_pad"] = _pad_heads(wukv, QK_NOPE + V_HEAD, QK_NOPE).astype(BF16)
    p["wv"] = _pad_heads(jnp.roll(wukv.reshape(KV_LORA, MLA_HEADS, QK_NOPE + V_HEAD), -QK_NOPE, axis=-1)
                         .reshape(KV_LORA, -1), QK_NOPE + V_HEAD, V_HEAD).astype(BF16)
    p["vone"] = jnp.asarray((np.arange(QK_PAD) % HEAD_PAD == V_HEAD).astype(np.float32).reshape(1, QK_PAD))
    half = QK_ROPE // 2
    w_pe = w_in_bf[:, o3:o4]
    col_pad = lambda w: jnp.pad(w, ((0, 0), (QK_NOPE, HEAD_PAD - QK_HEAD)))
    p["w_pe"] = jnp.concatenate([col_pad(w_pe), col_pad(jnp.concatenate([w_pe[:, half:], w_pe[:, :half]], axis=-1))],
                                axis=-1)

    ab_re, ab_im, f_re, f_im = _zoh(ssm_a_re[lyr], ssm_a_im[lyr], ssm_log_dt[lyr])
    b_re_, b_im_ = ssm_b_re[lyr], ssm_b_im[lyr]
    bb_re = f_re[..., None] * b_re_ - f_im[..., None] * b_im_
    bb_im = f_re[..., None] * b_im_ + f_im[..., None] * b_re_
    to_cp = lambda w: jnp.swapaxes(w, 1, 2)
    p["bmat"] = jnp.concatenate([_block_diag(to_cp(bb_re), SSM_GROUP_CH, SSM_STATE),
                                 _block_diag(to_cp(bb_im), SSM_GROUP_CH, SSM_STATE)], axis=-1).astype(BF16)
    to_pc = lambda w: jnp.swapaxes(w, 1, 2)
    p["cmat"] = jnp.stack([_block_diag(to_pc(ssm_c_re[lyr]), SSM_STATE, SSM_GROUP_CH),
                           _block_diag(to_pc(ssm_c_im[lyr]), SSM_STATE, SSM_GROUP_CH)]).astype(BF16)
    p["ab_re"] = ab_re.reshape(1, SSM_STATES)
    p["ab_im"] = ab_im.reshape(1, SSM_STATES)

    w_rt = jnp.concatenate([w_rg[lyr], w_re[lyr]], axis=-1)
    w_rt = jnp.pad(w_rt, ((0, 0), (0, ROUTE_LANES - w_rt.shape[1])))
    w_rt_hi = w_rt.astype(BF16)
    p["w_rt"] = jnp.concatenate([w_rt_hi, (w_rt - w_rt_hi.astype(F32)).astype(BF16)], axis=-1)
    b_rt = jnp.concatenate([b_rg[lyr], b_re[lyr].reshape(-1)])
    p["b_rt"] = jnp.pad(b_rt, (0, ROUTE_LANES - b_rt.shape[0])).reshape(1, ROUTE_LANES)

    mk, mv = _memkv(mem_prompt.reshape(-1, D_MODEL), g_mem[lyr], w_mem_kv[lyr].astype(BF16), g_mkn[lyr])
    m_tok = mem_prompt.shape[1]
    mk3, mv3 = mk.reshape(bp, m_tok, MEM_WIDTH), mv.reshape(bp, m_tok, MEM_WIDTH)
    zeros = jnp.zeros((bp, SSM_STATES), F32)
    cnt0 = jnp.zeros((1, ROUTE_LANES), F32)
    (x1_p, h2p_p, route_p, cnt_p), ckv_p, kpe_p, sre_p, sim_p = _layer(
        x_prompt, jnp.arange(lp), zeros, zeros, mk3, mv3, None, cnt0, p, tl=PROMPT_TILE)

    (x1_s, h2p_s, route_s, cnt), ckv_s, kpe_s, sre_s, sim_s = _layer(
        x_sample, past_len + jnp.arange(ls), cache_ssm_re[lyr].reshape(bs, SSM_STATES),
        cache_ssm_im[lyr].reshape(bs, SSM_STATES), cache_mem_k[lyr].reshape(bs, -1, MEM_WIDTH),
        cache_mem_v[lyr].reshape(bs, -1, MEM_WIDTH), (cache_mla_ckv[lyr], cache_mla_kpe[lyr]), cnt_p, p, tl=ls)

    n_p, n_s = x1_p.shape[0], x1_s.shape[0]
    counts = cnt[0, :N_EXPERTS].astype(jnp.int32)
    padded = (counts + MOE_TILE - 1) // MOE_TILE * MOE_TILE
    ends = jnp.cumsum(padded)
    offsets = ends - padded
    max_tiles = 2 * (n_p + n_s) // MOE_TILE + N_EXPERTS + 1
    tile_start = jnp.arange(max_tiles, dtype=jnp.int32) * MOE_TILE
    tile_expert = jnp.sum((ends[None, :] <= tile_start[:, None]).astype(jnp.int32), axis=1)
    tile_expert = jnp.minimum(tile_expert, N_EXPERTS - 1)
    n_tiles = (ends[-1:] // MOE_TILE).astype(jnp.int32)
    slots_p = _slots(route_p, offsets, MOE_DMA_TILE)
    slots_s = _slots(route_s, offsets, MOE_DMA_TILE)
    n_src = -(-max_tiles * MOE_TILE // SMEM_1D_TILE) * SMEM_1D_TILE
    slot_src = _slot_source(jnp.concatenate([slots_p, slots_s], axis=0), n_src)
    h2_all = jnp.concatenate([h2p_p, h2p_s], axis=0)
    ye = _experts(tile_expert, n_tiles, slot_src, h2_all, w_e1[lyr], w_e3[lyr], w_e2[lyr])
    yp = _combine(slots_p, x1_p, route_p, ye, MOE_DMA_TILE).reshape(bp, lp, D_MODEL)
    ys = _combine(slots_s, x1_s, route_s, ye, MOE_DMA_TILE).reshape(bs, ls, D_MODEL)

    st = lambda a, bsz: a.reshape(1, bsz, SSM_GROUPS, SSM_STATE)
    mem_shape = (1, bp, m_tok, MEM_HEADS, MEM_HEAD)
    return (yp, ys, ckv_p[None], kpe_p[None], st(sre_p, bp), st(sim_p, bp), mk.reshape(mem_shape),
            mv.reshape(mem_shape), ckv_s[None], kpe_s[None], st(sre_s, bs), st(sim_s, bs))
```

```python
import functools
import math

import jax
import jax.numpy as jnp
import numpy as np
from jax import lax
from jax.experimental import pallas as pl
from jax.experimental.pallas import tpu as pltpu

D_MODEL = 1024
CHUNK = 64
RMS_EPS = 1e-6
SSM_GROUPS = 32
SSM_GROUP_CH = 16
SSM_WIDTH = SSM_GROUPS * SSM_GROUP_CH
SSM_STATE = 64
SSM_STATES = SSM_GROUPS * SSM_STATE
SSM_BLOCKS = 2
MLA_HEADS = 8
QK_NOPE = 64
QK_ROPE = 32
QK_HEAD = QK_NOPE + QK_ROPE
V_HEAD = 64
Q_LORA = 384
KV_LORA = 256
ROPE_BASE = 10000.0
MLA_WIDTH = MLA_HEADS * V_HEAD
MEM_HEADS = 4
MEM_HEAD = 128
MEM_WIDTH = MEM_HEADS * MEM_HEAD
N_BRANCHES = 3
N_EXPERT_GROUPS = 4
EXPERTS_PER_GROUP = 8
N_EXPERTS = N_EXPERT_GROUPS * EXPERTS_PER_GROUP
EXPERT_FF = 256

LANES = 128
SUBLANES = 8
HEAD_PAD = LANES
QK_PAD = MLA_HEADS * HEAD_PAD
VMEM_LIMIT = 56 * 1024 * 1024
PROMPT_TILE = 512

BF16 = jnp.bfloat16
F32 = jnp.float32
NEG_INF = -1e30


def _cparams(sem):
    return pltpu.CompilerParams(dimension_semantics=sem, vmem_limit_bytes=VMEM_LIMIT)


def _const_spec(shape):
    nd = len(shape)
    return pl.BlockSpec(shape, lambda *_: (0,) * nd, pipeline_mode=pl.Buffered(1))


def _rms_scale(xf, width):
    return lax.rsqrt(jnp.sum(xf * xf, axis=-1, keepdims=True) * (1.0 / width) + RMS_EPS)


def _dot(a, b):
    return jnp.dot(a, b, preferred_element_type=F32)


def _dot_nt(a, b):
    return lax.dot_general(a, b, (((1,), (1,)), ((), ())), preferred_element_type=F32)


def _zoh_kernel(lr_ref, li_ref, ldt_ref, abr_ref, abi_ref, fr_ref, fi_ref):
    lr = lr_ref[...]
    li = li_ref[...]
    dt = jnp.exp(ldt_ref[...])
    mag = jnp.exp(lr * dt)
    ab_re = mag * jnp.cos(li * dt)
    ab_im = mag * jnp.sin(li * dt)
    den = lr * lr + li * li
    nr = ab_re - 1.0
    ni = ab_im
    abr_ref[...] = ab_re
    abi_ref[...] = ab_im
    fr_ref[...] = (nr * lr + ni * li) / den
    fi_ref[...] = (ni * lr - nr * li) / den


def _zoh(a_re, a_im, log_dt):
    shp = jax.ShapeDtypeStruct((SSM_GROUPS, SSM_STATE), F32)
    return pl.pallas_call(_zoh_kernel, out_shape=(shp, shp, shp, shp), name="zoh")(
        a_re, a_im, log_dt.reshape(SSM_GROUPS, 1))


def _expand_kv(ckv_bf, kpe_p, kpe_s, wk_ref, wv_ref, vone_ref, ak, bk):
    k_nope = _dot(ckv_bf, wk_ref[...])
    v = _dot(ckv_bf, wv_ref[...]) + vone_ref[...]
    ss_pe = jnp.sum(kpe_p * kpe_p, axis=-1, keepdims=True)
    rot = kpe_p * ak + kpe_s * bk
    heads = []
    for h in range(MLA_HEADS):
        kh = k_nope[:, h * HEAD_PAD:(h + 1) * HEAD_PAD]
        ss = jnp.sum(kh * kh, axis=-1, keepdims=True) + ss_pe
        rs = lax.rsqrt(ss * (1.0 / QK_HEAD) + RMS_EPS)
        heads.append((rs * (kh * ak + rot)).astype(BF16))
    return jnp.concatenate(heads, axis=-1), v.astype(BF16)


def _memkv_kernel(mem_ref, g_ref, w_ref, gk_ref, k_ref, v_ref):
    x = mem_ref[...]
    h = (x * _rms_scale(x, D_MODEL) * g_ref[...]).astype(BF16)
    kv = _dot(h, w_ref[...])
    gk = gk_ref[...]
    for hd in range(MEM_HEADS):
        kh = kv[:, hd * MEM_HEAD:(hd + 1) * MEM_HEAD]
        k_ref[:, hd * MEM_HEAD:(hd + 1) * MEM_HEAD] = kh * _rms_scale(kh, MEM_HEAD) * gk
    v_ref[...] = kv[:, MEM_WIDTH:]


def _memkv(mem2d, g_mem, w_mem_kv_bf, g_mkn):
    n = mem2d.shape[0]
    tm = 256
    out = jax.ShapeDtypeStruct((n, MEM_WIDTH), F32)
    return pl.pallas_call(
        _memkv_kernel,
        grid=(n // tm,),
        in_specs=[pl.BlockSpec((tm, D_MODEL), lambda i: (i, 0)),
                  _const_spec((1, D_MODEL)),
                  _const_spec((D_MODEL, 2 * MEM_WIDTH)),
                  _const_spec((1, MEM_HEAD))],
        out_specs=(pl.BlockSpec((tm, MEM_WIDTH), lambda i: (i, 0)),
                   pl.BlockSpec((tm, MEM_WIDTH), lambda i: (i, 0))),
        out_shape=(out, out),
        compiler_params=_cparams(("parallel",)),
        name="memkv",
    )(mem2d, g_mem.reshape(1, D_MODEL), w_mem_kv_bf, g_mkn.reshape(1, MEM_HEAD))


def _inproj_kernel(x_ref, mk_ref, mv_ref, aq_ref, bq_ref, ak_ref, bk_ref,
                   g_attn_ref, w_u_ref, w_q_ref, w_kv_ref, w_pe_ref, w_qm_ref, w_g_ref,
                   g_qlat_ref, wq_ref, wqs_ref, g_kvlat_ref, wk_ref, wv_ref, vone_ref, g_mqn_ref,
                   u_ref, q_ref, k_ref, v_ref, ckv_ref, kpe_ref, om_ref, gate_ref):
    x = x_ref[0]
    h = (x * _rms_scale(x, D_MODEL) * g_attn_ref[...]).astype(BF16)

    u_ref[0] = _dot(h, w_u_ref[...])
    gate_ref[0] = jax.nn.sigmoid(_dot(h, w_g_ref[...])).astype(BF16)

    q_lat = _dot(h, w_q_ref[...])
    qn = (q_lat * _rms_scale(q_lat, Q_LORA) * g_qlat_ref[...]).astype(BF16)
    q_up = _dot(qn, wq_ref[...])
    q_sw = _dot(qn, wqs_ref[...])
    aq = aq_ref[...]
    bq = bq_ref[...]
    for hd in range(MLA_HEADS):
        sl = slice(hd * HEAD_PAD, (hd + 1) * HEAD_PAD)
        qh = q_up[:, sl]
        rs = _rms_scale(qh, QK_HEAD)
        q_ref[0, :, sl] = (rs * (qh * aq + q_sw[:, sl] * bq)).astype(BF16)

    kv_lat = _dot(h, w_kv_ref[...])
    c_kv = kv_lat * _rms_scale(kv_lat, KV_LORA) * g_kvlat_ref[...]
    ckv_ref[0] = c_kv
    kpe_p = _dot(h, w_pe_ref[:, :HEAD_PAD])
    kpe_s = _dot(h, w_pe_ref[:, HEAD_PAD:])
    kpe_ref[0] = kpe_p[:, QK_NOPE:QK_HEAD]
    k_all, v_all = _expand_kv(c_kv.astype(BF16), kpe_p, kpe_s, wk_ref, wv_ref, vone_ref, ak_ref[...], bk_ref[...])
    k_ref[0] = k_all
    v_ref[0] = v_all

    q_mem = _dot(h, w_qm_ref[...])
    gq = g_mqn_ref[...] * (1.0 / math.sqrt(MEM_HEAD))
    for hd in range(MEM_HEADS):
        sl = slice(hd * MEM_HEAD, (hd + 1) * MEM_HEAD)
        qh = q_mem[:, sl]
        qh = (qh * _rms_scale(qh, MEM_HEAD) * gq).astype(BF16)
        s = _dot_nt(qh, mk_ref[0, :, sl])
        p = jnp.exp(s - jnp.max(s, axis=-1, keepdims=True))
        o = _dot(p.astype(BF16), mv_ref[0, :, sl])
        om_ref[0, :, sl] = (o / jnp.sum(p, axis=-1, keepdims=True)).astype(BF16)


def _inproj(x, mk_bf, mv_bf, tabs, wts, tl):
    b, l, _ = x.shape
    m = mk_bf.shape[1]
    aq, bq, ak, bk = tabs
    tok = lambda w: pl.BlockSpec((1, tl, w), lambda bi, li: (bi, li, 0))
    tab = pl.BlockSpec((tl, HEAD_PAD), lambda bi, li: (li, 0))
    memspec = pl.BlockSpec((1, m, MEM_WIDTH), lambda bi, li: (bi, 0, 0))
    in_specs = [tok(D_MODEL), memspec, memspec, tab, tab, tab, tab] + [_const_spec(w.shape) for w in wts]
    out_shape = (
        jax.ShapeDtypeStruct((b, l, SSM_WIDTH), F32),
        jax.ShapeDtypeStruct((b, l, QK_PAD), BF16),
        jax.ShapeDtypeStruct((b, l, QK_PAD), BF16),
        jax.ShapeDtypeStruct((b, l, QK_PAD), BF16),
        jax.ShapeDtypeStruct((b, l, KV_LORA), F32),
        jax.ShapeDtypeStruct((b, l, QK_ROPE), F32),
        jax.ShapeDtypeStruct((b, l, MEM_WIDTH), BF16),
        jax.ShapeDtypeStruct((b, l, N_BRANCHES * D_MODEL), BF16),
    )
    out_specs = (
        tok(SSM_WIDTH), tok(QK_PAD), tok(QK_PAD), tok(QK_PAD), tok(KV_LORA), tok(QK_ROPE), tok(MEM_WIDTH),
        tok(N_BRANCHES * D_MODEL),
    )
    return pl.pallas_call(
        _inproj_kernel,
        grid=(b, l // tl),
        in_specs=in_specs,
        out_specs=out_specs,
        out_shape=out_shape,
        compiler_params=_cparams(("parallel", "parallel")),
        name="inproj",
    )(x, mk_bf, mv_bf, aq, bq, ak, bk, *wts)


def _kvexp_kernel(ckv_ref, kpe_ref, ak_ref, bk_ref, wk_ref, wv_ref, vone_ref, k_ref, v_ref):
    kpe = kpe_ref[0]
    half = QK_ROPE // 2
    lo = jnp.zeros((kpe.shape[0], QK_NOPE), F32)
    hi = jnp.zeros((kpe.shape[0], HEAD_PAD - QK_HEAD), F32)
    kpe_p = jnp.concatenate([lo, kpe, hi], axis=-1)
    kpe_s = jnp.concatenate([lo, kpe[:, half:], kpe[:, :half], hi], axis=-1)
    k_all, v_all = _expand_kv(ckv_ref[0].astype(BF16), kpe_p, kpe_s, wk_ref, wv_ref, vone_ref,
                              ak_ref[...], bk_ref[...])
    k_ref[0] = k_all
    v_ref[0] = v_all


def _kvexp(ckv, kpe, ak, bk, wk, wv, vone, tl):
    b, l, _ = ckv.shape
    tok = lambda w: pl.BlockSpec((1, tl, w), lambda bi, li: (bi, li, 0))
    tab = pl.BlockSpec((tl, HEAD_PAD), lambda bi, li: (li, 0))
    return pl.pallas_call(
        _kvexp_kernel,
        grid=(b, l // tl),
        in_specs=[tok(KV_LORA), tok(QK_ROPE), tab, tab] + [_const_spec(w.shape) for w in (wk, wv, vone)],
        out_specs=(tok(QK_PAD), tok(QK_PAD)),
        out_shape=(jax.ShapeDtypeStruct((b, l, QK_PAD), BF16), jax.ShapeDtypeStruct((b, l, QK_PAD), BF16)),
        compiler_params=_cparams(("parallel", "parallel")),
        name="kvexp",
    )(ckv, kpe, ak, bk, wk, wv, vone)


SCAN_LANES = 1024
S5_STEPS = 64


def _s5_kernel(u_ref, h0r_ref, h0i_ref, ar_ref, ai_ref, bmat_ref, cmat_ref, d_ref, wglu_ref,
               out_ref, hr_ref, hi_ref, ut_ref, sre_ref, sim_ref, ot_ref):
    i = pl.program_id(0)
    batch, steps, _ = u_ref.shape
    rows = batch * steps

    @pl.when(i == 0)
    def _():
        hr_ref[...] = h0r_ref[...]
        hi_ref[...] = h0i_ref[...]

    u_bm = u_ref[...].reshape(rows, SSM_WIDTH)
    for j in range(SSM_WIDTH // LANES):
        ut_ref[j] = u_bm[:, j * LANES:(j + 1) * LANES]
    u = jnp.concatenate(
        [jnp.concatenate([ut_ref[j, pl.ds(t, batch, stride=steps), :] for j in range(SSM_WIDTH // LANES)], axis=-1)
         for t in range(steps)], axis=0)
    ub = u.astype(BF16)
    blk_ch = SSM_WIDTH // SSM_BLOCKS
    blk_st = SSM_STATES // SSM_BLOCKS
    for blk in range(SSM_BLOCKS):
        bu = _dot(ub[:, blk * blk_ch:(blk + 1) * blk_ch], bmat_ref[blk])
        sre_ref[:, blk * blk_st:(blk + 1) * blk_st] = bu[:, :blk_st]
        sim_ref[:, blk * blk_st:(blk + 1) * blk_st] = bu[:, blk_st:]

    for c in range(SSM_STATES // SCAN_LANES):
        sl = slice(c * SCAN_LANES, (c + 1) * SCAN_LANES)
        a_re = ar_ref[:, sl]
        a_im = ai_ref[:, sl]

        def body(t, carry):
            h_re, h_im = carry
            r0 = pl.multiple_of(t * batch, batch)
            n_re = a_re * h_re - a_im * h_im + sre_ref[pl.ds(r0, batch), sl]
            n_im = a_re * h_im + a_im * h_re + sim_ref[pl.ds(r0, batch), sl]
            sre_ref[pl.ds(r0, batch), sl] = n_re
            sim_ref[pl.ds(r0, batch), sl] = n_im
            return n_re, n_im

        h_re, h_im = lax.fori_loop(0, steps, body, (hr_ref[:, sl], hi_ref[:, sl]), unroll=2)
        hr_ref[:, sl] = h_re
        hi_ref[:, sl] = h_im

    ys = []
    for blk in range(SSM_BLOCKS):
        st = slice(blk * blk_st, (blk + 1) * blk_st)
        ys.append(_dot(sre_ref[:, st].astype(BF16), cmat_ref[0, blk])
                  - _dot(sim_ref[:, st].astype(BF16), cmat_ref[1, blk]))
    y = jnp.concatenate(ys, axis=-1) + d_ref[...] * u
    z = jax.nn.gelu(y).astype(BF16)
    zz = _dot(z, wglu_ref[...])
    out = zz[:, :D_MODEL] * jax.nn.sigmoid(zz[:, D_MODEL:])
    for j in range(D_MODEL // LANES):
        ot_ref[j] = out[:, j * LANES:(j + 1) * LANES]
    for b in range(batch):
        out_ref[b] = jnp.concatenate([ot_ref[j, pl.ds(b, steps, stride=batch), :] for j in range(D_MODEL // LANES)],
                                     axis=-1).astype(BF16)


def _s5(u, h0_re, h0_im, a_re8, a_im8, bmat, cmat, ssm_d, w_glu_bf, steps):
    batch, l, _ = u.shape
    rows = batch * steps
    st = jax.ShapeDtypeStruct((batch, SSM_STATES), F32)
    return pl.pallas_call(
        _s5_kernel,
        grid=(l // steps,),
        in_specs=[pl.BlockSpec((batch, steps, SSM_WIDTH), lambda i: (0, i, 0)),
                  _const_spec((batch, SSM_STATES)), _const_spec((batch, SSM_STATES)),
                  _const_spec((batch, SSM_STATES)), _const_spec((batch, SSM_STATES)),
                  _const_spec(bmat.shape), _const_spec(cmat.shape),
                  _const_spec((1, SSM_WIDTH)), _const_spec(w_glu_bf.shape)],
        out_specs=(pl.BlockSpec((batch, steps, D_MODEL), lambda i: (0, i, 0)),
                   pl.BlockSpec((batch, SSM_STATES), lambda i: (0, 0)),
                   pl.BlockSpec((batch, SSM_STATES), lambda i: (0, 0))),
        out_shape=(jax.ShapeDtypeStruct((batch, l, D_MODEL), BF16), st, st),
        scratch_shapes=[pltpu.VMEM((SSM_WIDTH // LANES, rows, LANES), F32),
                        pltpu.VMEM((rows, SSM_STATES), F32), pltpu.VMEM((rows, SSM_STATES), F32),
                        pltpu.VMEM((D_MODEL // LANES, rows, LANES), F32)],
        compiler_params=_cparams(("arbitrary",)),
        name="s5",
    )(u, h0_re, h0_im, a_re8, a_im8, bmat, cmat, ssm_d.reshape(1, SSM_WIDTH), w_glu_bf)


def _attn_causal_kernel(q_ref, k_ref, v_ref, o_ref, s_ref, m_ref, acc_ref, *, tq):
    i = pl.program_id(1)
    qc = lax.broadcasted_iota(jnp.int32, (tq, tq), 0) // CHUNK
    kc = lax.broadcasted_iota(jnp.int32, (tq, tq), 1) // CHUNK
    diag_mask = kc <= qc
    m_ref[...] = jnp.full(m_ref.shape, NEG_INF, F32)
    acc_ref[...] = jnp.zeros(acc_ref.shape, F32)

    def scores(j, mask):
        r0 = pl.multiple_of(j * tq, tq)
        for hd in range(MLA_HEADS):
            ks = slice(hd * HEAD_PAD, (hd + 1) * HEAD_PAD)
            s = _dot_nt(q_ref[0, :, ks], k_ref[0, pl.ds(r0, tq), ks])
            if mask is not None:
                s = jnp.where(mask, s, NEG_INF)
            s_ref[hd, j] = s
            m_ref[hd] = jnp.maximum(m_ref[hd], jnp.maximum(s[:, :LANES], s[:, LANES:]))

    def pass1(j, c):
        scores(j, None)
        return c

    lax.fori_loop(0, i, pass1, 0)
    scores(i, diag_mask)
    for hd in range(MLA_HEADS):
        m_ref[hd] = jnp.broadcast_to(jnp.max(m_ref[hd], axis=-1, keepdims=True), (tq, LANES))

    def pass2(j, c):
        r0 = pl.multiple_of(j * tq, tq)
        for hd in range(MLA_HEADS):
            ks = slice(hd * HEAD_PAD, (hd + 1) * HEAD_PAD)
            s = s_ref[hd, j]
            mb = m_ref[hd]
            p = jnp.concatenate([jnp.exp2(s[:, :LANES] - mb), jnp.exp2(s[:, LANES:] - mb)], axis=-1).astype(BF16)
            acc_ref[hd] += _dot(p, v_ref[0, pl.ds(r0, tq), ks])
        return c

    lax.fori_loop(0, i + 1, pass2, 0)
    for hd in range(MLA_HEADS):
        acc = acc_ref[hd]
        o_ref[0, :, hd * V_HEAD:(hd + 1) * V_HEAD] = (acc[:, :V_HEAD] / acc[:, V_HEAD:V_HEAD + 1]).astype(BF16)


def _attn_causal(q, k, v, tq):
    b, l, _ = q.shape
    assert tq == 2 * LANES and l % tq == 0 and tq % CHUNK == 0
    full = lambda w: pl.BlockSpec((1, l, w), lambda bi, qi: (bi, 0, 0))
    return pl.pallas_call(
        functools.partial(_attn_causal_kernel, tq=tq),
        grid=(b, l // tq),
        in_specs=[pl.BlockSpec((1, tq, QK_PAD), lambda bi, qi: (bi, qi, 0)), full(QK_PAD), full(QK_PAD)],
        out_specs=pl.BlockSpec((1, tq, MLA_WIDTH), lambda bi, qi: (bi, qi, 0)),
        out_shape=jax.ShapeDtypeStruct((b, l, MLA_WIDTH), BF16),
        scratch_shapes=[pltpu.VMEM((MLA_HEADS, l // tq, tq, tq), F32),
                        pltpu.VMEM((MLA_HEADS, tq, LANES), F32),
                        pltpu.VMEM((MLA_HEADS, tq, HEAD_PAD), F32)],
        compiler_params=_cparams(("parallel", "arbitrary")),
        name="attn_prompt",
    )(q, k, v)


def _attn_past_kernel(q_ref, kp_ref, vp_ref, kn_ref, vn_ref, o_ref):
    for hd in range(MLA_HEADS):
        ks = slice(hd * HEAD_PAD, (hd + 1) * HEAD_PAD)
        qh = q_ref[0, :, ks]
        s_past = _dot_nt(qh, kp_ref[0, :, ks])
        s_new = _dot_nt(qh, kn_ref[0, :, ks])
        m = jnp.maximum(jnp.max(s_past, axis=-1, keepdims=True), jnp.max(s_new, axis=-1, keepdims=True))
        p_past = jnp.exp2(s_past - m)
        p_new = jnp.exp2(s_new - m)
        o = _dot(p_past.astype(BF16), vp_ref[0, :, ks]) + _dot(p_new.astype(BF16), vn_ref[0, :, ks])
        o_ref[0, :, hd * V_HEAD:(hd + 1) * V_HEAD] = (o[:, :V_HEAD] / o[:, V_HEAD:V_HEAD + 1]).astype(BF16)


def _attn_past(q, k_past, v_past, k_new, v_new):
    b, lq, _ = q.shape
    lp = k_past.shape[1]
    assert lp % CHUNK == 0 and lq <= CHUNK
    spec = lambda n, w: pl.BlockSpec((1, n, w), lambda bi: (bi, 0, 0))
    return pl.pallas_call(
        _attn_past_kernel,
        grid=(b,),
        in_specs=[spec(lq, QK_PAD), spec(lp, QK_PAD), spec(lp, QK_PAD), spec(lq, QK_PAD), spec(lq, QK_PAD)],
        out_specs=spec(lq, MLA_WIDTH),
        out_shape=jax.ShapeDtypeStruct((b, lq, MLA_WIDTH), BF16),
        compiler_params=_cparams(("parallel",)),
        name="attn_sample",
    )(q, k_past, v_past, k_new, v_new)


ROUTE_LANES = LANES


def _first_argmax(v, lane, width):
    vmax = jnp.max(v, axis=-1, keepdims=True)
    idx = jnp.min(jnp.where(v == vmax, lane, width), axis=-1, keepdims=True)
    return vmax, idx


def _merge_kernel(x_ref, bra_ref, o_ref, om_ref, gate_ref, cnt0_ref, w_omla_ref, w_omem_ref, w_out_ref, g_ffn_ref,
                  w_rt_ref, b_rt_ref, x1_ref, h2_ref, route_ref, cnt_ref):
    first = (pl.program_id(0) == 0) & (pl.program_id(1) == 0)

    @pl.when(first)
    def _():
        cnt_ref[...] = cnt0_ref[...]

    br_b = _dot(o_ref[0], w_omla_ref[...])
    br_c = _dot(om_ref[0], w_omem_ref[...])
    g = gate_ref[0].astype(F32)
    merged = (g[:, :D_MODEL] * bra_ref[0].astype(F32) + g[:, D_MODEL:2 * D_MODEL] * br_b
              + g[:, 2 * D_MODEL:] * br_c)
    x1 = x_ref[0] + _dot(merged.astype(BF16), w_out_ref[...])
    x1_ref[0] = x1
    h2 = x1 * _rms_scale(x1, D_MODEL) * g_ffn_ref[...]
    h2_ref[0] = h2

    h2_hi = h2.astype(BF16)
    h2_lo = (h2 - h2_hi.astype(F32)).astype(BF16)
    hi_both = _dot(h2_hi, w_rt_ref[...])
    logits = (hi_both[:, :ROUTE_LANES] + hi_both[:, ROUTE_LANES:] + _dot(h2_lo, w_rt_ref[:, :ROUTE_LANES])
              + b_rt_ref[...])
    lane = lax.broadcasted_iota(jnp.int32, logits.shape, 1).astype(F32)
    ninf = jnp.float32(-jnp.inf)
    lg = jnp.where(lane < N_EXPERT_GROUPS, logits, ninf)
    lg_max, grp = _first_argmax(lg, lane, float(ROUTE_LANES))
    p_top = 1.0 / jnp.sum(jnp.exp(lg - lg_max), axis=-1, keepdims=True)
    lo = N_EXPERT_GROUPS + grp * EXPERTS_PER_GROUP
    in_grp = (lane >= lo) & (lane < lo + EXPERTS_PER_GROUP)
    le = jnp.where(in_grp, logits, ninf)
    v1, i1 = _first_argmax(le, lane, float(ROUTE_LANES))
    v2, i2 = _first_argmax(jnp.where(lane == i1, ninf, le), lane, float(ROUTE_LANES))
    e2 = jnp.exp(v2 - v1)
    w1 = p_top / (1.0 + e2)
    w2 = p_top * e2 / (1.0 + e2)
    elane = lane + N_EXPERT_GROUPS
    oh1 = elane == i1
    oh2 = elane == i2
    onehot = jnp.where(oh1 | oh2, 1.0, 0.0)
    tl = onehot.shape[0]
    tri = jnp.where(lax.broadcasted_iota(jnp.int32, (tl, tl), 0) > lax.broadcasted_iota(jnp.int32, (tl, tl), 1),
                    1.0, 0.0).astype(BF16)
    before = _dot(tri, onehot.astype(BF16)) + cnt_ref[...]
    r1 = jnp.sum(jnp.where(oh1, before, 0.0), axis=-1, keepdims=True)
    r2 = jnp.sum(jnp.where(oh2, before, 0.0), axis=-1, keepdims=True)
    cnt_ref[...] += jnp.sum(onehot, axis=0, keepdims=True)
    cols = (i1 - N_EXPERT_GROUPS, i2 - N_EXPERT_GROUPS, w1, w2, r1, r2)
    route = jnp.zeros_like(logits)
    for k, col in enumerate(cols):
        route = jnp.where(lane == k, col, route)
    route_ref[0] = route


def _merge(x, bra, o, om, gates, cnt0, wts, tl):
    b, l, _ = x.shape
    tok = lambda w: pl.BlockSpec((1, tl, w), lambda bi, li: (bi, li, 0))
    in_specs = [tok(D_MODEL), tok(D_MODEL), tok(MLA_WIDTH), tok(MEM_WIDTH), tok(N_BRANCHES * D_MODEL),
                _const_spec((1, ROUTE_LANES))] + [_const_spec(w.shape) for w in wts]
    return pl.pallas_call(
        _merge_kernel,
        grid=(b, l // tl),
        in_specs=in_specs,
        out_specs=(tok(D_MODEL), tok(D_MODEL), tok(ROUTE_LANES),
                   pl.BlockSpec((1, ROUTE_LANES), lambda bi, li: (0, 0))),
        out_shape=(jax.ShapeDtypeStruct((b, l, D_MODEL), F32), jax.ShapeDtypeStruct((b, l, D_MODEL), F32),
                   jax.ShapeDtypeStruct((b, l, ROUTE_LANES), F32), jax.ShapeDtypeStruct((1, ROUTE_LANES), F32)),
        compiler_params=_cparams(("arbitrary", "arbitrary")),
        name="merge",
    )(x, bra, o, om, gates, cnt0, *wts)


MOE_TILE = 256
MOE_DMA_TILE = 512


def _row_copy(src_ref, src_row, dst_ref, dst_row, sem):
    return pltpu.make_async_copy(src_ref.at[pl.ds(src_row, 1)], dst_ref.at[pl.ds(dst_row, 1)], sem)


def _tile_copy(src_ref, dst_hbm, tile, sem):
    return pltpu.make_async_copy(src_ref, dst_hbm.at[pl.ds(tile * MOE_TILE, MOE_TILE)], sem)


def _dma_cparams():
    return pltpu.CompilerParams(dimension_semantics=("arbitrary",), vmem_limit_bytes=VMEM_LIMIT,
                                disable_bounds_checks=True)


def _dispatch_kernel(last_tile_ref, n_tiles_ref, slot_ref, h2p_ref, h2s_ref, hs_hbm, zero_ref, sem, zsem,
                     *, steps_p, grid_tiles):
    i = pl.program_id(0)
    td = slot_ref.shape[2]

    @pl.when(i == 0)
    def _():
        zero_ref[...] = jnp.zeros(zero_ref.shape, zero_ref.dtype)
        for wait in (False, True):
            def pad_tile(e, c):
                @pl.when(last_tile_ref[e] >= 0)
                def _():
                    cp = _tile_copy(zero_ref, hs_hbm, last_tile_ref[e], zsem)
                    cp.wait() if wait else cp.start()
                return c

            def tail_tile(t, c):
                cp = _tile_copy(zero_ref, hs_hbm, t, zsem)
                cp.wait() if wait else cp.start()
                return c

            lax.fori_loop(0, N_EXPERTS, pad_tile, 0)
            lax.fori_loop(n_tiles_ref[0], grid_tiles, tail_tile, 0)

    def scatter(h2_ref):
        def start(r, c):
            for k in range(2):
                _row_copy(h2_ref, r, hs_hbm, slot_ref[0, k, r], sem).start()
            return c

        def wait(r, c):
            for k in range(2):
                _row_copy(h2_ref, r, hs_hbm, slot_ref[0, k, r], sem).wait()
            return c

        lax.fori_loop(0, td, start, 0, unroll=8)
        lax.fori_loop(0, td, wait, 0, unroll=8)

    @pl.when(i < steps_p)
    def _():
        scatter(h2p_ref)

    @pl.when(i >= steps_p)
    def _():
        scatter(h2s_ref)


def _dispatch(last_tile, n_tiles, slots, h2_p, h2_s, grid_tiles, td):
    steps_p, steps_s = h2_p.shape[0] // td, h2_s.shape[0] // td
    grid_spec = pltpu.PrefetchScalarGridSpec(
        num_scalar_prefetch=2,
        grid=(steps_p + steps_s,),
        in_specs=[pl.BlockSpec((1, 2, td), lambda i, lt, nt: (i, 0, 0), memory_space=pltpu.SMEM),
                  pl.BlockSpec((td, D_MODEL), lambda i, lt, nt: (jnp.minimum(i, steps_p - 1), 0)),
                  pl.BlockSpec((td, D_MODEL), lambda i, lt, nt: (jnp.maximum(i - steps_p, 0), 0))],
        out_specs=pl.BlockSpec(memory_space=pl.ANY),
        scratch_shapes=[pltpu.VMEM((MOE_TILE, D_MODEL), F32), pltpu.SemaphoreType.DMA, pltpu.SemaphoreType.DMA],
    )
    return pl.pallas_call(
        functools.partial(_dispatch_kernel, steps_p=steps_p, grid_tiles=grid_tiles),
        grid_spec=grid_spec,
        out_shape=jax.ShapeDtypeStruct((grid_tiles * MOE_TILE, D_MODEL), F32),
        compiler_params=_dma_cparams(),
        name="dispatch",
    )(last_tile, n_tiles, slots, h2_p, h2_s)


def _experts_kernel(tile_expert_ref, n_tiles_ref, hs_ref, w1_ref, w3_ref, w2_ref, y_ref, w13_bf_ref, w2_bf_ref):
    t = pl.program_id(0)
    used = t < n_tiles_ref[0]
    new_expert = (t == 0) | (tile_expert_ref[t] != tile_expert_ref[jnp.maximum(t - 1, 0)])

    @pl.when(used & new_expert)
    def _():
        w13_bf_ref[:, :EXPERT_FF] = w1_ref[0].astype(BF16)
        w13_bf_ref[:, EXPERT_FF:] = w3_ref[0].astype(BF16)
        w2_bf_ref[...] = w2_ref[0].astype(BF16)

    @pl.when(used)
    def _():
        a = _dot(hs_ref[...].astype(BF16), w13_bf_ref[...])
        hid = jax.nn.silu(a[:, :EXPERT_FF]) * a[:, EXPERT_FF:]
        y_ref[...] = _dot(hid.astype(BF16), w2_bf_ref[...])

    @pl.when(jnp.logical_not(used))
    def _():
        y_ref[...] = jnp.zeros(y_ref.shape, y_ref.dtype)


def _experts(tile_expert, n_tiles, hs, w_e1, w_e3, w_e2):
    s = hs.shape[0]
    w_in_spec = pl.BlockSpec((1, D_MODEL, EXPERT_FF), lambda t, te, nt: (te[t], 0, 0))
    grid_spec = pltpu.PrefetchScalarGridSpec(
        num_scalar_prefetch=2,
        grid=(s // MOE_TILE,),
        in_specs=[pl.BlockSpec((MOE_TILE, D_MODEL), lambda t, te, nt: (t, 0)), w_in_spec, w_in_spec,
                  pl.BlockSpec((1, EXPERT_FF, D_MODEL), lambda t, te, nt: (te[t], 0, 0))],
        out_specs=pl.BlockSpec((MOE_TILE, D_MODEL), lambda t, te, nt: (t, 0)),
        scratch_shapes=[pltpu.VMEM((D_MODEL, 2 * EXPERT_FF), BF16), pltpu.VMEM((EXPERT_FF, D_MODEL), BF16)],
    )
    return pl.pallas_call(
        _experts_kernel,
        grid_spec=grid_spec,
        out_shape=jax.ShapeDtypeStruct((s, D_MODEL), F32),
        compiler_params=_cparams(("arbitrary",)),
        name="experts",
    )(tile_expert, n_tiles, hs, w_e1, w_e3, w_e2)


def _combine_kernel(slot_ref, x1_ref, route_ref, ye_hbm, y_ref, rows_ref, sem):
    tc = slot_ref.shape[2]

    def start(r, c):
        for k in range(2):
            _row_copy(ye_hbm, slot_ref[0, k, r], rows_ref.at[k], r, sem).start()
        return c

    def wait(r, c):
        for k in range(2):
            _row_copy(ye_hbm, slot_ref[0, k, r], rows_ref.at[k], r, sem).wait()
        return c

    lax.fori_loop(0, tc, start, 0, unroll=8)
    lax.fori_loop(0, tc, wait, 0, unroll=8)
    route = route_ref[...]
    y = x1_ref[...]
    for k in range(2):
        gate = route[:, 2 + k:3 + k]
        y = y + gate * rows_ref[k]
    y_ref[...] = y


def _combine(slots, x1, route, ye, tc):
    n = x1.shape[0]
    return pl.pallas_call(
        _combine_kernel,
        grid=(n // tc,),
        in_specs=[pl.BlockSpec((1, 2, tc), lambda i: (i, 0, 0), memory_space=pltpu.SMEM),
                  pl.BlockSpec((tc, D_MODEL), lambda i: (i, 0)),
                  pl.BlockSpec((tc, ROUTE_LANES), lambda i: (i, 0)),
                  pl.BlockSpec(memory_space=pl.ANY)],
        out_specs=pl.BlockSpec((tc, D_MODEL), lambda i: (i, 0)),
        out_shape=jax.ShapeDtypeStruct((n, D_MODEL), F32),
        scratch_shapes=[pltpu.VMEM((2, tc, D_MODEL), F32), pltpu.SemaphoreType.DMA],
        compiler_params=_dma_cparams(),
        name="combine",
    )(slots, x1, route, ye)


def _rope_tables(pos, g, scale):
    half = QK_ROPE // 2
    inv = ROPE_BASE ** (-jnp.arange(half, dtype=F32) / half)
    ang = pos.astype(F32)[:, None] * inv[None, :]
    cos, sin = jnp.cos(ang), jnp.sin(ang)
    n = pos.shape[0]
    g1, g2 = g[QK_NOPE:QK_NOPE + half], g[QK_NOPE + half:QK_HEAD]
    pad = jnp.zeros((n, HEAD_PAD - QK_HEAD), F32)
    a = jnp.concatenate([jnp.broadcast_to(g[:QK_NOPE], (n, QK_NOPE)), g1 * cos, g2 * cos, pad], axis=-1)
    b = jnp.concatenate([jnp.zeros((n, QK_NOPE), F32), -g2 * sin, g1 * sin, pad], axis=-1)
    return a * scale, b * scale


def _pad_heads(w, per_head, keep):
    k = w.shape[0]
    w = w.reshape(k, MLA_HEADS, per_head)[:, :, :keep]
    return jnp.pad(w, ((0, 0), (0, 0), (0, HEAD_PAD - keep))).reshape(k, QK_PAD)


def _swap_rope_cols(w96):
    k = w96.shape[0]
    w = w96.reshape(k, MLA_HEADS, QK_HEAD)
    half = QK_ROPE // 2
    sw = jnp.concatenate([jnp.zeros((k, MLA_HEADS, QK_NOPE), w.dtype), w[:, :, QK_NOPE + half:],
                          w[:, :, QK_NOPE:QK_NOPE + half]], axis=-1)
    return sw.reshape(k, MLA_HEADS * QK_HEAD)


def _block_diag(w, rows_per_group, cols_per_group):
    gb = SSM_GROUPS // SSM_BLOCKS
    w = w.reshape(SSM_BLOCKS, gb, rows_per_group, cols_per_group)
    eye = jnp.eye(gb, dtype=w.dtype)
    out = jnp.einsum('bgrc,gh->bgrhc', w, eye)
    return out.reshape(SSM_BLOCKS, gb * rows_per_group, gb * cols_per_group)


def _layer(x, pos, h0_re, h0_im, mk, mv, past, cnt0, p, tl):
    b, l, _ = x.shape
    scale = math.log2(math.e) / math.sqrt(QK_HEAD)
    aq, bq = _rope_tables(pos, p["g_qn"], scale)
    ak, bk = _rope_tables(pos, p["g_kn"], 1.0)
    inproj_wts = (p["g_attn"], p["w_u"], p["w_q"], p["w_kv"], p["w_pe"], p["w_qm"], p["w_g"], p["g_qlat"],
                  p["wq_pad"], p["wq_swap"], p["g_kvlat"], p["wk_pad"], p["wv"], p["vone"], p["g_mqn"])
    u, q, k, v, c_kv, k_pe, om, gates = _inproj(x, mk.astype(BF16), mv.astype(BF16), (aq, bq, ak, bk),
                                                inproj_wts, tl)
    a_re = jnp.broadcast_to(p["ab_re"], (b, SSM_STATES))
    a_im = jnp.broadcast_to(p["ab_im"], (b, SSM_STATES))
    bra, h_re, h_im = _s5(u, h0_re, h0_im, a_re, a_im, p["bmat"], p["cmat"], p["ssm_d"], p["w_glu"], S5_STEPS)
    if past is None:
        o = _attn_causal(q, k, v, 256)
    else:
        past_ckv, past_kpe = past
        lp = past_ckv.shape[1]
        akp, bkp = _rope_tables(jnp.arange(lp), p["g_kn"], 1.0)
        k_past, v_past = _kvexp(past_ckv, past_kpe, akp, bkp, p["wk_pad"], p["wv"], p["vone"], 512)
        o = _attn_past(q, k_past, v_past, k, v)
    merge_wts = (p["w_o_mla"], p["w_o_mem"], p["w_out"], p["g_ffn"], p["w_rt"], p["b_rt"])
    x1, h2, route, cnt = _merge(x, bra, o, om, gates, cnt0, merge_wts, tl)
    n = b * l
    return (x1.reshape(n, D_MODEL), h2.reshape(n, D_MODEL), route.reshape(n, ROUTE_LANES), cnt), c_kv, k_pe, h_re, h_im


def _slots(route, offsets, tile):
    expert = route[:, 0:2].astype(jnp.int32)
    rank = route[:, 4:6].astype(jnp.int32)
    onehot = expert[:, :, None] == jnp.arange(N_EXPERTS, dtype=jnp.int32)
    slot = jnp.sum(jnp.where(onehot, offsets, 0), axis=-1) + rank
    return slot.reshape(-1, tile, 2).transpose(0, 2, 1)


def kernel(x_prompt, x_sample, cache_mla_ckv, cache_mla_kpe, cache_ssm_re, cache_ssm_im, cache_mem_k, cache_mem_v, mem_prompt, g_attn, w_in, ssm_a_re, ssm_a_im, ssm_log_dt, ssm_b_re, ssm_b_im, ssm_c_re, ssm_c_im, ssm_d, w_glu, g_qlat, w_uq, g_kvlat, w_ukv, g_qn, g_kn, w_o_mla, g_mem, w_mem_kv, g_mqn, g_mkn, w_o_mem, w_out, g_ffn, w_rg, b_rg, w_re, b_re, w_e1, w_e3, w_e2):
    assert g_attn.shape[0] == 1, "single-layer step"
    bp, lp, _ = x_prompt.shape
    bs, ls, _ = x_sample.shape
    past_len = cache_mla_ckv.shape[2]
    lyr = 0

    o1 = SSM_WIDTH
    o2 = o1 + Q_LORA
    o3 = o2 + KV_LORA
    o4 = o3 + QK_ROPE
    o5 = o4 + MEM_WIDTH
    w_in_bf = w_in[lyr].astype(BF16)
    row = lambda a: a.reshape(1, -1)
    p = {
        "g_attn": row(g_attn[lyr]), "w_u": w_in_bf[:, :o1], "w_q": w_in_bf[:, o1:o2], "w_kv": w_in_bf[:, o2:o3],
        "w_qm": w_in_bf[:, o4:o5], "w_g": w_in_bf[:, o5:],
        "g_qlat": row(g_qlat[lyr]), "g_kvlat": row(g_kvlat[lyr]), "g_mqn": row(g_mqn[lyr]),
        "g_qn": g_qn[lyr], "g_kn": g_kn[lyr], "g_ffn": row(g_ffn[lyr]),
        "ssm_d": ssm_d[lyr], "w_glu": w_glu[lyr].astype(BF16),
        "w_o_mla": w_o_mla[lyr].astype(BF16), "w_o_mem": w_o_mem[lyr].astype(BF16), "w_out": w_out[lyr].astype(BF16),
    }
    wuq = w_uq[lyr]
    p["wq_pad"] = _pad_heads(wuq, QK_HEAD, QK_HEAD).astype(BF16)
    p["wq_swap"] = _pad_heads(_swap_rope_cols(wuq), QK_HEAD, QK_HEAD).astype(BF16)
    wukv = w_ukv[lyr]
    p["wk_pad"] = _pad_heads(wukv, QK_NOPE + V_HEAD, QK_NOPE).astype(BF16)
    p["wv"] = _pad_heads(jnp.roll(wukv.reshape(KV_LORA, MLA_HEADS, QK_NOPE + V_HEAD), -QK_NOPE, axis=-1)
                         .reshape(KV_LORA, -1), QK_NOPE + V_HEAD, V_HEAD).astype(BF16)
    p["vone"] = jnp.asarray((np.arange(QK_PAD) % HEAD_PAD == V_HEAD).astype(np.float32).reshape(1, QK_PAD))
    half = QK_ROPE // 2
    w_pe = w_in_bf[:, o3:o4]
    col_pad = lambda w: jnp.pad(w, ((0, 0), (QK_NOPE, HEAD_PAD - QK_HEAD)))
    p["w_pe"] = jnp.concatenate([col_pad(w_pe), col_pad(jnp.concatenate([w_pe[:, half:], w_pe[:, :half]], axis=-1))],
                                axis=-1)

    ab_re, ab_im, f_re, f_im = _zoh(ssm_a_re[lyr], ssm_a_im[lyr], ssm_log_dt[lyr])
    b_re_, b_im_ = ssm_b_re[lyr], ssm_b_im[lyr]
    bb_re = f_re[..., None] * b_re_ - f_im[..., None] * b_im_
    bb_im = f_re[..., None] * b_im_ + f_im[..., None] * b_re_
    to_cp = lambda w: jnp.swapaxes(w, 1, 2)
    p["bmat"] = jnp.concatenate([_block_diag(to_cp(bb_re), SSM_GROUP_CH, SSM_STATE),
                                 _block_diag(to_cp(bb_im), SSM_GROUP_CH, SSM_STATE)], axis=-1).astype(BF16)
    to_pc = lambda w: jnp.swapaxes(w, 1, 2)
    p["cmat"] = jnp.stack([_block_diag(to_pc(ssm_c_re[lyr]), SSM_STATE, SSM_GROUP_CH),
                           _block_diag(to_pc(ssm_c_im[lyr]), SSM_STATE, SSM_GROUP_CH)]).astype(BF16)
    p["ab_re"] = ab_re.reshape(1, SSM_STATES)
    p["ab_im"] = ab_im.reshape(1, SSM_STATES)

    w_rt = jnp.concatenate([w_rg[lyr], w_re[lyr]], axis=-1)
    w_rt = jnp.pad(w_rt, ((0, 0), (0, ROUTE_LANES - w_rt.shape[1])))
    w_rt_hi = w_rt.astype(BF16)
    p["w_rt"] = jnp.concatenate([w_rt_hi, (w_rt - w_rt_hi.astype(F32)).astype(BF16)], axis=-1)
    b_rt = jnp.concatenate([b_rg[lyr], b_re[lyr].reshape(-1)])
    p["b_rt"] = jnp.pad(b_rt, (0, ROUTE_LANES - b_rt.shape[0])).reshape(1, ROUTE_LANES)

    mk, mv = _memkv(mem_prompt.reshape(-1, D_MODEL), g_mem[lyr], w_mem_kv[lyr].astype(BF16), g_mkn[lyr])
    m_tok = mem_prompt.shape[1]
    mk3, mv3 = mk.reshape(bp, m_tok, MEM_WIDTH), mv.reshape(bp, m_tok, MEM_WIDTH)
    zeros = jnp.zeros((bp, SSM_STATES), F32)
    cnt0 = jnp.zeros((1, ROUTE_LANES), F32)
    (x1_p, h2p_p, route_p, cnt_p), ckv_p, kpe_p, sre_p, sim_p = _layer(
        x_prompt, jnp.arange(lp), zeros, zeros, mk3, mv3, None, cnt0, p, tl=PROMPT_TILE)

    (x1_s, h2p_s, route_s, cnt), ckv_s, kpe_s, sre_s, sim_s = _layer(
        x_sample, past_len + jnp.arange(ls), cache_ssm_re[lyr].reshape(bs, SSM_STATES),
        cache_ssm_im[lyr].reshape(bs, SSM_STATES), cache_mem_k[lyr].reshape(bs, -1, MEM_WIDTH),
        cache_mem_v[lyr].reshape(bs, -1, MEM_WIDTH), (cache_mla_ckv[lyr], cache_mla_kpe[lyr]), cnt_p, p, tl=ls)

    n_p, n_s = x1_p.shape[0], x1_s.shape[0]
    counts = cnt[0, :N_EXPERTS].astype(jnp.int32)
    padded = (counts + MOE_TILE - 1) // MOE_TILE * MOE_TILE
    ends = jnp.cumsum(padded)
    offsets = ends - padded
    max_tiles = 2 * (n_p + n_s) // MOE_TILE + N_EXPERTS
    tile_start = jnp.arange(max_tiles, dtype=jnp.int32) * MOE_TILE
    tile_expert = jnp.sum((ends[None, :] <= tile_start[:, None]).astype(jnp.int32), axis=1)
    tile_expert = jnp.minimum(tile_expert, N_EXPERTS - 1)
    n_tiles = (ends[-1:] // MOE_TILE).astype(jnp.int32)
    slots_p = _slots(route_p, offsets, MOE_DMA_TILE)
    slots_s = _slots(route_s, offsets, MOE_DMA_TILE)
    last_tile = jnp.where(padded > 0, ends // MOE_TILE - 1, -1).astype(jnp.int32)
    hs = _dispatch(last_tile, n_tiles, jnp.concatenate([slots_p, slots_s], axis=0), h2p_p, h2p_s, max_tiles,
                   MOE_DMA_TILE)
    ye = _experts(tile_expert, n_tiles, hs, w_e1[lyr], w_e3[lyr], w_e2[lyr])
    yp = _combine(slots_p, x1_p, route_p, ye, MOE_DMA_TILE).reshape(bp, lp, D_MODEL)
    ys = _combine(slots_s, x1_s, route_s, ye, MOE_DMA_TILE).reshape(bs, ls, D_MODEL)

    st = lambda a, bsz: a.reshape(1, bsz, SSM_GROUPS, SSM_STATE)
    mem_shape = (1, bp, m_tok, MEM_HEADS, MEM_HEAD)
    return (yp, ys, ckv_p[None], kpe_p[None], st(sre_p, bp), st(sim_p, bp), mk.reshape(mem_shape),
            mv.reshape(mem_shape), ckv_s[None], kpe_s[None], st(sre_s, bs), st(sim_s, bs))
```

```python
import functools
import math

import jax
import jax.numpy as jnp
import numpy as np
from jax import lax
from jax.experimental import pallas as pl
from jax.experimental.pallas import tpu as pltpu

D_MODEL = 1024
CHUNK = 64
RMS_EPS = 1e-6
SSM_GROUPS = 32
SSM_GROUP_CH = 16
SSM_WIDTH = SSM_GROUPS * SSM_GROUP_CH
SSM_STATE = 64
SSM_STATES = SSM_GROUPS * SSM_STATE
SSM_BLOCKS = 2
MLA_HEADS = 8
QK_NOPE = 64
QK_ROPE = 32
QK_HEAD = QK_NOPE + QK_ROPE
V_HEAD = 64
Q_LORA = 384
KV_LORA = 256
ROPE_BASE = 10000.0
MLA_WIDTH = MLA_HEADS * V_HEAD
MEM_HEADS = 4
MEM_HEAD = 128
MEM_WIDTH = MEM_HEADS * MEM_HEAD
N_BRANCHES = 3
N_EXPERT_GROUPS = 4
EXPERTS_PER_GROUP = 8
N_EXPERTS = N_EXPERT_GROUPS * EXPERTS_PER_GROUP
EXPERT_FF = 256

LANES = 128
SUBLANES = 8
HEAD_PAD = LANES
QK_PAD = MLA_HEADS * HEAD_PAD
VMEM_LIMIT = 56 * 1024 * 1024
PROMPT_TILE = 512

BF16 = jnp.bfloat16
F32 = jnp.float32
NEG_INF = -1e30


def _cparams(sem):
    return pltpu.CompilerParams(dimension_semantics=sem, vmem_limit_bytes=VMEM_LIMIT)


def _const_spec(shape):
    nd = len(shape)
    return pl.BlockSpec(shape, lambda *_: (0,) * nd, pipeline_mode=pl.Buffered(1))


def _rms_scale(xf, width):
    return lax.rsqrt(jnp.sum(xf * xf, axis=-1, keepdims=True) * (1.0 / width) + RMS_EPS)


def _sigmoid(x):
    return 0.5 * jnp.tanh(0.5 * x) + 0.5


def _dot(a, b):
    return jnp.dot(a, b, preferred_element_type=F32)


def _dot_nt(a, b):
    return lax.dot_general(a, b, (((1,), (1,)), ((), ())), preferred_element_type=F32)


def _zoh_kernel(lr_ref, li_ref, ldt_ref, abr_ref, abi_ref, fr_ref, fi_ref):
    lr = lr_ref[...]
    li = li_ref[...]
    dt = jnp.exp(ldt_ref[...])
    mag = jnp.exp(lr * dt)
    ab_re = mag * jnp.cos(li * dt)
    ab_im = mag * jnp.sin(li * dt)
    den = lr * lr + li * li
    nr = ab_re - 1.0
    ni = ab_im
    abr_ref[...] = ab_re
    abi_ref[...] = ab_im
    fr_ref[...] = (nr * lr + ni * li) / den
    fi_ref[...] = (ni * lr - nr * li) / den


def _zoh(a_re, a_im, log_dt):
    shp = jax.ShapeDtypeStruct((SSM_GROUPS, SSM_STATE), F32)
    return pl.pallas_call(_zoh_kernel, out_shape=(shp, shp, shp, shp), name="zoh")(
        a_re, a_im, log_dt.reshape(SSM_GROUPS, 1))


def _expand_kv(ckv_bf, kpe_p, kpe_s, wk_ref, wv_ref, vone_ref, ak, bk):
    k_nope = _dot(ckv_bf, wk_ref[...])
    v = _dot(ckv_bf, wv_ref[...]) + vone_ref[...]
    ss_pe = jnp.sum(kpe_p * kpe_p, axis=-1, keepdims=True)
    rot = kpe_p * ak + kpe_s * bk
    heads = []
    for h in range(MLA_HEADS):
        kh = k_nope[:, h * HEAD_PAD:(h + 1) * HEAD_PAD]
        ss = jnp.sum(kh * kh, axis=-1, keepdims=True) + ss_pe
        rs = lax.rsqrt(ss * (1.0 / QK_HEAD) + RMS_EPS)
        heads.append((rs * (kh * ak + rot)).astype(BF16))
    return jnp.concatenate(heads, axis=-1), v.astype(BF16)


def _memkv_kernel(mem_ref, g_ref, w_ref, gk_ref, k_ref, v_ref, kb_ref, vb_ref):
    x = mem_ref[...]
    h = (x * _rms_scale(x, D_MODEL) * g_ref[...]).astype(BF16)
    kv = _dot(h, w_ref[...])
    gk = gk_ref[...]
    for hd in range(MEM_HEADS):
        sl = slice(hd * MEM_HEAD, (hd + 1) * MEM_HEAD)
        kh = kv[:, sl]
        kh = kh * _rms_scale(kh, MEM_HEAD) * gk
        k_ref[:, sl] = kh
        kb_ref[:, sl] = kh.astype(BF16)
    v_ref[...] = kv[:, MEM_WIDTH:]
    vb_ref[...] = kv[:, MEM_WIDTH:].astype(BF16)


def _memkv(mem2d, g_mem, w_mem_kv_bf, g_mkn):
    n = mem2d.shape[0]
    tm = 256
    out = jax.ShapeDtypeStruct((n, MEM_WIDTH), F32)
    out_bf = jax.ShapeDtypeStruct((n, MEM_WIDTH), BF16)
    blk = pl.BlockSpec((tm, MEM_WIDTH), lambda i: (i, 0))
    return pl.pallas_call(
        _memkv_kernel,
        grid=(n // tm,),
        in_specs=[pl.BlockSpec((tm, D_MODEL), lambda i: (i, 0)),
                  _const_spec((1, D_MODEL)),
                  _const_spec((D_MODEL, 2 * MEM_WIDTH)),
                  _const_spec((1, MEM_HEAD))],
        out_specs=(blk, blk, blk, blk),
        out_shape=(out, out, out_bf, out_bf),
        compiler_params=_cparams(("parallel",)),
        name="memkv",
    )(mem2d, g_mem.reshape(1, D_MODEL), w_mem_kv_bf, g_mkn.reshape(1, MEM_HEAD))


def _inproj_kernel(x_ref, mk_ref, mv_ref, aq_ref, bq_ref, ak_ref, bk_ref,
                   g_attn_ref, w_u_ref, w_q_ref, w_kv_ref, w_pe_ref, w_qm_ref, w_g_ref,
                   g_qlat_ref, wq_ref, wqs_ref, g_kvlat_ref, wk_ref, wv_ref, vone_ref, g_mqn_ref,
                   u_ref, q_ref, k_ref, v_ref, ckv_ref, kpe_ref, om_ref, gate_ref):
    x = x_ref[0]
    h = (x * _rms_scale(x, D_MODEL) * g_attn_ref[...]).astype(BF16)

    u_ref[0] = _dot(h, w_u_ref[...])
    gate_ref[0] = _sigmoid(_dot(h, w_g_ref[...])).astype(BF16)

    q_lat = _dot(h, w_q_ref[...])
    qn = (q_lat * _rms_scale(q_lat, Q_LORA) * g_qlat_ref[...]).astype(BF16)
    q_up = _dot(qn, wq_ref[...])
    q_sw = _dot(qn, wqs_ref[...])
    aq = aq_ref[...]
    bq = bq_ref[...]
    for hd in range(MLA_HEADS):
        sl = slice(hd * HEAD_PAD, (hd + 1) * HEAD_PAD)
        qh = q_up[:, sl]
        rs = _rms_scale(qh, QK_HEAD)
        q_ref[0, :, sl] = (rs * (qh * aq + q_sw[:, sl] * bq)).astype(BF16)

    kv_lat = _dot(h, w_kv_ref[...])
    c_kv = kv_lat * _rms_scale(kv_lat, KV_LORA) * g_kvlat_ref[...]
    ckv_ref[0] = c_kv
    kpe_p = _dot(h, w_pe_ref[:, :HEAD_PAD])
    kpe_s = _dot(h, w_pe_ref[:, HEAD_PAD:])
    kpe_ref[0] = kpe_p[:, QK_NOPE:QK_HEAD]
    k_all, v_all = _expand_kv(c_kv.astype(BF16), kpe_p, kpe_s, wk_ref, wv_ref, vone_ref, ak_ref[...], bk_ref[...])
    k_ref[0] = k_all
    v_ref[0] = v_all

    q_mem = _dot(h, w_qm_ref[...])
    gq = g_mqn_ref[...] * (1.0 / math.sqrt(MEM_HEAD))
    for hd in range(MEM_HEADS):
        sl = slice(hd * MEM_HEAD, (hd + 1) * MEM_HEAD)
        qh = q_mem[:, sl]
        qh = (qh * _rms_scale(qh, MEM_HEAD) * gq).astype(BF16)
        s = _dot_nt(qh, mk_ref[0, :, sl])
        p = jnp.exp(s - jnp.max(s, axis=-1, keepdims=True))
        o = _dot(p.astype(BF16), mv_ref[0, :, sl])
        om_ref[0, :, sl] = (o / jnp.sum(p, axis=-1, keepdims=True)).astype(BF16)


def _inproj(x, mk_bf, mv_bf, tabs, wts, tl):
    b, l, _ = x.shape
    m = mk_bf.shape[1]
    aq, bq, ak, bk = tabs
    tok = lambda w: pl.BlockSpec((1, tl, w), lambda bi, li: (bi, li, 0))
    tab = pl.BlockSpec((tl, HEAD_PAD), lambda bi, li: (li, 0))
    memspec = pl.BlockSpec((1, m, MEM_WIDTH), lambda bi, li: (bi, 0, 0))
    in_specs = [tok(D_MODEL), memspec, memspec, tab, tab, tab, tab] + [_const_spec(w.shape) for w in wts]
    out_shape = (
        jax.ShapeDtypeStruct((b, l, SSM_WIDTH), F32),
        jax.ShapeDtypeStruct((b, l, QK_PAD), BF16),
        jax.ShapeDtypeStruct((b, l, QK_PAD), BF16),
        jax.ShapeDtypeStruct((b, l, QK_PAD), BF16),
        jax.ShapeDtypeStruct((b, l, KV_LORA), F32),
        jax.ShapeDtypeStruct((b, l, QK_ROPE), F32),
        jax.ShapeDtypeStruct((b, l, MEM_WIDTH), BF16),
        jax.ShapeDtypeStruct((b, l, N_BRANCHES * D_MODEL), BF16),
    )
    out_specs = (
        tok(SSM_WIDTH), tok(QK_PAD), tok(QK_PAD), tok(QK_PAD), tok(KV_LORA), tok(QK_ROPE), tok(MEM_WIDTH),
        tok(N_BRANCHES * D_MODEL),
    )
    return pl.pallas_call(
        _inproj_kernel,
        grid=(b, l // tl),
        in_specs=in_specs,
        out_specs=out_specs,
        out_shape=out_shape,
        compiler_params=_cparams(("parallel", "parallel")),
        name="inproj",
    )(x, mk_bf, mv_bf, aq, bq, ak, bk, *wts)


def _kvexp_kernel(ckv_ref, kpe_ref, ak_ref, bk_ref, wk_ref, wv_ref, vone_ref, k_ref, v_ref):
    kpe = kpe_ref[0]
    half = QK_ROPE // 2
    lo = jnp.zeros((kpe.shape[0], QK_NOPE), F32)
    hi = jnp.zeros((kpe.shape[0], HEAD_PAD - QK_HEAD), F32)
    kpe_p = jnp.concatenate([lo, kpe, hi], axis=-1)
    kpe_s = jnp.concatenate([lo, kpe[:, half:], kpe[:, :half], hi], axis=-1)
    k_all, v_all = _expand_kv(ckv_ref[0].astype(BF16), kpe_p, kpe_s, wk_ref, wv_ref, vone_ref,
                              ak_ref[...], bk_ref[...])
    k_ref[0] = k_all
    v_ref[0] = v_all


def _kvexp(ckv, kpe, ak, bk, wk, wv, vone, tl):
    b, l, _ = ckv.shape
    tok = lambda w: pl.BlockSpec((1, tl, w), lambda bi, li: (bi, li, 0))
    tab = pl.BlockSpec((tl, HEAD_PAD), lambda bi, li: (li, 0))
    return pl.pallas_call(
        _kvexp_kernel,
        grid=(b, l // tl),
        in_specs=[tok(KV_LORA), tok(QK_ROPE), tab, tab] + [_const_spec(w.shape) for w in (wk, wv, vone)],
        out_specs=(tok(QK_PAD), tok(QK_PAD)),
        out_shape=(jax.ShapeDtypeStruct((b, l, QK_PAD), BF16), jax.ShapeDtypeStruct((b, l, QK_PAD), BF16)),
        compiler_params=_cparams(("parallel", "parallel")),
        name="kvexp",
    )(ckv, kpe, ak, bk, wk, wv, vone)


SCAN_LANES = 1024
S5_STEPS = 64


def _s5_kernel(u_ref, h0r_ref, h0i_ref, ar_ref, ai_ref, bmat_ref, cmat_ref, d_ref, wglu_ref,
               out_ref, hr_ref, hi_ref, ut_ref, sre_ref, sim_ref, ot_ref):
    i = pl.program_id(0)
    batch, steps, _ = u_ref.shape
    rows = batch * steps

    @pl.when(i == 0)
    def _():
        hr_ref[...] = h0r_ref[...]
        hi_ref[...] = h0i_ref[...]

    u_bm = u_ref[...].reshape(rows, SSM_WIDTH)
    for j in range(SSM_WIDTH // LANES):
        ut_ref[j] = u_bm[:, j * LANES:(j + 1) * LANES]
    u = jnp.concatenate(
        [jnp.concatenate([ut_ref[j, pl.ds(t, batch, stride=steps), :] for j in range(SSM_WIDTH // LANES)], axis=-1)
         for t in range(steps)], axis=0)
    ub = u.astype(BF16)
    blk_ch = SSM_WIDTH // SSM_BLOCKS
    blk_st = SSM_STATES // SSM_BLOCKS
    for blk in range(SSM_BLOCKS):
        bu = _dot(ub[:, blk * blk_ch:(blk + 1) * blk_ch], bmat_ref[blk])
        sre_ref[:, blk * blk_st:(blk + 1) * blk_st] = bu[:, :blk_st]
        sim_ref[:, blk * blk_st:(blk + 1) * blk_st] = bu[:, blk_st:]

    for c in range(SSM_STATES // SCAN_LANES):
        sl = slice(c * SCAN_LANES, (c + 1) * SCAN_LANES)
        a_re = ar_ref[:, sl]
        a_im = ai_ref[:, sl]

        def body(t, carry):
            h_re, h_im = carry
            r0 = pl.multiple_of(t * batch, batch)
            n_re = a_re * h_re - a_im * h_im + sre_ref[pl.ds(r0, batch), sl]
            n_im = a_re * h_im + a_im * h_re + sim_ref[pl.ds(r0, batch), sl]
            sre_ref[pl.ds(r0, batch), sl] = n_re
            sim_ref[pl.ds(r0, batch), sl] = n_im
            return n_re, n_im

        h_re, h_im = lax.fori_loop(0, steps, body, (hr_ref[:, sl], hi_ref[:, sl]), unroll=2)
        hr_ref[:, sl] = h_re
        hi_ref[:, sl] = h_im

    ys = []
    for blk in range(SSM_BLOCKS):
        st = slice(blk * blk_st, (blk + 1) * blk_st)
        ys.append(_dot(sre_ref[:, st].astype(BF16), cmat_ref[0, blk])
                  - _dot(sim_ref[:, st].astype(BF16), cmat_ref[1, blk]))
    y = jnp.concatenate(ys, axis=-1) + d_ref[...] * u
    z = jax.nn.gelu(y).astype(BF16)
    zz = _dot(z, wglu_ref[...])
    out = zz[:, :D_MODEL] * _sigmoid(zz[:, D_MODEL:])
    for j in range(D_MODEL // LANES):
        ot_ref[j] = out[:, j * LANES:(j + 1) * LANES]
    for b in range(batch):
        out_ref[b] = jnp.concatenate([ot_ref[j, pl.ds(b, steps, stride=batch), :] for j in range(D_MODEL // LANES)],
                                     axis=-1).astype(BF16)


def _s5(u, h0_re, h0_im, a_re8, a_im8, bmat, cmat, ssm_d, w_glu_bf, steps):
    batch, l, _ = u.shape
    rows = batch * steps
    st = jax.ShapeDtypeStruct((batch, SSM_STATES), F32)
    return pl.pallas_call(
        _s5_kernel,
        grid=(l // steps,),
        in_specs=[pl.BlockSpec((batch, steps, SSM_WIDTH), lambda i: (0, i, 0)),
                  _const_spec((batch, SSM_STATES)), _const_spec((batch, SSM_STATES)),
                  _const_spec((batch, SSM_STATES)), _const_spec((batch, SSM_STATES)),
                  _const_spec(bmat.shape), _const_spec(cmat.shape),
                  _const_spec((1, SSM_WIDTH)), _const_spec(w_glu_bf.shape)],
        out_specs=(pl.BlockSpec((batch, steps, D_MODEL), lambda i: (0, i, 0)),
                   pl.BlockSpec((batch, SSM_STATES), lambda i: (0, 0)),
                   pl.BlockSpec((batch, SSM_STATES), lambda i: (0, 0))),
        out_shape=(jax.ShapeDtypeStruct((batch, l, D_MODEL), BF16), st, st),
        scratch_shapes=[pltpu.VMEM((SSM_WIDTH // LANES, rows, LANES), F32),
                        pltpu.VMEM((rows, SSM_STATES), F32), pltpu.VMEM((rows, SSM_STATES), F32),
                        pltpu.VMEM((D_MODEL // LANES, rows, LANES), F32)],
        compiler_params=_cparams(("arbitrary",)),
        name="s5",
    )(u, h0_re, h0_im, a_re8, a_im8, bmat, cmat, ssm_d.reshape(1, SSM_WIDTH), w_glu_bf)


def _attn_causal_kernel(q_ref, k_ref, v_ref, o_ref, s_ref, m_ref, acc_ref, *, tq):
    i = pl.program_id(1)
    qc = lax.broadcasted_iota(jnp.int32, (tq, tq), 0) // CHUNK
    kc = lax.broadcasted_iota(jnp.int32, (tq, tq), 1) // CHUNK
    diag_mask = kc <= qc
    m_ref[...] = jnp.full(m_ref.shape, NEG_INF, F32)
    acc_ref[...] = jnp.zeros(acc_ref.shape, F32)

    def scores(j, mask):
        r0 = pl.multiple_of(j * tq, tq)
        for hd in range(MLA_HEADS):
            ks = slice(hd * HEAD_PAD, (hd + 1) * HEAD_PAD)
            s = _dot_nt(q_ref[0, :, ks], k_ref[0, pl.ds(r0, tq), ks])
            if mask is not None:
                s = jnp.where(mask, s, NEG_INF)
            s_ref[hd, j] = s
            m_ref[hd] = jnp.maximum(m_ref[hd], jnp.maximum(s[:, :LANES], s[:, LANES:]))

    def pass1(j, c):
        scores(j, None)
        return c

    lax.fori_loop(0, i, pass1, 0)
    scores(i, diag_mask)
    for hd in range(MLA_HEADS):
        m_ref[hd] = jnp.broadcast_to(jnp.max(m_ref[hd], axis=-1, keepdims=True), (tq, LANES))

    def pass2(j, c):
        r0 = pl.multiple_of(j * tq, tq)
        for hd in range(MLA_HEADS):
            ks = slice(hd * HEAD_PAD, (hd + 1) * HEAD_PAD)
            s = s_ref[hd, j]
            mb = m_ref[hd]
            p = jnp.concatenate([jnp.exp2(s[:, :LANES] - mb), jnp.exp2(s[:, LANES:] - mb)], axis=-1).astype(BF16)
            acc_ref[hd] += _dot(p, v_ref[0, pl.ds(r0, tq), ks])
        return c

    lax.fori_loop(0, i + 1, pass2, 0)
    for hd in range(MLA_HEADS):
        acc = acc_ref[hd]
        o_ref[0, :, hd * V_HEAD:(hd + 1) * V_HEAD] = (acc[:, :V_HEAD] / acc[:, V_HEAD:V_HEAD + 1]).astype(BF16)


def _attn_causal(q, k, v, tq):
    b, l, _ = q.shape
    assert tq == 2 * LANES and l % tq == 0 and tq % CHUNK == 0
    full = lambda w: pl.BlockSpec((1, l, w), lambda bi, qi: (bi, 0, 0))
    return pl.pallas_call(
        functools.partial(_attn_causal_kernel, tq=tq),
        grid=(b, l // tq),
        in_specs=[pl.BlockSpec((1, tq, QK_PAD), lambda bi, qi: (bi, qi, 0)), full(QK_PAD), full(QK_PAD)],
        out_specs=pl.BlockSpec((1, tq, MLA_WIDTH), lambda bi, qi: (bi, qi, 0)),
        out_shape=jax.ShapeDtypeStruct((b, l, MLA_WIDTH), BF16),
        scratch_shapes=[pltpu.VMEM((MLA_HEADS, l // tq, tq, tq), F32),
                        pltpu.VMEM((MLA_HEADS, tq, LANES), F32),
                        pltpu.VMEM((MLA_HEADS, tq, HEAD_PAD), F32)],
        compiler_params=_cparams(("parallel", "arbitrary")),
        name="attn_prompt",
    )(q, k, v)


def _attn_past_kernel(q_ref, kp_ref, vp_ref, kn_ref, vn_ref, o_ref):
    for hd in range(MLA_HEADS):
        ks = slice(hd * HEAD_PAD, (hd + 1) * HEAD_PAD)
        qh = q_ref[0, :, ks]
        s_past = _dot_nt(qh, kp_ref[0, :, ks])
        s_new = _dot_nt(qh, kn_ref[0, :, ks])
        m = jnp.maximum(jnp.max(s_past, axis=-1, keepdims=True), jnp.max(s_new, axis=-1, keepdims=True))
        p_past = jnp.exp2(s_past - m)
        p_new = jnp.exp2(s_new - m)
        o = _dot(p_past.astype(BF16), vp_ref[0, :, ks]) + _dot(p_new.astype(BF16), vn_ref[0, :, ks])
        o_ref[0, :, hd * V_HEAD:(hd + 1) * V_HEAD] = (o[:, :V_HEAD] / o[:, V_HEAD:V_HEAD + 1]).astype(BF16)


def _attn_past(q, k_past, v_past, k_new, v_new):
    b, lq, _ = q.shape
    lp = k_past.shape[1]
    assert lp % CHUNK == 0 and lq <= CHUNK
    spec = lambda n, w: pl.BlockSpec((1, n, w), lambda bi: (bi, 0, 0))
    return pl.pallas_call(
        _attn_past_kernel,
        grid=(b,),
        in_specs=[spec(lq, QK_PAD), spec(lp, QK_PAD), spec(lp, QK_PAD), spec(lq, QK_PAD), spec(lq, QK_PAD)],
        out_specs=spec(lq, MLA_WIDTH),
        out_shape=jax.ShapeDtypeStruct((b, lq, MLA_WIDTH), BF16),
        compiler_params=_cparams(("parallel",)),
        name="attn_sample",
    )(q, k_past, v_past, k_new, v_new)


ROUTE_LANES = LANES


def _first_argmax(v, lane, width):
    vmax = jnp.max(v, axis=-1, keepdims=True)
    idx = jnp.min(jnp.where(v == vmax, lane, width), axis=-1, keepdims=True)
    return vmax, idx


def _merge_kernel(x_ref, bra_ref, o_ref, om_ref, gate_ref, cnt0_ref, w_omla_ref, w_omem_ref, w_out_ref, g_ffn_ref,
                  w_rt_ref, b_rt_ref, x1_ref, h2_ref, route_ref, route_t_ref, cnt_ref):
    first = (pl.program_id(0) == 0) & (pl.program_id(1) == 0)

    @pl.when(first)
    def _():
        cnt_ref[...] = cnt0_ref[...]

    br_b = _dot(o_ref[0], w_omla_ref[...])
    br_c = _dot(om_ref[0], w_omem_ref[...])
    g = gate_ref[0].astype(F32)
    merged = (g[:, :D_MODEL] * bra_ref[0].astype(F32) + g[:, D_MODEL:2 * D_MODEL] * br_b
              + g[:, 2 * D_MODEL:] * br_c)
    x1 = x_ref[0] + _dot(merged.astype(BF16), w_out_ref[...])
    x1_ref[0] = x1
    h2 = x1 * _rms_scale(x1, D_MODEL) * g_ffn_ref[...]
    h2_ref[0] = h2

    h2_hi = h2.astype(BF16)
    h2_lo = (h2 - h2_hi.astype(F32)).astype(BF16)
    hi_both = _dot(h2_hi, w_rt_ref[...])
    logits = (hi_both[:, :ROUTE_LANES] + hi_both[:, ROUTE_LANES:] + _dot(h2_lo, w_rt_ref[:, :ROUTE_LANES])
              + b_rt_ref[...])
    lane = lax.broadcasted_iota(jnp.int32, logits.shape, 1).astype(F32)
    ninf = jnp.float32(-jnp.inf)
    lg = jnp.where(lane < N_EXPERT_GROUPS, logits, ninf)
    lg_max, grp = _first_argmax(lg, lane, float(ROUTE_LANES))
    p_top = 1.0 / jnp.sum(jnp.exp(lg - lg_max), axis=-1, keepdims=True)
    lo = N_EXPERT_GROUPS + grp * EXPERTS_PER_GROUP
    in_grp = (lane >= lo) & (lane < lo + EXPERTS_PER_GROUP)
    le = jnp.where(in_grp, logits, ninf)
    v1, i1 = _first_argmax(le, lane, float(ROUTE_LANES))
    v2, i2 = _first_argmax(jnp.where(lane == i1, ninf, le), lane, float(ROUTE_LANES))
    e2 = jnp.exp(v2 - v1)
    w1 = p_top / (1.0 + e2)
    w2 = p_top * e2 / (1.0 + e2)
    elane = lane + N_EXPERT_GROUPS
    oh1 = elane == i1
    oh2 = elane == i2
    onehot = jnp.where(oh1 | oh2, 1.0, 0.0)
    tl = onehot.shape[0]
    tri = jnp.where(lax.broadcasted_iota(jnp.int32, (tl, tl), 0) > lax.broadcasted_iota(jnp.int32, (tl, tl), 1),
                    1.0, 0.0).astype(BF16)
    before = _dot(tri, onehot.astype(BF16)) + cnt_ref[...]
    r1 = jnp.sum(jnp.where(oh1, before, 0.0), axis=-1, keepdims=True)
    r2 = jnp.sum(jnp.where(oh2, before, 0.0), axis=-1, keepdims=True)
    cnt_ref[...] += jnp.sum(onehot, axis=0, keepdims=True)
    cols = (i1 - N_EXPERT_GROUPS, i2 - N_EXPERT_GROUPS, w1, w2, r1, r2)
    route = jnp.zeros_like(logits)
    for k, col in enumerate(cols):
        route = jnp.where(lane == k, col, route)
    route_ref[0] = route
    route_t_ref[0] = jnp.transpose(route)[:SUBLANES]


def _merge(x, bra, o, om, gates, cnt0, wts, tl):
    b, l, _ = x.shape
    tok = lambda w: pl.BlockSpec((1, tl, w), lambda bi, li: (bi, li, 0))
    in_specs = [tok(D_MODEL), tok(D_MODEL), tok(MLA_WIDTH), tok(MEM_WIDTH), tok(N_BRANCHES * D_MODEL),
                _const_spec((1, ROUTE_LANES))] + [_const_spec(w.shape) for w in wts]
    return pl.pallas_call(
        _merge_kernel,
        grid=(b, l // tl),
        in_specs=in_specs,
        out_specs=(tok(D_MODEL), tok(D_MODEL), tok(ROUTE_LANES),
                   pl.BlockSpec((1, SUBLANES, tl), lambda bi, li: (bi * (l // tl) + li, 0, 0)),
                   pl.BlockSpec((1, ROUTE_LANES), lambda bi, li: (0, 0))),
        out_shape=(jax.ShapeDtypeStruct((b, l, D_MODEL), F32), jax.ShapeDtypeStruct((b, l, D_MODEL), F32),
                   jax.ShapeDtypeStruct((b, l, ROUTE_LANES), F32),
                   jax.ShapeDtypeStruct((b * l // tl, SUBLANES, tl), F32),
                   jax.ShapeDtypeStruct((1, ROUTE_LANES), F32)),
        compiler_params=_cparams(("arbitrary", "arbitrary")),
        name="merge",
    )(x, bra, o, om, gates, cnt0, *wts)


MOE_TILE = 256
MOE_DMA_TILE = 512


def _row_copy(src_ref, src_row, dst_ref, dst_row, sem):
    return pltpu.make_async_copy(src_ref.at[pl.ds(src_row, 1)], dst_ref.at[pl.ds(dst_row, 1)], sem)


def _tile_copy(src_ref, dst_hbm, tile, sem):
    return pltpu.make_async_copy(src_ref, dst_hbm.at[pl.ds(tile * MOE_TILE, MOE_TILE)], sem)


def _dma_cparams():
    return pltpu.CompilerParams(dimension_semantics=("arbitrary",), vmem_limit_bytes=VMEM_LIMIT,
                                disable_bounds_checks=True)


def _dispatch_kernel(last_tile_ref, n_tiles_ref, slot_ref, h2p_ref, h2s_ref, hs_hbm, zero_ref, sem, zsem,
                     *, steps_p, grid_tiles):
    i = pl.program_id(0)
    td = slot_ref.shape[2]

    @pl.when(i == 0)
    def _():
        zero_ref[...] = jnp.zeros(zero_ref.shape, zero_ref.dtype)
        for wait in (False, True):
            def pad_tile(e, c):
                @pl.when(last_tile_ref[e] >= 0)
                def _():
                    cp = _tile_copy(zero_ref, hs_hbm, last_tile_ref[e], zsem)
                    cp.wait() if wait else cp.start()
                return c

            def tail_tile(t, c):
                cp = _tile_copy(zero_ref, hs_hbm, t, zsem)
                cp.wait() if wait else cp.start()
                return c

            lax.fori_loop(0, N_EXPERTS, pad_tile, 0)
            lax.fori_loop(n_tiles_ref[0], grid_tiles, tail_tile, 0)

    def scatter(h2_ref):
        def start(r, c):
            for k in range(2):
                _row_copy(h2_ref, r, hs_hbm, slot_ref[0, k, r], sem).start()
            return c

        def wait(r, c):
            for k in range(2):
                _row_copy(h2_ref, r, hs_hbm, slot_ref[0, k, r], sem).wait()
            return c

        lax.fori_loop(0, td, start, 0, unroll=8)
        lax.fori_loop(0, td, wait, 0, unroll=8)

    @pl.when(i < steps_p)
    def _():
        scatter(h2p_ref)

    @pl.when(i >= steps_p)
    def _():
        scatter(h2s_ref)


def _dispatch(last_tile, n_tiles, slots, h2_p, h2_s, grid_tiles, td):
    steps_p, steps_s = h2_p.shape[0] // td, h2_s.shape[0] // td
    grid_spec = pltpu.PrefetchScalarGridSpec(
        num_scalar_prefetch=2,
        grid=(steps_p + steps_s,),
        in_specs=[pl.BlockSpec((1, 2, td), lambda i, lt, nt: (i, 0, 0), memory_space=pltpu.SMEM),
                  pl.BlockSpec((td, D_MODEL), lambda i, lt, nt: (jnp.minimum(i, steps_p - 1), 0)),
                  pl.BlockSpec((td, D_MODEL), lambda i, lt, nt: (jnp.maximum(i - steps_p, 0), 0))],
        out_specs=pl.BlockSpec(memory_space=pl.ANY),
        scratch_shapes=[pltpu.VMEM((MOE_TILE, D_MODEL), F32), pltpu.SemaphoreType.DMA, pltpu.SemaphoreType.DMA],
    )
    return pl.pallas_call(
        functools.partial(_dispatch_kernel, steps_p=steps_p, grid_tiles=grid_tiles),
        grid_spec=grid_spec,
        out_shape=jax.ShapeDtypeStruct((grid_tiles * MOE_TILE, D_MODEL), F32),
        compiler_params=_dma_cparams(),
        name="dispatch",
    )(last_tile, n_tiles, slots, h2_p, h2_s)


def _experts_kernel(tile_expert_ref, n_tiles_ref, hs_ref, w1_ref, w3_ref, w2_ref, y_ref, w13_bf_ref, w2_bf_ref):
    t = pl.program_id(0)
    used = t < n_tiles_ref[0]
    new_expert = (t == 0) | (tile_expert_ref[t] != tile_expert_ref[jnp.maximum(t - 1, 0)])

    @pl.when(used & new_expert)
    def _():
        w13_bf_ref[:, :EXPERT_FF] = w1_ref[0].astype(BF16)
        w13_bf_ref[:, EXPERT_FF:] = w3_ref[0].astype(BF16)
        w2_bf_ref[...] = w2_ref[0].astype(BF16)

    @pl.when(used)
    def _():
        a = _dot(hs_ref[...].astype(BF16), w13_bf_ref[...])
        gate = a[:, :EXPERT_FF]
        hid = gate * _sigmoid(gate) * a[:, EXPERT_FF:]
        y_ref[...] = _dot(hid.astype(BF16), w2_bf_ref[...])

    @pl.when(jnp.logical_not(used))
    def _():
        y_ref[...] = jnp.zeros(y_ref.shape, y_ref.dtype)


def _experts(tile_expert, n_tiles, hs, w_e1, w_e3, w_e2):
    s = hs.shape[0]
    w_in_spec = pl.BlockSpec((1, D_MODEL, EXPERT_FF), lambda t, te, nt: (te[t], 0, 0))
    grid_spec = pltpu.PrefetchScalarGridSpec(
        num_scalar_prefetch=2,
        grid=(s // MOE_TILE,),
        in_specs=[pl.BlockSpec((MOE_TILE, D_MODEL), lambda t, te, nt: (t, 0)), w_in_spec, w_in_spec,
                  pl.BlockSpec((1, EXPERT_FF, D_MODEL), lambda t, te, nt: (te[t], 0, 0))],
        out_specs=pl.BlockSpec((MOE_TILE, D_MODEL), lambda t, te, nt: (t, 0)),
        scratch_shapes=[pltpu.VMEM((D_MODEL, 2 * EXPERT_FF), BF16), pltpu.VMEM((EXPERT_FF, D_MODEL), BF16)],
    )
    return pl.pallas_call(
        _experts_kernel,
        grid_spec=grid_spec,
        out_shape=jax.ShapeDtypeStruct((s, D_MODEL), F32),
        compiler_params=_cparams(("arbitrary",)),
        name="experts",
    )(tile_expert, n_tiles, hs, w_e1, w_e3, w_e2)


def _combine_kernel(slot_ref, x1_ref, route_ref, ye_hbm, y_ref, rows_ref, sem):
    tc = slot_ref.shape[2]

    def start(r, c):
        for k in range(2):
            _row_copy(ye_hbm, slot_ref[0, k, r], rows_ref.at[k], r, sem).start()
        return c

    def wait(r, c):
        for k in range(2):
            _row_copy(ye_hbm, slot_ref[0, k, r], rows_ref.at[k], r, sem).wait()
        return c

    lax.fori_loop(0, tc, start, 0, unroll=8)
    lax.fori_loop(0, tc, wait, 0, unroll=8)
    route = route_ref[...]
    y = x1_ref[...]
    for k in range(2):
        gate = route[:, 2 + k:3 + k]
        y = y + gate * rows_ref[k]
    y_ref[...] = y


def _combine(slots, x1, route, ye, tc):
    n = x1.shape[0]
    return pl.pallas_call(
        _combine_kernel,
        grid=(n // tc,),
        in_specs=[pl.BlockSpec((1, 2, tc), lambda i: (i, 0, 0), memory_space=pltpu.SMEM),
                  pl.BlockSpec((tc, D_MODEL), lambda i: (i, 0)),
                  pl.BlockSpec((tc, ROUTE_LANES), lambda i: (i, 0)),
                  pl.BlockSpec(memory_space=pl.ANY)],
        out_specs=pl.BlockSpec((tc, D_MODEL), lambda i: (i, 0)),
        out_shape=jax.ShapeDtypeStruct((n, D_MODEL), F32),
        scratch_shapes=[pltpu.VMEM((2, tc, D_MODEL), F32), pltpu.SemaphoreType.DMA],
        compiler_params=_dma_cparams(),
        name="combine",
    )(slots, x1, route, ye)


def _rope_tables(pos, g, scale):
    half = QK_ROPE // 2
    inv = ROPE_BASE ** (-jnp.arange(half, dtype=F32) / half)
    ang = pos.astype(F32)[:, None] * inv[None, :]
    cos, sin = jnp.cos(ang), jnp.sin(ang)
    n = pos.shape[0]
    g1, g2 = g[QK_NOPE:QK_NOPE + half], g[QK_NOPE + half:QK_HEAD]
    pad = jnp.zeros((n, HEAD_PAD - QK_HEAD), F32)
    a = jnp.concatenate([jnp.broadcast_to(g[:QK_NOPE], (n, QK_NOPE)), g1 * cos, g2 * cos, pad], axis=-1)
    b = jnp.concatenate([jnp.zeros((n, QK_NOPE), F32), -g2 * sin, g1 * sin, pad], axis=-1)
    return a * scale, b * scale


def _pad_heads(w, per_head, keep):
    k = w.shape[0]
    w = w.reshape(k, MLA_HEADS, per_head)[:, :, :keep]
    return jnp.pad(w, ((0, 0), (0, 0), (0, HEAD_PAD - keep))).reshape(k, QK_PAD)


def _swap_rope_cols(w96):
    k = w96.shape[0]
    w = w96.reshape(k, MLA_HEADS, QK_HEAD)
    half = QK_ROPE // 2
    sw = jnp.concatenate([jnp.zeros((k, MLA_HEADS, QK_NOPE), w.dtype), w[:, :, QK_NOPE + half:],
                          w[:, :, QK_NOPE:QK_NOPE + half]], axis=-1)
    return sw.reshape(k, MLA_HEADS * QK_HEAD)


def _block_diag(w, rows_per_group, cols_per_group):
    gb = SSM_GROUPS // SSM_BLOCKS
    w = w.reshape(SSM_BLOCKS, gb, rows_per_group, cols_per_group)
    eye = jnp.eye(gb, dtype=w.dtype)
    out = jnp.einsum('bgrc,gh->bgrhc', w, eye)
    return out.reshape(SSM_BLOCKS, gb * rows_per_group, gb * cols_per_group)


def _layer(x, pos, h0_re, h0_im, mk, mv, past, cnt0, p, tl):
    b, l, _ = x.shape
    scale = math.log2(math.e) / math.sqrt(QK_HEAD)
    aq, bq = _rope_tables(pos, p["g_qn"], scale)
    ak, bk = _rope_tables(pos, p["g_kn"], 1.0)
    inproj_wts = (p["g_attn"], p["w_u"], p["w_q"], p["w_kv"], p["w_pe"], p["w_qm"], p["w_g"], p["g_qlat"],
                  p["wq_pad"], p["wq_swap"], p["g_kvlat"], p["wk_pad"], p["wv"], p["vone"], p["g_mqn"])
    u, q, k, v, c_kv, k_pe, om, gates = _inproj(x, mk, mv, (aq, bq, ak, bk),
                                                inproj_wts, tl)
    a_re = jnp.broadcast_to(p["ab_re"], (b, SSM_STATES))
    a_im = jnp.broadcast_to(p["ab_im"], (b, SSM_STATES))
    bra, h_re, h_im = _s5(u, h0_re, h0_im, a_re, a_im, p["bmat"], p["cmat"], p["ssm_d"], p["w_glu"], S5_STEPS)
    if past is None:
        o = _attn_causal(q, k, v, 256)
    else:
        past_ckv, past_kpe = past
        lp = past_ckv.shape[1]
        akp, bkp = _rope_tables(jnp.arange(lp), p["g_kn"], 1.0)
        k_past, v_past = _kvexp(past_ckv, past_kpe, akp, bkp, p["wk_pad"], p["wv"], p["vone"], 512)
        o = _attn_past(q, k_past, v_past, k, v)
    merge_wts = (p["w_o_mla"], p["w_o_mem"], p["w_out"], p["g_ffn"], p["w_rt"], p["b_rt"])
    x1, h2, route, route_t, cnt = _merge(x, bra, o, om, gates, cnt0, merge_wts, tl)
    n = b * l
    moe_in = (x1.reshape(n, D_MODEL), h2.reshape(n, D_MODEL), route.reshape(n, ROUTE_LANES), route_t, cnt)
    return moe_in, c_kv, k_pe, h_re, h_im


def _slots(route_t, offsets, tile):
    expert = route_t[:, 0:2, :].astype(jnp.int32)
    rank = route_t[:, 4:6, :].astype(jnp.int32)
    onehot = expert[..., None] == jnp.arange(N_EXPERTS, dtype=jnp.int32)
    slot = jnp.sum(jnp.where(onehot, offsets, 0), axis=-1) + rank
    if slot.shape[2] != tile:
        slot = slot.transpose(1, 0, 2).reshape(2, -1, tile).transpose(1, 0, 2)
    return slot


def kernel(x_prompt, x_sample, cache_mla_ckv, cache_mla_kpe, cache_ssm_re, cache_ssm_im, cache_mem_k, cache_mem_v, mem_prompt, g_attn, w_in, ssm_a_re, ssm_a_im, ssm_log_dt, ssm_b_re, ssm_b_im, ssm_c_re, ssm_c_im, ssm_d, w_glu, g_qlat, w_uq, g_kvlat, w_ukv, g_qn, g_kn, w_o_mla, g_mem, w_mem_kv, g_mqn, g_mkn, w_o_mem, w_out, g_ffn, w_rg, b_rg, w_re, b_re, w_e1, w_e3, w_e2):
    assert g_attn.shape[0] == 1, "single-layer step"
    bp, lp, _ = x_prompt.shape
    bs, ls, _ = x_sample.shape
    past_len = cache_mla_ckv.shape[2]
    lyr = 0

    o1 = SSM_WIDTH
    o2 = o1 + Q_LORA
    o3 = o2 + KV_LORA
    o4 = o3 + QK_ROPE
    o5 = o4 + MEM_WIDTH
    w_in_bf = w_in[lyr].astype(BF16)
    row = lambda a: a.reshape(1, -1)
    p = {
        "g_attn": row(g_attn[lyr]), "w_u": w_in_bf[:, :o1], "w_q": w_in_bf[:, o1:o2], "w_kv": w_in_bf[:, o2:o3],
        "w_qm": w_in_bf[:, o4:o5], "w_g": w_in_bf[:, o5:],
        "g_qlat": row(g_qlat[lyr]), "g_kvlat": row(g_kvlat[lyr]), "g_mqn": row(g_mqn[lyr]),
        "g_qn": g_qn[lyr], "g_kn": g_kn[lyr], "g_ffn": row(g_ffn[lyr]),
        "ssm_d": ssm_d[lyr], "w_glu": w_glu[lyr].astype(BF16),
        "w_o_mla": w_o_mla[lyr].astype(BF16), "w_o_mem": w_o_mem[lyr].astype(BF16), "w_out": w_out[lyr].astype(BF16),
    }
    wuq = w_uq[lyr]
    p["wq_pad"] = _pad_heads(wuq, QK_HEAD, QK_HEAD).astype(BF16)
    p["wq_swap"] = _pad_heads(_swap_rope_cols(wuq), QK_HEAD, QK_HEAD).astype(BF16)
    wukv = w_ukv[lyr]
    p["wk_pad"] = _pad_heads(wukv, QK_NOPE + V_HEAD, QK_NOPE).astype(BF16)
    p["wv"] = _pad_heads(jnp.roll(wukv.reshape(KV_LORA, MLA_HEADS, QK_NOPE + V_HEAD), -QK_NOPE, axis=-1)
                         .reshape(KV_LORA, -1), QK_NOPE + V_HEAD, V_HEAD).astype(BF16)
    p["vone"] = jnp.asarray((np.arange(QK_PAD) % HEAD_PAD == V_HEAD).astype(np.float32).reshape(1, QK_PAD))
    half = QK_ROPE // 2
    w_pe = w_in_bf[:, o3:o4]
    col_pad = lambda w: jnp.pad(w, ((0, 0), (QK_NOPE, HEAD_PAD - QK_HEAD)))
    p["w_pe"] = jnp.concatenate([col_pad(w_pe), col_pad(jnp.concatenate([w_pe[:, half:], w_pe[:, :half]], axis=-1))],
                                axis=-1)

    ab_re, ab_im, f_re, f_im = _zoh(ssm_a_re[lyr], ssm_a_im[lyr], ssm_log_dt[lyr])
    b_re_, b_im_ = ssm_b_re[lyr], ssm_b_im[lyr]
    bb_re = f_re[..., None] * b_re_ - f_im[..., None] * b_im_
    bb_im = f_re[..., None] * b_im_ + f_im[..., None] * b_re_
    to_cp = lambda w: jnp.swapaxes(w, 1, 2)
    p["bmat"] = jnp.concatenate([_block_diag(to_cp(bb_re), SSM_GROUP_CH, SSM_STATE),
                                 _block_diag(to_cp(bb_im), SSM_GROUP_CH, SSM_STATE)], axis=-1).astype(BF16)
    to_pc = lambda w: jnp.swapaxes(w, 1, 2)
    p["cmat"] = jnp.stack([_block_diag(to_pc(ssm_c_re[lyr]), SSM_STATE, SSM_GROUP_CH),
                           _block_diag(to_pc(ssm_c_im[lyr]), SSM_STATE, SSM_GROUP_CH)]).astype(BF16)
    p["ab_re"] = ab_re.reshape(1, SSM_STATES)
    p["ab_im"] = ab_im.reshape(1, SSM_STATES)

    w_rt = jnp.concatenate([w_rg[lyr], w_re[lyr]], axis=-1)
    w_rt = jnp.pad(w_rt, ((0, 0), (0, ROUTE_LANES - w_rt.shape[1])))
    w_rt_hi = w_rt.astype(BF16)
    p["w_rt"] = jnp.concatenate([w_rt_hi, (w_rt - w_rt_hi.astype(F32)).astype(BF16)], axis=-1)
    b_rt = jnp.concatenate([b_rg[lyr], b_re[lyr].reshape(-1)])
    p["b_rt"] = jnp.pad(b_rt, (0, ROUTE_LANES - b_rt.shape[0])).reshape(1, ROUTE_LANES)

    mk, mv, mk_bf, mv_bf = _memkv(mem_prompt.reshape(-1, D_MODEL), g_mem[lyr], w_mem_kv[lyr].astype(BF16),
                                  g_mkn[lyr])
    m_tok = mem_prompt.shape[1]
    mk3, mv3 = mk_bf.reshape(bp, m_tok, MEM_WIDTH), mv_bf.reshape(bp, m_tok, MEM_WIDTH)
    zeros = jnp.zeros((bp, SSM_STATES), F32)
    cnt0 = jnp.zeros((1, ROUTE_LANES), F32)
    (x1_p, h2p_p, route_p, route_t_p, cnt_p), ckv_p, kpe_p, sre_p, sim_p = _layer(
        x_prompt, jnp.arange(lp), zeros, zeros, mk3, mv3, None, cnt0, p, tl=PROMPT_TILE)

    (x1_s, h2p_s, route_s, route_t_s, cnt), ckv_s, kpe_s, sre_s, sim_s = _layer(
        x_sample, past_len + jnp.arange(ls), cache_ssm_re[lyr].reshape(bs, SSM_STATES),
        cache_ssm_im[lyr].reshape(bs, SSM_STATES), cache_mem_k[lyr].reshape(bs, -1, MEM_WIDTH).astype(BF16),
        cache_mem_v[lyr].reshape(bs, -1, MEM_WIDTH).astype(BF16), (cache_mla_ckv[lyr], cache_mla_kpe[lyr]),
        cnt_p, p, tl=ls)

    n_p, n_s = x1_p.shape[0], x1_s.shape[0]
    counts = cnt[0, :N_EXPERTS].astype(jnp.int32)
    padded = (counts + MOE_TILE - 1) // MOE_TILE * MOE_TILE
    ends = jnp.cumsum(padded)
    offsets = ends - padded
    max_tiles = 2 * (n_p + n_s) // MOE_TILE + N_EXPERTS
    tile_start = jnp.arange(max_tiles, dtype=jnp.int32) * MOE_TILE
    tile_expert = jnp.sum((ends[None, :] <= tile_start[:, None]).astype(jnp.int32), axis=1)
    tile_expert = jnp.minimum(tile_expert, N_EXPERTS - 1)
    n_tiles = (ends[-1:] // MOE_TILE).astype(jnp.int32)
    slots_p = _slots(route_t_p, offsets, MOE_DMA_TILE)
    slots_s = _slots(route_t_s, offsets, MOE_DMA_TILE)
    last_tile = jnp.where(padded > 0, ends // MOE_TILE - 1, -1).astype(jnp.int32)
    hs = _dispatch(last_tile, n_tiles, jnp.concatenate([slots_p, slots_s], axis=0), h2p_p, h2p_s, max_tiles,
                   MOE_DMA_TILE)
    ye = _experts(tile_expert, n_tiles, hs, w_e1[lyr], w_e3[lyr], w_e2[lyr])
    yp = _combine(slots_p, x1_p, route_p, ye, MOE_DMA_TILE).reshape(bp, lp, D_MODEL)
    ys = _combine(slots_s, x1_s, route_s, ye, MOE_DMA_TILE).reshape(bs, ls, D_MODEL)

    st = lambda a, bsz: a.reshape(1, bsz, SSM_GROUPS, SSM_STATE)
    mem_shape = (1, bp, m_tok, MEM_HEADS, MEM_HEAD)
    return (yp, ys, ckv_p[None], kpe_p[None], st(sre_p, bp), st(sim_p, bp), mk.reshape(mem_shape),
            mv.reshape(mem_shape), ckv_s[None], kpe_s[None], st(sre_s, bs), st(sim_s, bs))
```

```python
import functools
import math

import jax
import jax.numpy as jnp
import numpy as np
from jax import lax
from jax.experimental import pallas as pl
from jax.experimental.pallas import tpu as pltpu

D_MODEL = 1024
CHUNK = 64
RMS_EPS = 1e-6
SSM_GROUPS = 32
SSM_GROUP_CH = 16
SSM_WIDTH = SSM_GROUPS * SSM_GROUP_CH
SSM_STATE = 64
SSM_STATES = SSM_GROUPS * SSM_STATE
SSM_BLOCKS = 2
MLA_HEADS = 8
QK_NOPE = 64
QK_ROPE = 32
QK_HEAD = QK_NOPE + QK_ROPE
V_HEAD = 64
Q_LORA = 384
KV_LORA = 256
ROPE_BASE = 10000.0
MLA_WIDTH = MLA_HEADS * V_HEAD
MEM_HEADS = 4
MEM_HEAD = 128
MEM_WIDTH = MEM_HEADS * MEM_HEAD
N_BRANCHES = 3
N_EXPERT_GROUPS = 4
EXPERTS_PER_GROUP = 8
N_EXPERTS = N_EXPERT_GROUPS * EXPERTS_PER_GROUP
EXPERT_FF = 256

LANES = 128
SUBLANES = 8
HEAD_PAD = LANES
QK_PAD = MLA_HEADS * HEAD_PAD
VMEM_LIMIT = 56 * 1024 * 1024
PROMPT_TILE = 512

BF16 = jnp.bfloat16
F32 = jnp.float32
NEG_INF = -1e30


def _cparams(sem):
    return pltpu.CompilerParams(dimension_semantics=sem, vmem_limit_bytes=VMEM_LIMIT)


def _const_spec(shape):
    nd = len(shape)
    return pl.BlockSpec(shape, lambda *_: (0,) * nd, pipeline_mode=pl.Buffered(1))


def _rms_scale(xf, width):
    return lax.rsqrt(jnp.sum(xf * xf, axis=-1, keepdims=True) * (1.0 / width) + RMS_EPS)


def _sigmoid(x):
    return 0.5 * jnp.tanh(0.5 * x) + 0.5


def _dot(a, b):
    return jnp.dot(a, b, preferred_element_type=F32)


def _dot_nt(a, b):
    return lax.dot_general(a, b, (((1,), (1,)), ((), ())), preferred_element_type=F32)


def _zoh_kernel(lr_ref, li_ref, ldt_ref, abr_ref, abi_ref, fr_ref, fi_ref):
    lr = lr_ref[...]
    li = li_ref[...]
    dt = jnp.exp(ldt_ref[...])
    mag = jnp.exp(lr * dt)
    ab_re = mag * jnp.cos(li * dt)
    ab_im = mag * jnp.sin(li * dt)
    den = lr * lr + li * li
    nr = ab_re - 1.0
    ni = ab_im
    abr_ref[...] = ab_re
    abi_ref[...] = ab_im
    fr_ref[...] = (nr * lr + ni * li) / den
    fi_ref[...] = (ni * lr - nr * li) / den


def _zoh(a_re, a_im, log_dt):
    shp = jax.ShapeDtypeStruct((SSM_GROUPS, SSM_STATE), F32)
    return pl.pallas_call(_zoh_kernel, out_shape=(shp, shp, shp, shp), name="zoh")(
        a_re, a_im, log_dt.reshape(SSM_GROUPS, 1))


def _expand_kv(ckv_bf, kpe_p, kpe_s, wk_ref, wv_ref, vone_ref, ak, bk):
    k_nope = _dot(ckv_bf, wk_ref[...])
    v = _dot(ckv_bf, wv_ref[...]) + vone_ref[...]
    ss_pe = jnp.sum(kpe_p * kpe_p, axis=-1, keepdims=True)
    rot = kpe_p * ak + kpe_s * bk
    heads = []
    for h in range(MLA_HEADS):
        kh = k_nope[:, h * HEAD_PAD:(h + 1) * HEAD_PAD]
        ss = jnp.sum(kh * kh, axis=-1, keepdims=True) + ss_pe
        rs = lax.rsqrt(ss * (1.0 / QK_HEAD) + RMS_EPS)
        heads.append((rs * (kh * ak + rot)).astype(BF16))
    return jnp.concatenate(heads, axis=-1), v.astype(BF16)


def _memkv_kernel(mem_ref, g_ref, w_ref, gk_ref, k_ref, v_ref, kb_ref, vb_ref):
    x = mem_ref[...]
    h = (x * _rms_scale(x, D_MODEL) * g_ref[...]).astype(BF16)
    kv = _dot(h, w_ref[...])
    gk = gk_ref[...]
    for hd in range(MEM_HEADS):
        sl = slice(hd * MEM_HEAD, (hd + 1) * MEM_HEAD)
        kh = kv[:, sl]
        kh = kh * _rms_scale(kh, MEM_HEAD) * gk
        k_ref[:, sl] = kh
        kb_ref[:, sl] = kh.astype(BF16)
    v_ref[...] = kv[:, MEM_WIDTH:]
    vb_ref[...] = kv[:, MEM_WIDTH:].astype(BF16)


def _memkv(mem2d, g_mem, w_mem_kv_bf, g_mkn):
    n = mem2d.shape[0]
    tm = 256
    out = jax.ShapeDtypeStruct((n, MEM_WIDTH), F32)
    out_bf = jax.ShapeDtypeStruct((n, MEM_WIDTH), BF16)
    blk = pl.BlockSpec((tm, MEM_WIDTH), lambda i: (i, 0))
    return pl.pallas_call(
        _memkv_kernel,
        grid=(n // tm,),
        in_specs=[pl.BlockSpec((tm, D_MODEL), lambda i: (i, 0)),
                  _const_spec((1, D_MODEL)),
                  _const_spec((D_MODEL, 2 * MEM_WIDTH)),
                  _const_spec((1, MEM_HEAD))],
        out_specs=(blk, blk, blk, blk),
        out_shape=(out, out, out_bf, out_bf),
        compiler_params=_cparams(("parallel",)),
        name="memkv",
    )(mem2d, g_mem.reshape(1, D_MODEL), w_mem_kv_bf, g_mkn.reshape(1, MEM_HEAD))


def _inproj_kernel(x_ref, mk_ref, mv_ref, aq_ref, bq_ref, ak_ref, bk_ref,
                   g_attn_ref, w_u_ref, w_q_ref, w_kv_ref, w_pe_ref, w_qm_ref, w_g_ref,
                   g_qlat_ref, wq_ref, wqs_ref, g_kvlat_ref, wk_ref, wv_ref, vone_ref, g_mqn_ref,
                   u_ref, q_ref, k_ref, v_ref, ckv_ref, kpe_ref, om_ref, gate_ref):
    x = x_ref[0]
    h = (x * _rms_scale(x, D_MODEL) * g_attn_ref[...]).astype(BF16)

    u_ref[0] = _dot(h, w_u_ref[...])
    gate_ref[0] = _sigmoid(_dot(h, w_g_ref[...])).astype(BF16)

    q_lat = _dot(h, w_q_ref[...])
    qn = (q_lat * _rms_scale(q_lat, Q_LORA) * g_qlat_ref[...]).astype(BF16)
    q_up = _dot(qn, wq_ref[...])
    q_sw = _dot(qn, wqs_ref[...])
    aq = aq_ref[...]
    bq = bq_ref[...]
    for hd in range(MLA_HEADS):
        sl = slice(hd * HEAD_PAD, (hd + 1) * HEAD_PAD)
        qh = q_up[:, sl]
        rs = _rms_scale(qh, QK_HEAD)
        q_ref[0, :, sl] = (rs * (qh * aq + q_sw[:, sl] * bq)).astype(BF16)

    kv_lat = _dot(h, w_kv_ref[...])
    c_kv = kv_lat * _rms_scale(kv_lat, KV_LORA) * g_kvlat_ref[...]
    ckv_ref[0] = c_kv
    kpe_p = _dot(h, w_pe_ref[:, :HEAD_PAD])
    kpe_s = _dot(h, w_pe_ref[:, HEAD_PAD:])
    kpe_ref[0] = kpe_p[:, QK_NOPE:QK_HEAD]
    k_all, v_all = _expand_kv(c_kv.astype(BF16), kpe_p, kpe_s, wk_ref, wv_ref, vone_ref, ak_ref[...], bk_ref[...])
    k_ref[0] = k_all
    v_ref[0] = v_all

    q_mem = _dot(h, w_qm_ref[...])
    gq = g_mqn_ref[...] * (1.0 / math.sqrt(MEM_HEAD))
    for hd in range(MEM_HEADS):
        sl = slice(hd * MEM_HEAD, (hd + 1) * MEM_HEAD)
        qh = q_mem[:, sl]
        qh = (qh * _rms_scale(qh, MEM_HEAD) * gq).astype(BF16)
        s = _dot_nt(qh, mk_ref[0, :, sl])
        p = jnp.exp(s - jnp.max(s, axis=-1, keepdims=True))
        o = _dot(p.astype(BF16), mv_ref[0, :, sl])
        om_ref[0, :, sl] = (o / jnp.sum(p, axis=-1, keepdims=True)).astype(BF16)


def _inproj(x, mk_bf, mv_bf, tabs, wts, tl):
    b, l, _ = x.shape
    m = mk_bf.shape[1]
    aq, bq, ak, bk = tabs
    tok = lambda w: pl.BlockSpec((1, tl, w), lambda bi, li: (bi, li, 0))
    tab = pl.BlockSpec((tl, HEAD_PAD), lambda bi, li: (li, 0))
    memspec = pl.BlockSpec((1, m, MEM_WIDTH), lambda bi, li: (bi, 0, 0))
    in_specs = [tok(D_MODEL), memspec, memspec, tab, tab, tab, tab] + [_const_spec(w.shape) for w in wts]
    out_shape = (
        jax.ShapeDtypeStruct((b, l, SSM_WIDTH), F32),
        jax.ShapeDtypeStruct((b, l, QK_PAD), BF16),
        jax.ShapeDtypeStruct((b, l, QK_PAD), BF16),
        jax.ShapeDtypeStruct((b, l, QK_PAD), BF16),
        jax.ShapeDtypeStruct((b, l, KV_LORA), F32),
        jax.ShapeDtypeStruct((b, l, QK_ROPE), F32),
        jax.ShapeDtypeStruct((b, l, MEM_WIDTH), BF16),
        jax.ShapeDtypeStruct((b, l, N_BRANCHES * D_MODEL), BF16),
    )
    out_specs = (
        tok(SSM_WIDTH), tok(QK_PAD), tok(QK_PAD), tok(QK_PAD), tok(KV_LORA), tok(QK_ROPE), tok(MEM_WIDTH),
        tok(N_BRANCHES * D_MODEL),
    )
    return pl.pallas_call(
        _inproj_kernel,
        grid=(b, l // tl),
        in_specs=in_specs,
        out_specs=out_specs,
        out_shape=out_shape,
        compiler_params=_cparams(("parallel", "parallel")),
        name="inproj",
    )(x, mk_bf, mv_bf, aq, bq, ak, bk, *wts)


def _kvexp_kernel(ckv_ref, kpe_ref, ak_ref, bk_ref, wk_ref, wv_ref, vone_ref, k_ref, v_ref):
    kpe = kpe_ref[0]
    half = QK_ROPE // 2
    lo = jnp.zeros((kpe.shape[0], QK_NOPE), F32)
    hi = jnp.zeros((kpe.shape[0], HEAD_PAD - QK_HEAD), F32)
    kpe_p = jnp.concatenate([lo, kpe, hi], axis=-1)
    kpe_s = jnp.concatenate([lo, kpe[:, half:], kpe[:, :half], hi], axis=-1)
    k_all, v_all = _expand_kv(ckv_ref[0].astype(BF16), kpe_p, kpe_s, wk_ref, wv_ref, vone_ref,
                              ak_ref[...], bk_ref[...])
    k_ref[0] = k_all
    v_ref[0] = v_all


def _kvexp(ckv, kpe, ak, bk, wk, wv, vone, tl):
    b, l, _ = ckv.shape
    tok = lambda w: pl.BlockSpec((1, tl, w), lambda bi, li: (bi, li, 0))
    tab = pl.BlockSpec((tl, HEAD_PAD), lambda bi, li: (li, 0))
    return pl.pallas_call(
        _kvexp_kernel,
        grid=(b, l // tl),
        in_specs=[tok(KV_LORA), tok(QK_ROPE), tab, tab] + [_const_spec(w.shape) for w in (wk, wv, vone)],
        out_specs=(tok(QK_PAD), tok(QK_PAD)),
        out_shape=(jax.ShapeDtypeStruct((b, l, QK_PAD), BF16), jax.ShapeDtypeStruct((b, l, QK_PAD), BF16)),
        compiler_params=_cparams(("parallel", "parallel")),
        name="kvexp",
    )(ckv, kpe, ak, bk, wk, wv, vone)


SCAN_LANES = 1024
S5_STEPS = 64


def _s5_kernel(u_ref, h0r_ref, h0i_ref, ar_ref, ai_ref, bmat_ref, cmat_ref, d_ref, wglu_ref,
               out_ref, hr_ref, hi_ref, ut_ref, sre_ref, sim_ref, ot_ref):
    i = pl.program_id(0)
    batch, steps, _ = u_ref.shape
    rows = batch * steps

    @pl.when(i == 0)
    def _():
        hr_ref[...] = h0r_ref[...]
        hi_ref[...] = h0i_ref[...]

    u_bm = u_ref[...].reshape(rows, SSM_WIDTH)
    for j in range(SSM_WIDTH // LANES):
        ut_ref[j] = u_bm[:, j * LANES:(j + 1) * LANES]
    u = jnp.concatenate(
        [jnp.concatenate([ut_ref[j, pl.ds(t, batch, stride=steps), :] for j in range(SSM_WIDTH // LANES)], axis=-1)
         for t in range(steps)], axis=0)
    ub = u.astype(BF16)
    blk_ch = SSM_WIDTH // SSM_BLOCKS
    blk_st = SSM_STATES // SSM_BLOCKS
    for blk in range(SSM_BLOCKS):
        bu = _dot(ub[:, blk * blk_ch:(blk + 1) * blk_ch], bmat_ref[blk])
        sre_ref[:, blk * blk_st:(blk + 1) * blk_st] = bu[:, :blk_st]
        sim_ref[:, blk * blk_st:(blk + 1) * blk_st] = bu[:, blk_st:]

    for c in range(SSM_STATES // SCAN_LANES):
        sl = slice(c * SCAN_LANES, (c + 1) * SCAN_LANES)
        a_re = ar_ref[:, sl]
        a_im = ai_ref[:, sl]

        def body(t, carry):
            h_re, h_im = carry
            r0 = pl.multiple_of(t * batch, batch)
            n_re = a_re * h_re - a_im * h_im + sre_ref[pl.ds(r0, batch), sl]
            n_im = a_re * h_im + a_im * h_re + sim_ref[pl.ds(r0, batch), sl]
            sre_ref[pl.ds(r0, batch), sl] = n_re
            sim_ref[pl.ds(r0, batch), sl] = n_im
            return n_re, n_im

        h_re, h_im = lax.fori_loop(0, steps, body, (hr_ref[:, sl], hi_ref[:, sl]), unroll=2)
        hr_ref[:, sl] = h_re
        hi_ref[:, sl] = h_im

    ys = []
    for blk in range(SSM_BLOCKS):
        st = slice(blk * blk_st, (blk + 1) * blk_st)
        ys.append(_dot(sre_ref[:, st].astype(BF16), cmat_ref[0, blk])
                  - _dot(sim_ref[:, st].astype(BF16), cmat_ref[1, blk]))
    y = jnp.concatenate(ys, axis=-1) + d_ref[...] * u
    z = jax.nn.gelu(y).astype(BF16)
    zz = _dot(z, wglu_ref[...])
    out = zz[:, :D_MODEL] * _sigmoid(zz[:, D_MODEL:])
    for j in range(D_MODEL // LANES):
        ot_ref[j] = out[:, j * LANES:(j + 1) * LANES]
    for b in range(batch):
        out_ref[b] = jnp.concatenate([ot_ref[j, pl.ds(b, steps, stride=batch), :] for j in range(D_MODEL // LANES)],
                                     axis=-1).astype(BF16)


def _s5(u, h0_re, h0_im, a_re8, a_im8, bmat, cmat, ssm_d, w_glu_bf, steps):
    batch, l, _ = u.shape
    rows = batch * steps
    st = jax.ShapeDtypeStruct((batch, SSM_STATES), F32)
    return pl.pallas_call(
        _s5_kernel,
        grid=(l // steps,),
        in_specs=[pl.BlockSpec((batch, steps, SSM_WIDTH), lambda i: (0, i, 0)),
                  _const_spec((batch, SSM_STATES)), _const_spec((batch, SSM_STATES)),
                  _const_spec((batch, SSM_STATES)), _const_spec((batch, SSM_STATES)),
                  _const_spec(bmat.shape), _const_spec(cmat.shape),
                  _const_spec((1, SSM_WIDTH)), _const_spec(w_glu_bf.shape)],
        out_specs=(pl.BlockSpec((batch, steps, D_MODEL), lambda i: (0, i, 0)),
                   pl.BlockSpec((batch, SSM_STATES), lambda i: (0, 0)),
                   pl.BlockSpec((batch, SSM_STATES), lambda i: (0, 0))),
        out_shape=(jax.ShapeDtypeStruct((batch, l, D_MODEL), BF16), st, st),
        scratch_shapes=[pltpu.VMEM((SSM_WIDTH // LANES, rows, LANES), F32),
                        pltpu.VMEM((rows, SSM_STATES), F32), pltpu.VMEM((rows, SSM_STATES), F32),
                        pltpu.VMEM((D_MODEL // LANES, rows, LANES), F32)],
        compiler_params=_cparams(("arbitrary",)),
        name="s5",
    )(u, h0_re, h0_im, a_re8, a_im8, bmat, cmat, ssm_d.reshape(1, SSM_WIDTH), w_glu_bf)


def _attn_causal_kernel(q_ref, k_ref, v_ref, o_ref, s_ref, m_ref, acc_ref, *, tq):
    i = pl.program_id(1)
    qc = lax.broadcasted_iota(jnp.int32, (tq, tq), 0) // CHUNK
    kc = lax.broadcasted_iota(jnp.int32, (tq, tq), 1) // CHUNK
    diag_mask = kc <= qc
    m_ref[...] = jnp.full(m_ref.shape, NEG_INF, F32)
    acc_ref[...] = jnp.zeros(acc_ref.shape, F32)

    def scores(j, mask):
        r0 = pl.multiple_of(j * tq, tq)
        for hd in range(MLA_HEADS):
            ks = slice(hd * HEAD_PAD, (hd + 1) * HEAD_PAD)
            s = _dot_nt(q_ref[0, :, ks], k_ref[0, pl.ds(r0, tq), ks])
            if mask is not None:
                s = jnp.where(mask, s, NEG_INF)
            s_ref[hd, j] = s
            m_ref[hd] = jnp.maximum(m_ref[hd], jnp.maximum(s[:, :LANES], s[:, LANES:]))

    def pass1(j, c):
        scores(j, None)
        return c

    lax.fori_loop(0, i, pass1, 0)
    scores(i, diag_mask)
    for hd in range(MLA_HEADS):
        m_ref[hd] = jnp.broadcast_to(jnp.max(m_ref[hd], axis=-1, keepdims=True), (tq, LANES))

    def pass2(j, c):
        r0 = pl.multiple_of(j * tq, tq)
        for hd in range(MLA_HEADS):
            ks = slice(hd * HEAD_PAD, (hd + 1) * HEAD_PAD)
            s = s_ref[hd, j]
            mb = m_ref[hd]
            p = jnp.concatenate([jnp.exp2(s[:, :LANES] - mb), jnp.exp2(s[:, LANES:] - mb)], axis=-1).astype(BF16)
            acc_ref[hd] += _dot(p, v_ref[0, pl.ds(r0, tq), ks])
        return c

    lax.fori_loop(0, i + 1, pass2, 0)
    for hd in range(MLA_HEADS):
        acc = acc_ref[hd]
        o_ref[0, :, hd * V_HEAD:(hd + 1) * V_HEAD] = (acc[:, :V_HEAD] / acc[:, V_HEAD:V_HEAD + 1]).astype(BF16)


def _attn_causal(q, k, v, tq):
    b, l, _ = q.shape
    assert tq == 2 * LANES and l % tq == 0 and tq % CHUNK == 0
    full = lambda w: pl.BlockSpec((1, l, w), lambda bi, qi: (bi, 0, 0))
    return pl.pallas_call(
        functools.partial(_attn_causal_kernel, tq=tq),
        grid=(b, l // tq),
        in_specs=[pl.BlockSpec((1, tq, QK_PAD), lambda bi, qi: (bi, qi, 0)), full(QK_PAD), full(QK_PAD)],
        out_specs=pl.BlockSpec((1, tq, MLA_WIDTH), lambda bi, qi: (bi, qi, 0)),
        out_shape=jax.ShapeDtypeStruct((b, l, MLA_WIDTH), BF16),
        scratch_shapes=[pltpu.VMEM((MLA_HEADS, l // tq, tq, tq), F32),
                        pltpu.VMEM((MLA_HEADS, tq, LANES), F32),
                        pltpu.VMEM((MLA_HEADS, tq, HEAD_PAD), F32)],
        compiler_params=_cparams(("parallel", "arbitrary")),
        name="attn_prompt",
    )(q, k, v)


def _attn_past_kernel(q_ref, kp_ref, vp_ref, kn_ref, vn_ref, o_ref):
    for hd in range(MLA_HEADS):
        ks = slice(hd * HEAD_PAD, (hd + 1) * HEAD_PAD)
        qh = q_ref[0, :, ks]
        s_past = _dot_nt(qh, kp_ref[0, :, ks])
        s_new = _dot_nt(qh, kn_ref[0, :, ks])
        m = jnp.maximum(jnp.max(s_past, axis=-1, keepdims=True), jnp.max(s_new, axis=-1, keepdims=True))
        p_past = jnp.exp2(s_past - m)
        p_new = jnp.exp2(s_new - m)
        o = _dot(p_past.astype(BF16), vp_ref[0, :, ks]) + _dot(p_new.astype(BF16), vn_ref[0, :, ks])
        o_ref[0, :, hd * V_HEAD:(hd + 1) * V_HEAD] = (o[:, :V_HEAD] / o[:, V_HEAD:V_HEAD + 1]).astype(BF16)


def _attn_past(q, k_past, v_past, k_new, v_new):
    b, lq, _ = q.shape
    lp = k_past.shape[1]
    assert lp % CHUNK == 0 and lq <= CHUNK
    spec = lambda n, w: pl.BlockSpec((1, n, w), lambda bi: (bi, 0, 0))
    return pl.pallas_call(
        _attn_past_kernel,
        grid=(b,),
        in_specs=[spec(lq, QK_PAD), spec(lp, QK_PAD), spec(lp, QK_PAD), spec(lq, QK_PAD), spec(lq, QK_PAD)],
        out_specs=spec(lq, MLA_WIDTH),
        out_shape=jax.ShapeDtypeStruct((b, lq, MLA_WIDTH), BF16),
        compiler_params=_cparams(("parallel",)),
        name="attn_sample",
    )(q, k_past, v_past, k_new, v_new)


ROUTE_LANES = LANES


def _first_argmax(v, lane, width):
    vmax = jnp.max(v, axis=-1, keepdims=True)
    idx = jnp.min(jnp.where(v == vmax, lane, width), axis=-1, keepdims=True)
    return vmax, idx


def _merge_kernel(x_ref, bra_ref, o_ref, om_ref, gate_ref, cnt0_ref, w_omla_ref, w_omem_ref, w_out_ref, g_ffn_ref,
                  w_rt_ref, b_rt_ref, x1_ref, h2_ref, route_ref, route_t_ref, cnt_ref):
    first = (pl.program_id(0) == 0) & (pl.program_id(1) == 0)

    @pl.when(first)
    def _():
        cnt_ref[...] = cnt0_ref[...]

    br_b = _dot(o_ref[0], w_omla_ref[...])
    br_c = _dot(om_ref[0], w_omem_ref[...])
    g = gate_ref[0].astype(F32)
    merged = (g[:, :D_MODEL] * bra_ref[0].astype(F32) + g[:, D_MODEL:2 * D_MODEL] * br_b
              + g[:, 2 * D_MODEL:] * br_c)
    x1 = x_ref[0] + _dot(merged.astype(BF16), w_out_ref[...])
    x1_ref[0] = x1
    h2 = x1 * _rms_scale(x1, D_MODEL) * g_ffn_ref[...]
    h2_ref[0] = h2

    h2_hi = h2.astype(BF16)
    h2_lo = (h2 - h2_hi.astype(F32)).astype(BF16)
    hi_both = _dot(h2_hi, w_rt_ref[...])
    logits = (hi_both[:, :ROUTE_LANES] + hi_both[:, ROUTE_LANES:] + _dot(h2_lo, w_rt_ref[:, :ROUTE_LANES])
              + b_rt_ref[...])
    lane = lax.broadcasted_iota(jnp.int32, logits.shape, 1).astype(F32)
    ninf = jnp.float32(-jnp.inf)
    lg = jnp.where(lane < N_EXPERT_GROUPS, logits, ninf)
    lg_max, grp = _first_argmax(lg, lane, float(ROUTE_LANES))
    p_top = 1.0 / jnp.sum(jnp.exp(lg - lg_max), axis=-1, keepdims=True)
    lo = N_EXPERT_GROUPS + grp * EXPERTS_PER_GROUP
    in_grp = (lane >= lo) & (lane < lo + EXPERTS_PER_GROUP)
    le = jnp.where(in_grp, logits, ninf)
    v1, i1 = _first_argmax(le, lane, float(ROUTE_LANES))
    v2, i2 = _first_argmax(jnp.where(lane == i1, ninf, le), lane, float(ROUTE_LANES))
    e2 = jnp.exp(v2 - v1)
    w1 = p_top / (1.0 + e2)
    w2 = p_top * e2 / (1.0 + e2)
    elane = lane + N_EXPERT_GROUPS
    oh1 = elane == i1
    oh2 = elane == i2
    onehot = jnp.where(oh1 | oh2, 1.0, 0.0)
    tl = onehot.shape[0]
    tri = jnp.where(lax.broadcasted_iota(jnp.int32, (tl, tl), 0) > lax.broadcasted_iota(jnp.int32, (tl, tl), 1),
                    1.0, 0.0).astype(BF16)
    before = _dot(tri, onehot.astype(BF16)) + cnt_ref[...]
    r1 = jnp.sum(jnp.where(oh1, before, 0.0), axis=-1, keepdims=True)
    r2 = jnp.sum(jnp.where(oh2, before, 0.0), axis=-1, keepdims=True)
    cnt_ref[...] += jnp.sum(onehot, axis=0, keepdims=True)
    cols = (i1 - N_EXPERT_GROUPS, i2 - N_EXPERT_GROUPS, w1, w2, r1, r2)
    route = jnp.zeros_like(logits)
    for k, col in enumerate(cols):
        route = jnp.where(lane == k, col, route)
    route_ref[0] = route
    route_t_ref[0] = jnp.transpose(route)[:SUBLANES]


def _merge(x, bra, o, om, gates, cnt0, wts, tl):
    b, l, _ = x.shape
    tok = lambda w: pl.BlockSpec((1, tl, w), lambda bi, li: (bi, li, 0))
    in_specs = [tok(D_MODEL), tok(D_MODEL), tok(MLA_WIDTH), tok(MEM_WIDTH), tok(N_BRANCHES * D_MODEL),
                _const_spec((1, ROUTE_LANES))] + [_const_spec(w.shape) for w in wts]
    return pl.pallas_call(
        _merge_kernel,
        grid=(b, l // tl),
        in_specs=in_specs,
        out_specs=(tok(D_MODEL), tok(D_MODEL), tok(ROUTE_LANES),
                   pl.BlockSpec((1, SUBLANES, tl), lambda bi, li: (bi * (l // tl) + li, 0, 0)),
                   pl.BlockSpec((1, ROUTE_LANES), lambda bi, li: (0, 0))),
        out_shape=(jax.ShapeDtypeStruct((b, l, D_MODEL), F32), jax.ShapeDtypeStruct((b, l, D_MODEL), F32),
                   jax.ShapeDtypeStruct((b, l, ROUTE_LANES), F32),
                   jax.ShapeDtypeStruct((b * l // tl, SUBLANES, tl), F32),
                   jax.ShapeDtypeStruct((1, ROUTE_LANES), F32)),
        compiler_params=_cparams(("arbitrary", "arbitrary")),
        name="merge",
    )(x, bra, o, om, gates, cnt0, *wts)


MOE_TILE = 256
MOE_DMA_TILE = 512


def _row_copy(src_ref, src_row, dst_ref, dst_row, sem):
    return pltpu.make_async_copy(src_ref.at[pl.ds(src_row, 1)], dst_ref.at[pl.ds(dst_row, 1)], sem)


def _tile_copy(src_ref, dst_hbm, tile, sem):
    return pltpu.make_async_copy(src_ref, dst_hbm.at[pl.ds(tile * MOE_TILE, MOE_TILE)], sem)


def _dma_cparams():
    return pltpu.CompilerParams(dimension_semantics=("arbitrary",), vmem_limit_bytes=VMEM_LIMIT,
                                disable_bounds_checks=True)


def _dispatch_kernel(last_tile_ref, n_tiles_ref, slot_ref, h2p_ref, h2s_ref, hs_hbm, zero_ref, sem, zsem,
                     *, steps_p, grid_tiles):
    i = pl.program_id(0)
    td = slot_ref.shape[2]

    @pl.when(i == 0)
    def _():
        zero_ref[...] = jnp.zeros(zero_ref.shape, zero_ref.dtype)
        for wait in (False, True):
            def pad_tile(e, c):
                @pl.when(last_tile_ref[e] >= 0)
                def _():
                    cp = _tile_copy(zero_ref, hs_hbm, last_tile_ref[e], zsem)
                    cp.wait() if wait else cp.start()
                return c

            def tail_tile(t, c):
                cp = _tile_copy(zero_ref, hs_hbm, t, zsem)
                cp.wait() if wait else cp.start()
                return c

            lax.fori_loop(0, N_EXPERTS, pad_tile, 0)
            lax.fori_loop(n_tiles_ref[0], grid_tiles, tail_tile, 0)

    def scatter(h2_ref):
        def start(r, c):
            for k in range(2):
                _row_copy(h2_ref, r, hs_hbm, slot_ref[0, k, r], sem).start(priority=k)
            return c

        def wait(r, c):
            for k in range(2):
                _row_copy(h2_ref, r, hs_hbm, slot_ref[0, k, r], sem).wait()
            return c

        lax.fori_loop(0, td, start, 0, unroll=8)
        lax.fori_loop(0, td, wait, 0, unroll=8)

    @pl.when(i < steps_p)
    def _():
        scatter(h2p_ref)

    @pl.when(i >= steps_p)
    def _():
        scatter(h2s_ref)


def _dispatch(last_tile, n_tiles, slots, h2_p, h2_s, grid_tiles, td):
    steps_p, steps_s = h2_p.shape[0] // td, h2_s.shape[0] // td
    grid_spec = pltpu.PrefetchScalarGridSpec(
        num_scalar_prefetch=2,
        grid=(steps_p + steps_s,),
        in_specs=[pl.BlockSpec((1, 2, td), lambda i, lt, nt: (i, 0, 0), memory_space=pltpu.SMEM),
                  pl.BlockSpec((td, D_MODEL), lambda i, lt, nt: (jnp.minimum(i, steps_p - 1), 0)),
                  pl.BlockSpec((td, D_MODEL), lambda i, lt, nt: (jnp.maximum(i - steps_p, 0), 0))],
        out_specs=pl.BlockSpec(memory_space=pl.ANY),
        scratch_shapes=[pltpu.VMEM((MOE_TILE, D_MODEL), F32), pltpu.SemaphoreType.DMA, pltpu.SemaphoreType.DMA],
    )
    return pl.pallas_call(
        functools.partial(_dispatch_kernel, steps_p=steps_p, grid_tiles=grid_tiles),
        grid_spec=grid_spec,
        out_shape=jax.ShapeDtypeStruct((grid_tiles * MOE_TILE, D_MODEL), F32),
        compiler_params=_dma_cparams(),
        name="dispatch",
    )(last_tile, n_tiles, slots, h2_p, h2_s)


def _experts_kernel(tile_expert_ref, n_tiles_ref, hs_ref, w1_ref, w3_ref, w2_ref, y_ref, w13_bf_ref, w2_bf_ref):
    t = pl.program_id(0)
    used = t < n_tiles_ref[0]
    new_expert = (t == 0) | (tile_expert_ref[t] != tile_expert_ref[jnp.maximum(t - 1, 0)])

    @pl.when(used & new_expert)
    def _():
        w13_bf_ref[:, :EXPERT_FF] = w1_ref[0].astype(BF16)
        w13_bf_ref[:, EXPERT_FF:] = w3_ref[0].astype(BF16)
        w2_bf_ref[...] = w2_ref[0].astype(BF16)

    @pl.when(used)
    def _():
        a = _dot(hs_ref[...].astype(BF16), w13_bf_ref[...])
        gate = a[:, :EXPERT_FF]
        hid = gate * _sigmoid(gate) * a[:, EXPERT_FF:]
        y_ref[...] = _dot(hid.astype(BF16), w2_bf_ref[...])

    @pl.when(jnp.logical_not(used))
    def _():
        y_ref[...] = jnp.zeros(y_ref.shape, y_ref.dtype)


def _experts(tile_expert, n_tiles, hs, w_e1, w_e3, w_e2):
    s = hs.shape[0]
    w_in_spec = pl.BlockSpec((1, D_MODEL, EXPERT_FF), lambda t, te, nt: (te[t], 0, 0))
    grid_spec = pltpu.PrefetchScalarGridSpec(
        num_scalar_prefetch=2,
        grid=(s // MOE_TILE,),
        in_specs=[pl.BlockSpec((MOE_TILE, D_MODEL), lambda t, te, nt: (t, 0)), w_in_spec, w_in_spec,
                  pl.BlockSpec((1, EXPERT_FF, D_MODEL), lambda t, te, nt: (te[t], 0, 0))],
        out_specs=pl.BlockSpec((MOE_TILE, D_MODEL), lambda t, te, nt: (t, 0)),
        scratch_shapes=[pltpu.VMEM((D_MODEL, 2 * EXPERT_FF), BF16), pltpu.VMEM((EXPERT_FF, D_MODEL), BF16)],
    )
    return pl.pallas_call(
        _experts_kernel,
        grid_spec=grid_spec,
        out_shape=jax.ShapeDtypeStruct((s, D_MODEL), F32),
        compiler_params=_cparams(("arbitrary",)),
        name="experts",
    )(tile_expert, n_tiles, hs, w_e1, w_e3, w_e2)


def _combine_kernel(slot_ref, x1_ref, route_ref, ye_hbm, y_ref, rows_ref, sem):
    tc = slot_ref.shape[2]

    def start(r, c):
        for k in range(2):
            _row_copy(ye_hbm, slot_ref[0, k, r], rows_ref.at[k], r, sem).start(priority=k)
        return c

    def wait(r, c):
        for k in range(2):
            _row_copy(ye_hbm, slot_ref[0, k, r], rows_ref.at[k], r, sem).wait()
        return c

    lax.fori_loop(0, tc, start, 0, unroll=8)
    lax.fori_loop(0, tc, wait, 0, unroll=8)
    route = route_ref[...]
    y = x1_ref[...]
    for k in range(2):
        gate = route[:, 2 + k:3 + k]
        y = y + gate * rows_ref[k]
    y_ref[...] = y


def _combine(slots, x1, route, ye, tc):
    n = x1.shape[0]
    return pl.pallas_call(
        _combine_kernel,
        grid=(n // tc,),
        in_specs=[pl.BlockSpec((1, 2, tc), lambda i: (i, 0, 0), memory_space=pltpu.SMEM),
                  pl.BlockSpec((tc, D_MODEL), lambda i: (i, 0)),
                  pl.BlockSpec((tc, ROUTE_LANES), lambda i: (i, 0)),
                  pl.BlockSpec(memory_space=pl.ANY)],
        out_specs=pl.BlockSpec((tc, D_MODEL), lambda i: (i, 0)),
        out_shape=jax.ShapeDtypeStruct((n, D_MODEL), F32),
        scratch_shapes=[pltpu.VMEM((2, tc, D_MODEL), F32), pltpu.SemaphoreType.DMA],
        compiler_params=_dma_cparams(),
        name="combine",
    )(slots, x1, route, ye)


def _rope_tables(pos, g, scale):
    half = QK_ROPE // 2
    inv = ROPE_BASE ** (-jnp.arange(half, dtype=F32) / half)
    ang = pos.astype(F32)[:, None] * inv[None, :]
    cos, sin = jnp.cos(ang), jnp.sin(ang)
    n = pos.shape[0]
    g1, g2 = g[QK_NOPE:QK_NOPE + half], g[QK_NOPE + half:QK_HEAD]
    pad = jnp.zeros((n, HEAD_PAD - QK_HEAD), F32)
    a = jnp.concatenate([jnp.broadcast_to(g[:QK_NOPE], (n, QK_NOPE)), g1 * cos, g2 * cos, pad], axis=-1)
    b = jnp.concatenate([jnp.zeros((n, QK_NOPE), F32), -g2 * sin, g1 * sin, pad], axis=-1)
    return a * scale, b * scale


def _pad_heads(w, per_head, keep):
    k = w.shape[0]
    w = w.reshape(k, MLA_HEADS, per_head)[:, :, :keep]
    return jnp.pad(w, ((0, 0), (0, 0), (0, HEAD_PAD - keep))).reshape(k, QK_PAD)


def _swap_rope_cols(w96):
    k = w96.shape[0]
    w = w96.reshape(k, MLA_HEADS, QK_HEAD)
    half = QK_ROPE // 2
    sw = jnp.concatenate([jnp.zeros((k, MLA_HEADS, QK_NOPE), w.dtype), w[:, :, QK_NOPE + half:],
                          w[:, :, QK_NOPE:QK_NOPE + half]], axis=-1)
    return sw.reshape(k, MLA_HEADS * QK_HEAD)


def _block_diag(w, rows_per_group, cols_per_group):
    gb = SSM_GROUPS // SSM_BLOCKS
    w = w.reshape(SSM_BLOCKS, gb, rows_per_group, cols_per_group)
    eye = jnp.eye(gb, dtype=w.dtype)
    out = jnp.einsum('bgrc,gh->bgrhc', w, eye)
    return out.reshape(SSM_BLOCKS, gb * rows_per_group, gb * cols_per_group)


def _layer(x, pos, h0_re, h0_im, mk, mv, past, cnt0, p, tl):
    b, l, _ = x.shape
    scale = math.log2(math.e) / math.sqrt(QK_HEAD)
    aq, bq = _rope_tables(pos, p["g_qn"], scale)
    ak, bk = _rope_tables(pos, p["g_kn"], 1.0)
    inproj_wts = (p["g_attn"], p["w_u"], p["w_q"], p["w_kv"], p["w_pe"], p["w_qm"], p["w_g"], p["g_qlat"],
                  p["wq_pad"], p["wq_swap"], p["g_kvlat"], p["wk_pad"], p["wv"], p["vone"], p["g_mqn"])
    u, q, k, v, c_kv, k_pe, om, gates = _inproj(x, mk, mv, (aq, bq, ak, bk),
                                                inproj_wts, tl)
    a_re = jnp.broadcast_to(p["ab_re"], (b, SSM_STATES))
    a_im = jnp.broadcast_to(p["ab_im"], (b, SSM_STATES))
    bra, h_re, h_im = _s5(u, h0_re, h0_im, a_re, a_im, p["bmat"], p["cmat"], p["ssm_d"], p["w_glu"], S5_STEPS)
    if past is None:
        o = _attn_causal(q, k, v, 256)
    else:
        past_ckv, past_kpe = past
        lp = past_ckv.shape[1]
        akp, bkp = _rope_tables(jnp.arange(lp), p["g_kn"], 1.0)
        k_past, v_past = _kvexp(past_ckv, past_kpe, akp, bkp, p["wk_pad"], p["wv"], p["vone"], 512)
        o = _attn_past(q, k_past, v_past, k, v)
    merge_wts = (p["w_o_mla"], p["w_o_mem"], p["w_out"], p["g_ffn"], p["w_rt"], p["b_rt"])
    x1, h2, route, route_t, cnt = _merge(x, bra, o, om, gates, cnt0, merge_wts, tl)
    n = b * l
    moe_in = (x1.reshape(n, D_MODEL), h2.reshape(n, D_MODEL), route.reshape(n, ROUTE_LANES), route_t, cnt)
    return moe_in, c_kv, k_pe, h_re, h_im


def _slots(route_t, offsets, tile):
    expert = route_t[:, 0:2, :].astype(jnp.int32)
    rank = route_t[:, 4:6, :].astype(jnp.int32)
    onehot = expert[..., None] == jnp.arange(N_EXPERTS, dtype=jnp.int32)
    slot = jnp.sum(jnp.where(onehot, offsets, 0), axis=-1) + rank
    if slot.shape[2] != tile:
        slot = slot.transpose(1, 0, 2).reshape(2, -1, tile).transpose(1, 0, 2)
    return slot


def kernel(x_prompt, x_sample, cache_mla_ckv, cache_mla_kpe, cache_ssm_re, cache_ssm_im, cache_mem_k, cache_mem_v, mem_prompt, g_attn, w_in, ssm_a_re, ssm_a_im, ssm_log_dt, ssm_b_re, ssm_b_im, ssm_c_re, ssm_c_im, ssm_d, w_glu, g_qlat, w_uq, g_kvlat, w_ukv, g_qn, g_kn, w_o_mla, g_mem, w_mem_kv, g_mqn, g_mkn, w_o_mem, w_out, g_ffn, w_rg, b_rg, w_re, b_re, w_e1, w_e3, w_e2):
    assert g_attn.shape[0] == 1, "single-layer step"
    bp, lp, _ = x_prompt.shape
    bs, ls, _ = x_sample.shape
    past_len = cache_mla_ckv.shape[2]
    lyr = 0

    o1 = SSM_WIDTH
    o2 = o1 + Q_LORA
    o3 = o2 + KV_LORA
    o4 = o3 + QK_ROPE
    o5 = o4 + MEM_WIDTH
    w_in_bf = w_in[lyr].astype(BF16)
    row = lambda a: a.reshape(1, -1)
    p = {
        "g_attn": row(g_attn[lyr]), "w_u": w_in_bf[:, :o1], "w_q": w_in_bf[:, o1:o2], "w_kv": w_in_bf[:, o2:o3],
        "w_qm": w_in_bf[:, o4:o5], "w_g": w_in_bf[:, o5:],
        "g_qlat": row(g_qlat[lyr]), "g_kvlat": row(g_kvlat[lyr]), "g_mqn": row(g_mqn[lyr]),
        "g_qn": g_qn[lyr], "g_kn": g_kn[lyr], "g_ffn": row(g_ffn[lyr]),
        "ssm_d": ssm_d[lyr], "w_glu": w_glu[lyr].astype(BF16),
        "w_o_mla": w_o_mla[lyr].astype(BF16), "w_o_mem": w_o_mem[lyr].astype(BF16), "w_out": w_out[lyr].astype(BF16),
    }
    wuq = w_uq[lyr]
    p["wq_pad"] = _pad_heads(wuq, QK_HEAD, QK_HEAD).astype(BF16)
    p["wq_swap"] = _pad_heads(_swap_rope_cols(wuq), QK_HEAD, QK_HEAD).astype(BF16)
    wukv = w_ukv[lyr]
    p["wk_pad"] = _pad_heads(wukv, QK_NOPE + V_HEAD, QK_NOPE).astype(BF16)
    p["wv"] = _pad_heads(jnp.roll(wukv.reshape(KV_LORA, MLA_HEADS, QK_NOPE + V_HEAD), -QK_NOPE, axis=-1)
                         .reshape(KV_LORA, -1), QK_NOPE + V_HEAD, V_HEAD).astype(BF16)
    p["vone"] = jnp.asarray((np.arange(QK_PAD) % HEAD_PAD == V_HEAD).astype(np.float32).reshape(1, QK_PAD))
    half = QK_ROPE // 2
    w_pe = w_in_bf[:, o3:o4]
    col_pad = lambda w: jnp.pad(w, ((0, 0), (QK_NOPE, HEAD_PAD - QK_HEAD)))
    p["w_pe"] = jnp.concatenate([col_pad(w_pe), col_pad(jnp.concatenate([w_pe[:, half:], w_pe[:, :half]], axis=-1))],
                                axis=-1)

    ab_re, ab_im, f_re, f_im = _zoh(ssm_a_re[lyr], ssm_a_im[lyr], ssm_log_dt[lyr])
    b_re_, b_im_ = ssm_b_re[lyr], ssm_b_im[lyr]
    bb_re = f_re[..., None] * b_re_ - f_im[..., None] * b_im_
    bb_im = f_re[..., None] * b_im_ + f_im[..., None] * b_re_
    to_cp = lambda w: jnp.swapaxes(w, 1, 2)
    p["bmat"] = jnp.concatenate([_block_diag(to_cp(bb_re), SSM_GROUP_CH, SSM_STATE),
                                 _block_diag(to_cp(bb_im), SSM_GROUP_CH, SSM_STATE)], axis=-1).astype(BF16)
    to_pc = lambda w: jnp.swapaxes(w, 1, 2)
    p["cmat"] = jnp.stack([_block_diag(to_pc(ssm_c_re[lyr]), SSM_STATE, SSM_GROUP_CH),
                           _block_diag(to_pc(ssm_c_im[lyr]), SSM_STATE, SSM_GROUP_CH)]).astype(BF16)
    p["ab_re"] = ab_re.reshape(1, SSM_STATES)
    p["ab_im"] = ab_im.reshape(1, SSM_STATES)

    w_rt = jnp.concatenate([w_rg[lyr], w_re[lyr]], axis=-1)
    w_rt = jnp.pad(w_rt, ((0, 0), (0, ROUTE_LANES - w_rt.shape[1])))
    w_rt_hi = w_rt.astype(BF16)
    p["w_rt"] = jnp.concatenate([w_rt_hi, (w_rt - w_rt_hi.astype(F32)).astype(BF16)], axis=-1)
    b_rt = jnp.concatenate([b_rg[lyr], b_re[lyr].reshape(-1)])
    p["b_rt"] = jnp.pad(b_rt, (0, ROUTE_LANES - b_rt.shape[0])).reshape(1, ROUTE_LANES)

    mk, mv, mk_bf, mv_bf = _memkv(mem_prompt.reshape(-1, D_MODEL), g_mem[lyr], w_mem_kv[lyr].astype(BF16),
                                  g_mkn[lyr])
    m_tok = mem_prompt.shape[1]
    mk3, mv3 = mk_bf.reshape(bp, m_tok, MEM_WIDTH), mv_bf.reshape(bp, m_tok, MEM_WIDTH)
    zeros = jnp.zeros((bp, SSM_STATES), F32)
    cnt0 = jnp.zeros((1, ROUTE_LANES), F32)
    (x1_p, h2p_p, route_p, route_t_p, cnt_p), ckv_p, kpe_p, sre_p, sim_p = _layer(
        x_prompt, jnp.arange(lp), zeros, zeros, mk3, mv3, None, cnt0, p, tl=PROMPT_TILE)

    (x1_s, h2p_s, route_s, route_t_s, cnt), ckv_s, kpe_s, sre_s, sim_s = _layer(
        x_sample, past_len + jnp.arange(ls), cache_ssm_re[lyr].reshape(bs, SSM_STATES),
        cache_ssm_im[lyr].reshape(bs, SSM_STATES), cache_mem_k[lyr].reshape(bs, -1, MEM_WIDTH).astype(BF16),
        cache_mem_v[lyr].reshape(bs, -1, MEM_WIDTH).astype(BF16), (cache_mla_ckv[lyr], cache_mla_kpe[lyr]),
        cnt_p, p, tl=ls)

    n_p, n_s = x1_p.shape[0], x1_s.shape[0]
    counts = cnt[0, :N_EXPERTS].astype(jnp.int32)
    padded = (counts + MOE_TILE - 1) // MOE_TILE * MOE_TILE
    ends = jnp.cumsum(padded)
    offsets = ends - padded
    max_tiles = 2 * (n_p + n_s) // MOE_TILE + N_EXPERTS
    tile_start = jnp.arange(max_tiles, dtype=jnp.int32) * MOE_TILE
    tile_expert = jnp.sum((ends[None, :] <= tile_start[:, None]).astype(jnp.int32), axis=1)
    tile_expert = jnp.minimum(tile_expert, N_EXPERTS - 1)
    n_tiles = (ends[-1:] // MOE_TILE).astype(jnp.int32)
    slots_p = _slots(route_t_p, offsets, MOE_DMA_TILE)
    slots_s = _slots(route_t_s, offsets, MOE_DMA_TILE)
    last_tile = jnp.where(padded > 0, ends // MOE_TILE - 1, -1).astype(jnp.int32)
    hs = _dispatch(last_tile, n_tiles, jnp.concatenate([slots_p, slots_s], axis=0), h2p_p, h2p_s, max_tiles,
                   MOE_DMA_TILE)
    ye = _experts(tile_expert, n_tiles, hs, w_e1[lyr], w_e3[lyr], w_e2[lyr])
    yp = _combine(slots_p, x1_p, route_p, ye, MOE_DMA_TILE).reshape(bp, lp, D_MODEL)
    ys = _combine(slots_s, x1_s, route_s, ye, MOE_DMA_TILE).reshape(bs, ls, D_MODEL)

    st = lambda a, bsz: a.reshape(1, bsz, SSM_GROUPS, SSM_STATE)
    mem_shape = (1, bp, m_tok, MEM_HEADS, MEM_HEAD)
    return (yp, ys, ckv_p[None], kpe_p[None], st(sre_p, bp), st(sim_p, bp), mk.reshape(mem_shape),
            mv.reshape(mem_shape), ckv_s[None], kpe_s[None], st(sre_s, bs), st(sim_s, bs))
```
